```python
import math
import jax, jax.numpy as jnp
from jax import lax
import numpy as np

D_MODEL = 2048
BATCH = 1
SEQ = 8192
DEPTH = 1
DEC_BATCH = 128
DEC_SEQ = 1
PAST_LEN = 2048
PAGE_SIZE = 128

MIX_WIDTH = D_MODEL
ATTN_WIDTH = MIX_WIDTH // 2
CONV_CH = MIX_WIDTH - ATTN_WIDTH
N_HEADS = 8
HEAD_DIM_V = ATTN_WIDTH // N_HEADS
HEAD_DIM_QK = HEAD_DIM_V // 2
CONV_LEN = 31
CONV_STATE = CONV_LEN - 1
N_EXPERTS = 256
TOP_K = 8
N_EXPERT_GROUPS = 8
TOPK_GROUPS = 4
EXPERT_FF = D_MODEL // 4
SHARED_FF = EXPERT_FF
ROUTED_SCALE = 2.5
ROPE_THETA = 10000.0
Q_BLOCK = 128
NORM_EPS = 1e-6
IN_COLS = 3 * ATTN_WIDTH + 2 * CONV_CH

kernel_name = 'hymba_diffattn_conformer_moe_step'


def _rms_norm(x, g):
    xf = x.astype(jnp.float32)
    y = xf * lax.rsqrt(jnp.mean(xf * xf, axis=-1, keepdims=True) + NORM_EPS)
    return (y * g.astype(jnp.float32)).astype(x.dtype)


def _adaln(c, w_ada, b_ada):
    m = jax.nn.silu(c) @ w_ada + b_ada
    return jnp.split(m, 6, axis=-1)


def _modulate(x, shift, scale):
    return x * (1 + scale[:, None, :]) + shift[:, None, :]


def _rope(x, pos):
    d = x.shape[-1]
    half = d // 2
    inv = ROPE_THETA ** (-jnp.arange(half, dtype=jnp.float32) * 2.0 / d)
    ang = pos.astype(jnp.float32)[:, None] * inv[None, :]
    cos = jnp.cos(ang)[:, None, None, :]
    sin = jnp.sin(ang)[:, None, None, :]
    xf = x.astype(jnp.float32)
    x1, x2 = xf[..., :half], xf[..., half:]
    return jnp.concatenate([x1 * cos - x2 * sin, x2 * cos + x1 * sin], axis=-1).astype(x.dtype)


def _lambda_init(layer):
    return 0.8 - 0.6 * math.exp(-0.3 * layer)


def _diff_lambda(lq1, lk1, lq2, lk2, lam_init):
    f = jnp.float32
    return (jnp.exp(jnp.sum(lq1.astype(f) * lk1.astype(f)))
            - jnp.exp(jnp.sum(lq2.astype(f) * lk2.astype(f))) + lam_init)


def _mixer_inputs(xn, pos, w_in, g_q, g_k):
    b, s, _ = xn.shape
    a, c = ATTN_WIDTH, CONV_CH
    u = xn @ w_in
    q = u[..., :a].reshape(b, s, N_HEADS, 2, HEAD_DIM_QK)
    k = u[..., a:2 * a].reshape(b, s, N_HEADS, 2, HEAD_DIM_QK)
    v = u[..., 2 * a:3 * a].reshape(b, s, N_HEADS, HEAD_DIM_V)
    glu = u[..., 3 * a:3 * a + c] * jax.nn.sigmoid(u[..., 3 * a + c:])
    q = _rope(_rms_norm(q, g_q), pos)
    k = _rope(_rms_norm(k, g_k), pos)
    return q, k, v, glu


def _diff_attn_prompt(q, k, v, lam):
    b, s = q.shape[:2]
    nb = s // Q_BLOCK
    scale = HEAD_DIM_QK ** -0.5
    qb = jnp.swapaxes(q.reshape(b, nb, Q_BLOCK, N_HEADS, 2, HEAD_DIM_QK), 0, 1)
    kpos = jnp.arange(s)

    def block(args):
        qi, bi = args
        sc = jnp.einsum('bqhcd,bkhcd->bhcqk', qi, k).astype(jnp.float32) * scale
        qpos = bi * Q_BLOCK + jnp.arange(Q_BLOCK)
        sc = jnp.where(kpos[None, :] <= qpos[:, None], sc, -jnp.inf)
        p = jax.nn.softmax(sc, axis=-1)
        pd = p[:, :, 0] - lam * p[:, :, 1]
        return jnp.einsum('bhqk,bkhd->bqhd', pd.astype(v.dtype), v)

    o = lax.map(block, (qb, jnp.arange(nb)))
    return jnp.swapaxes(o, 0, 1).reshape(b, s, N_HEADS, HEAD_DIM_V)


def _diff_attn_sample(q, k, v, cache_k, cache_v, layer, page_table, lam):
    b, s = q.shape[:2]
    n_pages = page_table.shape[1]
    past = n_pages * PAGE_SIZE
    scale = HEAD_DIM_QK ** -0.5
    pt = page_table.T

    def page_scores(ids):
        kp = cache_k[layer, ids].astype(q.dtype)
        return jnp.einsum('bqhcd,bkhcd->bhcqk', q, kp).astype(jnp.float32)

    sp = lax.map(page_scores, pt)
    sp = jnp.moveaxis(sp, 0, 4).reshape(b, N_HEADS, 2, s, past)
    sn = jnp.einsum('bqhcd,bkhcd->bhcqk', q, k).astype(jnp.float32)
    causal = jnp.arange(s)[None, :] <= jnp.arange(s)[:, None]
    sn = jnp.where(causal, sn, -jnp.inf)
    p = jax.nn.softmax(jnp.concatenate([sp, sn], axis=-1) * scale, axis=-1)
    pd = p[:, :, 0] - lam * p[:, :, 1]
    pp = jnp.moveaxis(pd[..., :past].reshape(b, N_HEADS, s, n_pages, PAGE_SIZE), 3, 0)

    def page_out(args):
        ids, pj = args
        vp = cache_v[layer, ids].astype(v.dtype)
        return jnp.einsum('bhqk,bkhd->bqhd', pj.astype(v.dtype), vp)

    o = lax.map(page_out, (pt, pp)).sum(axis=0)
    return o + jnp.einsum('bhqk,bkhd->bqhd', pd[..., past:].astype(v.dtype), v)


def _conformer_conv(glu, left, w_dw, b_dw, ln_g, ln_b):
    xp = jnp.concatenate([left.astype(glu.dtype), glu], axis=1)
    y = lax.conv_general_dilated(xp, w_dw[:, None, :].astype(xp.dtype), window_strides=(1,),
                                 padding='VALID', dimension_numbers=('NWC', 'WIO', 'NWC'),
                                 feature_group_count=CONV_CH)
    yf = y.astype(jnp.float32) + b_dw.astype(jnp.float32)
    mu = jnp.mean(yf, axis=-1, keepdims=True)
    var = jnp.mean(jnp.square(yf - mu), axis=-1, keepdims=True)
    yn = (yf - mu) * lax.rsqrt(var + NORM_EPS) * ln_g.astype(jnp.float32) + ln_b.astype(jnp.float32)
    return (yn * jax.nn.sigmoid(yn)).astype(glu.dtype), xp[:, -CONV_STATE:]


def _mixer_out(o_attn, conv_y, g_sub, lam_init, w_out):
    b, s = conv_y.shape[:2]
    o = _rms_norm(o_attn, g_sub) * (1.0 - lam_init)
    return jnp.concatenate([o.reshape(b, s, ATTN_WIDTH), conv_y], axis=-1) @ w_out


def _swiglu(x, wg, wu, wd):
    return (jax.nn.silu(x @ wg) * (x @ wu)) @ wd


def _route(h, w_router, router_bias):
    t = h.shape[0]
    s = jax.nn.sigmoid(h.astype(jnp.float32) @ w_router.astype(jnp.float32))
    sc = s + router_bias.astype(jnp.float32)
    per = N_EXPERTS // N_EXPERT_GROUPS
    gs = lax.top_k(sc.reshape(t, N_EXPERT_GROUPS, per), 2)[0].sum(-1)
    _, gi = lax.top_k(gs, TOPK_GROUPS)
    gmask = jax.nn.one_hot(gi, N_EXPERT_GROUPS, dtype=jnp.float32).sum(-2) > 0
    emask = jnp.repeat(gmask, per, axis=-1)
    _, idx = lax.top_k(jnp.where(emask, sc, -jnp.inf), TOP_K)
    w = jnp.take_along_axis(s, idx, axis=-1)
    w = w / jnp.sum(w, axis=-1, keepdims=True) * ROUTED_SCALE
    return idx.astype(jnp.int32), w


def _moe_block(n_tokens):
    target = max(1, (n_tokens * TOP_K) // N_EXPERTS)
    blk = 8
    while blk < target and blk < 128:
        blk *= 2
    return blk


def _routed_experts(h, idx, wts, w_e_gate, w_e_up, w_e_down, layer):
    t, d = h.shape
    blk = _moe_block(t)
    n_assign = t * TOP_K
    n_blocks = -(-(n_assign + N_EXPERTS * (blk - 1)) // blk)
    n_slots = n_blocks * blk
    flat_e = idx.reshape(-1)
    flat_t = jnp.arange(n_assign, dtype=jnp.int32) // TOP_K
    flat_w = wts.reshape(-1)
    order = jnp.argsort(flat_e)
    se = flat_e[order]
    counts = jnp.bincount(flat_e, length=N_EXPERTS)
    padded = (counts + blk - 1) // blk * blk
    pad_end = jnp.cumsum(padded)
    pad_start = pad_end - padded
    start = jnp.cumsum(counts) - counts
    dest = pad_start[se] + jnp.arange(n_assign) - start[se]
    slot_tok = jnp.full((n_slots,), t, jnp.int32).at[dest].set(flat_t[order])
    slot_w = jnp.zeros((n_slots,), h.dtype).at[dest].set(flat_w[order].astype(h.dtype))
    block_e = jnp.minimum(jnp.searchsorted(pad_end, jnp.arange(n_blocks) * blk, side='right'), N_EXPERTS - 1)
    h_pad = jnp.concatenate([h, jnp.zeros((1, d), h.dtype)], axis=0)

    def run(args):
        tok, e = args
        return _swiglu(h_pad[tok], w_e_gate[layer, e], w_e_up[layer, e], w_e_down[layer, e])

    ys = lax.map(run, (slot_tok.reshape(n_blocks, blk), block_e)).reshape(n_slots, d)
    return jax.ops.segment_sum(ys * slot_w[:, None], slot_tok, num_segments=t + 1)[:t]


def _moe(hn, layer, w_router, router_bias, w_e_gate, w_e_up, w_e_down, w_s_gate, w_s_up, w_s_down):
    b, s, d = hn.shape
    h = hn.reshape(b * s, d)
    idx, wts = _route(h, w_router, router_bias)
    out = _routed_experts(h, idx, wts, w_e_gate, w_e_up, w_e_down, layer) + _swiglu(h, w_s_gate, w_s_up, w_s_down)
    return out.reshape(b, s, d)


def _layer(x, c, pos, attend, conv_left, lam_init, layer, w_ada, b_ada, g_norm1, g_norm2, w_in, g_q, g_k,
           g_sub, w_dw, b_dw, ln_g, ln_b, w_out, w_router, router_bias, w_e_gate, w_e_up, w_e_down,
           w_s_gate, w_s_up, w_s_down):
    sh_a, sc_a, gt_a, sh_m, sc_m, gt_m = _adaln(c, w_ada, b_ada)
    xn = _modulate(_rms_norm(x, g_norm1), sh_a, sc_a)
    q, k, v, glu = _mixer_inputs(xn, pos, w_in, g_q, g_k)
    conv_y, conv_state = _conformer_conv(glu, conv_left, w_dw, b_dw, ln_g, ln_b)
    x = x + gt_a[:, None, :] * _mixer_out(attend(q, k, v), conv_y, g_sub, lam_init, w_out)
    hn = _modulate(_rms_norm(x, g_norm2), sh_m, sc_m)
    x = x + gt_m[:, None, :] * _moe(hn, layer, w_router, router_bias, w_e_gate, w_e_up, w_e_down,
                                     w_s_gate, w_s_up, w_s_down)
    return x, k, v, conv_state


def setup_inputs(seed: int = 0) -> dict:
    key = jax.random.key(seed)
    ks = jax.random.split(key, 40)
    f = jnp.float32

    def nrm(k, shape, scale):
        return jax.random.normal(k, shape, f) * scale

    d = D_MODEL
    n_pages = PAST_LEN // PAGE_SIZE
    n_used = DEC_BATCH * n_pages
    n_phys = n_used + max(1, n_used // 4)
    page_table = jax.random.permutation(ks[4], n_phys)[:n_used].reshape(DEC_BATCH, n_pages).astype(jnp.int32)
    return {
        'x_prompt': nrm(ks[0], (BATCH, SEQ, d), 1.0),
        'x_sample': nrm(ks[1], (DEC_BATCH, DEC_SEQ, d), 1.0),
        'c_prompt': nrm(ks[2], (BATCH, d), 1.0),
        'c_sample': nrm(ks[3], (DEC_BATCH, d), 1.0),
        'cache_k': nrm(ks[5], (DEPTH, n_phys, PAGE_SIZE, N_HEADS, 2, HEAD_DIM_QK), 1.0),
        'cache_v': nrm(ks[6], (DEPTH, n_phys, PAGE_SIZE, N_HEADS, HEAD_DIM_V), 1.0),
        'state_conv': nrm(ks[7], (DEPTH, DEC_BATCH, CONV_STATE, CONV_CH), 0.5),
        'page_table': page_table,
        'w_ada': nrm(ks[8], (DEPTH, d, 6 * d), 0.5 * d ** -0.5),
        'b_ada': nrm(ks[9], (DEPTH, 6 * d), 0.02),
        'g_norm1': 1.0 + nrm(ks[10], (DEPTH, d), 0.02),
        'g_norm2': 1.0 + nrm(ks[11], (DEPTH, d), 0.02),
        'w_in': nrm(ks[12], (DEPTH, d, IN_COLS), d ** -0.5),
        'g_q': 1.0 + nrm(ks[13], (DEPTH, HEAD_DIM_QK), 0.02),
        'g_k': 1.0 + nrm(ks[14], (DEPTH, HEAD_DIM_QK), 0.02),
        'lambda_q1': nrm(ks[15], (DEPTH, HEAD_DIM_QK), 0.1),
        'lambda_k1': nrm(ks[16], (DEPTH, HEAD_DIM_QK), 0.1),
        'lambda_q2': nrm(ks[17], (DEPTH, HEAD_DIM_QK), 0.1),
        'lambda_k2': nrm(ks[18], (DEPTH, HEAD_DIM_QK), 0.1),
        'g_sub': 1.0 + nrm(ks[19], (DEPTH, HEAD_DIM_V), 0.02),
        'w_dw': nrm(ks[20], (DEPTH, CONV_LEN, CONV_CH), CONV_LEN ** -0.5),
        'b_dw': nrm(ks[21], (DEPTH, CONV_CH), 0.02),
        'ln_g': 1.0 + nrm(ks[22], (DEPTH, CONV_CH), 0.02),
        'ln_b': nrm(ks[23], (DEPTH, CONV_CH), 0.02),
        'w_out': nrm(ks[24], (DEPTH, MIX_WIDTH, d), MIX_WIDTH ** -0.5),
        'w_router': nrm(ks[25], (DEPTH, d, N_EXPERTS), d ** -0.5),
        'router_bias': nrm(ks[26], (DEPTH, N_EXPERTS), 0.01),
        'w_e_gate': nrm(ks[27], (DEPTH, N_EXPERTS, d, EXPERT_FF), d ** -0.5),
        'w_e_up': nrm(ks[28], (DEPTH, N_EXPERTS, d, EXPERT_FF), d ** -0.5),
        'w_e_down': nrm(ks[29], (DEPTH, N_EXPERTS, EXPERT_FF, d), EXPERT_FF ** -0.5),
        'w_s_gate': nrm(ks[30], (DEPTH, d, SHARED_FF), d ** -0.5),
        'w_s_up': nrm(ks[31], (DEPTH, d, SHARED_FF), d ** -0.5),
        'w_s_down': nrm(ks[32], (DEPTH, SHARED_FF, d), SHARED_FF ** -0.5),
    }


def reference(x_prompt, x_sample, c_prompt, c_sample, cache_k, cache_v, state_conv, page_table,
              w_ada, b_ada, g_norm1, g_norm2, w_in, g_q, g_k, lambda_q1, lambda_k1, lambda_q2, lambda_k2,
              g_sub, w_dw, b_dw, ln_g, ln_b, w_out, w_router, router_bias, w_e_gate, w_e_up, w_e_down,
              w_s_gate, w_s_up, w_s_down):
    pos_p = jnp.arange(x_prompt.shape[1], dtype=jnp.int32)
    pos_s = PAST_LEN + jnp.arange(x_sample.shape[1], dtype=jnp.int32)
    y_p, y_s = x_prompt, x_sample
    kp_l, vp_l, cp_l, ks_l, vs_l, cs_l = [], [], [], [], [], []
    for l in range(DEPTH):
        lam_init = _lambda_init(l)
        lam = _diff_lambda(lambda_q1[l], lambda_k1[l], lambda_q2[l], lambda_k2[l], lam_init)
        lw = (w_ada[l], b_ada[l], g_norm1[l], g_norm2[l], w_in[l], g_q[l], g_k[l], g_sub[l], w_dw[l], b_dw[l],
              ln_g[l], ln_b[l], w_out[l], w_router[l], router_bias[l], w_e_gate, w_e_up, w_e_down,
              w_s_gate[l], w_s_up[l], w_s_down[l])
        left_p = jnp.zeros((x_prompt.shape[0], CONV_STATE, CONV_CH), x_prompt.dtype)
        y_p, k_p, v_p, c_p = _layer(y_p, c_prompt, pos_p,
                                    lambda q, k, v: _diff_attn_prompt(q, k, v, lam),
                                    left_p, lam_init, l, *lw)
        y_s, k_s, v_s, c_s = _layer(y_s, c_sample, pos_s,
                                    lambda q, k, v: _diff_attn_sample(q, k, v, cache_k, cache_v, l, page_table, lam),
                                    state_conv[l], lam_init, l, *lw)
        kp_l.append(k_p)
        vp_l.append(v_p)
        cp_l.append(c_p)
        ks_l.append(k_s)
        vs_l.append(v_s)
        cs_l.append(c_s)
    return (y_p, y_s, jnp.stack(kp_l), jnp.stack(vp_l), jnp.stack(cp_l), jnp.stack(ks_l), jnp.stack(vs_l), jnp.stack(cs_l))
```

```python
import functools
import math

import jax
import jax.numpy as jnp
from jax import lax
from jax.experimental import pallas as pl
from jax.experimental.pallas import tpu as pltpu

F32 = jnp.float32
BF16 = jnp.bfloat16
I32 = jnp.int32

D_MODEL = 2048
ATTN_WIDTH = 1024
CONV_CH = 1024
N_HEADS = 8
HEAD_DIM_V = 128
HEAD_DIM_QK = 64
CONV_LEN = 31
CONV_STATE = CONV_LEN - 1
N_EXPERTS = 256
TOP_K = 8
N_EXPERT_GROUPS = 8
GROUP_SIZE = N_EXPERTS // N_EXPERT_GROUPS
TOPK_GROUPS = 4
EXPERT_FF = 512
ROUTED_SCALE = 2.5
ROPE_THETA = 10000.0
NORM_EPS = 1e-6
PAST_LEN = 2048
PAGE_SIZE = 128
IN_COLS = 3 * ATTN_WIDTH + 2 * CONV_CH
LANES = 128
SUBLANES = 8
COL_TILE = 1024
CONV_HALO = 32
EXPERT_ROWS = 384
VMEM_LIMIT = 56 * 1024 * 1024


def _cparams(sem, vmem=VMEM_LIMIT):
    return pltpu.CompilerParams(dimension_semantics=sem, vmem_limit_bytes=vmem)


def _rms(x, eps=NORM_EPS):
    return x * lax.rsqrt(jnp.mean(x * x, axis=-1, keepdims=True) + eps)


def _silu(x):
    return x * jax.nn.sigmoid(x)


def _mod_rows(ref, per_row):
    return ref[...] if per_row else ref[0:1, :]


def _diff_lambda(lq1_ref, lk1_ref, lq2_ref, lk2_ref, lam_init):
    a = jnp.sum(lq1_ref[...] * lk1_ref[...], axis=-1, keepdims=True)
    b = jnp.sum(lq2_ref[...] * lk2_ref[...], axis=-1, keepdims=True)
    return jnp.exp(a) - jnp.exp(b) + lam_init


def _ada_kernel(c_ref, w_ref, b_ref, o_ref):
    s = _silu(c_ref[...]).astype(BF16)
    o_ref[...] = jnp.dot(s, w_ref[...].astype(BF16), preferred_element_type=F32) + b_ref[...]


def _ada(c_all, w_ada, b_ada):
    rows, d = c_all.shape
    n = w_ada.shape[1]
    tn = 1024
    return pl.pallas_call(
        _ada_kernel,
        out_shape=jax.ShapeDtypeStruct((rows, n), F32),
        grid=(n // tn,),
        in_specs=[pl.BlockSpec((rows, d), lambda j: (0, 0)),
                  pl.BlockSpec((d, tn), lambda j: (0, j)),
                  pl.BlockSpec((1, tn), lambda j: (0, j))],
        out_specs=pl.BlockSpec((rows, tn), lambda j: (0, j)),
        compiler_params=_cparams(("arbitrary",)),
        name="adaln",
    )(c_all, w_ada, b_ada)


def _qk_norm_rope(u, g, cos, sin, bd):
    lane = lax.broadcasted_iota(I32, (u.shape[0], LANES), 1)
    first_half = (lane & 32) == 0
    outs = []
    for c in range(u.shape[1] // LANES):
        x = u[:, c * LANES:(c + 1) * LANES]
        ss = jnp.dot((x * x).astype(BF16), bd, preferred_element_type=F32)
        xn = x * lax.rsqrt(ss * (1.0 / HEAD_DIM_QK) + NORM_EPS) * g
        rot = jnp.where(first_half, pltpu.roll(xn, 96, 1), pltpu.roll(xn, 32, 1))
        outs.append(xn * cos + rot * sin)
    return jnp.concatenate(outs, axis=1)


def _inproj_kernel(x_ref, sh_ref, sc_ref, g1_ref, w_ref, gq_ref, gk_ref, cos_ref, sin_ref, bd_ref,
                   q_ref, k_ref, kb_ref, v_ref, vb_ref, glu_ref, xn_scr, val_scr, *, per_row):
    j = pl.program_id(1)

    @pl.when(j == 0)
    def _():
        y = _rms(x_ref[...]) * g1_ref[...]
        xn_scr[...] = (y * (1.0 + _mod_rows(sc_ref, per_row)) + _mod_rows(sh_ref, per_row)).astype(BF16)

    u = jnp.dot(xn_scr[...], w_ref[...], preferred_element_type=F32)

    @pl.when(j == 0)
    def _():
        q = _qk_norm_rope(u, gq_ref[...], cos_ref[...], sin_ref[...], bd_ref[...]) * (HEAD_DIM_QK ** -0.5)
        lane = lax.broadcasted_iota(I32, q.shape, 1)
        map0 = (lane & HEAD_DIM_QK) == 0
        q_ref[0] = jnp.where(map0, q, 0.0).astype(BF16)
        q_ref[1] = jnp.where(map0, 0.0, q).astype(BF16)

    @pl.when(j == 1)
    def _():
        k = _qk_norm_rope(u, gk_ref[...], cos_ref[...], sin_ref[...], bd_ref[...])
        k_ref[...] = k
        kb_ref[...] = k.astype(BF16)

    @pl.when(j == 2)
    def _():
        v_ref[...] = u
        vb_ref[...] = u.astype(BF16)

    @pl.when(j == 3)
    def _():
        val_scr[...] = u

    @pl.when(j == 4)
    def _():
        glu_ref[...] = val_scr[...] * jax.nn.sigmoid(u)


def _inproj(x, mod, mod_block_rows, mod_row_block, per_row, g1, w_in_bf, gq, gk, cos, sin, bd, tm):
    t, d = x.shape
    nj = IN_COLS // COL_TILE
    row = lambda i, j: (i, 0)
    mod_spec = lambda chunk: pl.BlockSpec(
        (mod_block_rows, d), (lambda i, j: (i, chunk)) if per_row else (lambda i, j: (mod_row_block, chunk)))
    const = lambda shape: pl.BlockSpec(shape, lambda i, j: (0, 0))
    return pl.pallas_call(
        functools.partial(_inproj_kernel, per_row=per_row),
        out_shape=(jax.ShapeDtypeStruct((2, t, COL_TILE), BF16),
                   jax.ShapeDtypeStruct((t, COL_TILE), F32), jax.ShapeDtypeStruct((t, COL_TILE), BF16),
                   jax.ShapeDtypeStruct((t, COL_TILE), F32), jax.ShapeDtypeStruct((t, COL_TILE), BF16),
                   jax.ShapeDtypeStruct((t, COL_TILE), F32)),
        grid=(t // tm, nj),
        in_specs=[pl.BlockSpec((tm, d), row), mod_spec(0), mod_spec(1), const((1, d)),
                  pl.BlockSpec((d, COL_TILE), lambda i, j: (0, j)),
                  const((1, LANES)), const((1, LANES)),
                  pl.BlockSpec((tm, LANES), row), pl.BlockSpec((tm, LANES), row), const((LANES, LANES))],
        out_specs=(pl.BlockSpec((2, tm, COL_TILE), lambda i, j: (0, i, 0)),
                   pl.BlockSpec((tm, COL_TILE), row), pl.BlockSpec((tm, COL_TILE), row),
                   pl.BlockSpec((tm, COL_TILE), row), pl.BlockSpec((tm, COL_TILE), row),
                   pl.BlockSpec((tm, COL_TILE), row)),
        scratch_shapes=[pltpu.VMEM((tm, d), BF16), pltpu.VMEM((tm, COL_TILE), F32)],
        compiler_params=_cparams(("arbitrary", "arbitrary")),
        name="inproj",
    )(x, mod, mod, g1, w_in_bf, gq, gk, cos, sin, bd)


def _layernorm_swish(y, g, b):
    mu = jnp.mean(y, axis=-1, keepdims=True)
    yc = y - mu
    var = jnp.mean(yc * yc, axis=-1, keepdims=True)
    yn = yc * lax.rsqrt(var + NORM_EPS) * g + b
    return yn * jax.nn.sigmoid(yn)


def _conv_kernel(cur_ref, prev_ref, w_ref, b_ref, lg_ref, lb_ref, y_ref, xs_scr, sh_scr, *, tm, rc):
    i = pl.program_id(0)
    xs_scr[0:CONV_HALO, :] = jnp.where(i == 0, 0.0, prev_ref[...])
    xs_scr[CONV_HALO:, :] = cur_ref[...]
    n_sh = tm + CONV_HALO - SUBLANES
    for b in range(1, SUBLANES):
        sh_scr[b - 1] = xs_scr[b:b + n_sh, :]

    first = CONV_HALO - CONV_STATE

    def chunk(r, carry):
        r0 = pl.multiple_of(r * rc, rc)
        acc = jnp.zeros((rc, CONV_CH), F32)
        for j in range(CONV_LEN):
            o = first + j
            b, a = o % SUBLANES, (o // SUBLANES) * SUBLANES
            src = xs_scr if b == 0 else sh_scr.at[b - 1]
            acc = acc + w_ref[j:j + 1, :] * src[pl.ds(r0 + a, rc), :]
        y = _layernorm_swish(acc + b_ref[...], lg_ref[...], lb_ref[...])
        y_ref[pl.ds(r0, rc), :] = y.astype(y_ref.dtype)
        return carry

    lax.fori_loop(0, tm // rc, chunk, 0)


def _conv_prompt(glu, w_dw, b_dw, ln_g, ln_b, tm=256, rc=32):
    t, c = glu.shape
    per = tm // CONV_HALO
    const = lambda shape: pl.BlockSpec(shape, lambda i: (0, 0))
    return pl.pallas_call(
        functools.partial(_conv_kernel, tm=tm, rc=rc),
        out_shape=jax.ShapeDtypeStruct((t, c), BF16),
        grid=(t // tm,),
        in_specs=[pl.BlockSpec((tm, c), lambda i: (i, 0)),
                  pl.BlockSpec((CONV_HALO, c), lambda i: (jnp.maximum(i * per - 1, 0), 0)),
                  const((CONV_LEN, c)), const((1, c)), const((1, c)), const((1, c))],
        out_specs=pl.BlockSpec((tm, c), lambda i: (i, 0)),
        scratch_shapes=[pltpu.VMEM((tm + CONV_HALO, c), F32),
                        pltpu.VMEM((SUBLANES - 1, tm + CONV_HALO - SUBLANES, c), F32)],
        compiler_params=_cparams(("arbitrary",)),
        name="conv_prompt",
    )(glu, glu, w_dw, b_dw, ln_g, ln_b)


def _conv_sample_kernel(st_ref, glu_ref, w_ref, b_ref, lg_ref, lb_ref, y_ref):
    w = w_ref[...]
    acc = jnp.sum(st_ref[...] * w[None, 0:CONV_STATE, :], axis=1) + glu_ref[...] * w[CONV_STATE:CONV_LEN, :]
    y_ref[...] = _layernorm_swish(acc + b_ref[...], lg_ref[...], lb_ref[...]).astype(y_ref.dtype)


def _conv_sample(state, glu, w_dw, b_dw, ln_g, ln_b, sb=32):
    b, s, c = state.shape
    const = lambda shape: pl.BlockSpec(shape, lambda i: (0, 0))
    return pl.pallas_call(
        _conv_sample_kernel,
        out_shape=jax.ShapeDtypeStruct((b, c), BF16),
        grid=(b // sb,),
        in_specs=[pl.BlockSpec((sb, s, c), lambda i: (i, 0, 0)), pl.BlockSpec((sb, c), lambda i: (i, 0)),
                  const((CONV_LEN, c)), const((1, c)), const((1, c)), const((1, c))],
        out_specs=pl.BlockSpec((sb, c), lambda i: (i, 0)),
        compiler_params=_cparams(("arbitrary",)),
        name="conv_sample",
    )(state, glu, w_dw, b_dw, ln_g, ln_b)


def _flash_kernel(q_ref, k_ref, v_ref, lq1_ref, lk1_ref, lq2_ref, lk2_ref, gs_ref, o_ref,
                  m_scr, l_scr, acc_scr, *, tq, lam_init):
    i = pl.program_id(1)
    q = q_ref[...].reshape(2 * tq, LANES)
    m_scr[...] = jnp.full(m_scr.shape, -jnp.inf, F32)
    l_scr[...] = jnp.zeros(l_scr.shape, F32)
    acc_scr[...] = jnp.zeros(acc_scr.shape, F32)

    def step(j, masked):
        k0 = pl.multiple_of(j * tq, tq)
        k = k_ref[pl.ds(k0, tq), :]
        v = v_ref[pl.ds(k0, tq), :]
        s = lax.dot_general(q, k, (((1,), (1,)), ((), ())), preferred_element_type=F32)
        if masked:
            row = lax.broadcasted_iota(I32, s.shape, 0)
            col = lax.broadcasted_iota(I32, s.shape, 1)
            qpos = jnp.where(row >= tq, row - tq, row)
            s = jnp.where(col <= qpos, s, -jnp.inf)
        m_prev = m_scr[...]
        m_new = jnp.maximum(m_prev, jnp.max(s, axis=-1, keepdims=True))
        alpha = jnp.exp(m_prev - m_new)
        p = jnp.exp(s - m_new)
        l_scr[...] = alpha * l_scr[...] + jnp.sum(p, axis=-1, keepdims=True)
        acc_scr[...] = alpha * acc_scr[...] + jnp.dot(p.astype(BF16), v, preferred_element_type=F32)
        m_scr[...] = m_new

    def body(j, carry):
        step(j, False)
        return carry

    lax.fori_loop(0, i, body, 0)
    step(i, True)

    o = acc_scr[...] / l_scr[...]
    lam = _diff_lambda(lq1_ref, lk1_ref, lq2_ref, lk2_ref, lam_init)
    od = o[0:tq] - lam * o[tq:2 * tq]
    o_ref[...] = (_rms(od) * gs_ref[...] * (1.0 - lam_init)).astype(o_ref.dtype)


def _flash_prompt(q2, kb, vb, lq1, lk1, lq2, lk2, g_sub, lam_init, tq=256):
    _, t, _ = q2.shape
    const = lambda shape: pl.BlockSpec(shape, lambda h, i: (0, 0))
    return pl.pallas_call(
        functools.partial(_flash_kernel, tq=tq, lam_init=lam_init),
        out_shape=jax.ShapeDtypeStruct((t, ATTN_WIDTH), BF16),
        grid=(N_HEADS, t // tq),
        in_specs=[pl.BlockSpec((2, tq, LANES), lambda h, i: (0, i, h)),
                  pl.BlockSpec((t, LANES), lambda h, i: (0, h)),
                  pl.BlockSpec((t, LANES), lambda h, i: (0, h)),
                  const((1, HEAD_DIM_QK)), const((1, HEAD_DIM_QK)), const((1, HEAD_DIM_QK)),
                  const((1, HEAD_DIM_QK)), const((1, LANES))],
        out_specs=pl.BlockSpec((tq, LANES), lambda h, i: (i, h)),
        scratch_shapes=[pltpu.VMEM((2 * tq, 1), F32), pltpu.VMEM((2 * tq, 1), F32),
                        pltpu.VMEM((2 * tq, LANES), F32)],
        compiler_params=_cparams(("arbitrary", "arbitrary")),
        name="flash_prompt",
    )(q2, kb, vb, lq1, lk1, lq2, lk2, g_sub)


def _paged_copies(pt_ref, ck_hbm, cv_hbm, kbuf, vbuf, sem, b, slot, n_pages):
    copies = []
    for p in range(n_pages):
        pg = pt_ref[b, p]
        copies.append(pltpu.make_async_copy(ck_hbm.at[pg], kbuf.at[slot, p], sem.at[0, slot]))
        copies.append(pltpu.make_async_copy(cv_hbm.at[pg], vbuf.at[slot, p], sem.at[1, slot]))
    return copies


def _sattn_kernel(pt_ref, q_ref, k_ref, v_ref, lq1_ref, lk1_ref, lq2_ref, lk2_ref, gs_ref,
                  ck_hbm, cv_hbm, o_ref, kbuf, vbuf, s_scr, sem, *, n_pages, lam_init):
    b = pl.program_id(0)
    nb = pl.num_programs(0)
    slot = b % 2

    @pl.when(b == 0)
    def _():
        for c in _paged_copies(pt_ref, ck_hbm, cv_hbm, kbuf, vbuf, sem, 0, 0, n_pages):
            c.start()

    @pl.when(b + 1 < nb)
    def _():
        for c in _paged_copies(pt_ref, ck_hbm, cv_hbm, kbuf, vbuf, sem, b + 1, 1 - slot, n_pages):
            c.start()

    nr = 2 * N_HEADS
    row = lax.broadcasted_iota(I32, (nr, ATTN_WIDTH), 0)
    lane = lax.broadcasted_iota(I32, (nr, ATTN_WIDTH), 1)
    head_lanes = (lane // LANES) == (row % N_HEADS)
    q0 = q_ref[0, pl.ds(b, 1), :]
    q1 = q_ref[1, pl.ds(b, 1), :]
    qbd = jnp.where(head_lanes, jnp.where(row < N_HEADS, q0, q1), 0.0)
    qbd_bf = qbd.astype(BF16)
    s_self = jnp.sum(qbd * k_ref[pl.ds(b, 1), :], axis=-1, keepdims=True)

    for c in _paged_copies(pt_ref, ck_hbm, cv_hbm, kbuf, vbuf, sem, b, slot, n_pages):
        c.wait()

    for p in range(n_pages):
        kp = kbuf[slot, p].astype(BF16)
        s_scr[p] = lax.dot_general(qbd_bf, kp, (((1,), (1,)), ((), ())), preferred_element_type=F32)

    s = s_scr[...]
    m = jnp.maximum(jnp.max(jnp.max(s, axis=0), axis=-1, keepdims=True), s_self)
    e = jnp.exp(s - m[None])
    e_self = jnp.exp(s_self - m)
    inv_l = 1.0 / (jnp.sum(jnp.sum(e, axis=0), axis=-1, keepdims=True) + e_self)
    lam = _diff_lambda(lq1_ref, lk1_ref, lq2_ref, lk2_ref, lam_init)
    w = inv_l[0:N_HEADS]
    w1 = lam * inv_l[N_HEADS:nr]
    pd = e[:, 0:N_HEADS, :] * w[None] - e[:, N_HEADS:nr, :] * w1[None]
    pd_self = e_self[0:N_HEADS] * w - e_self[N_HEADS:nr] * w1

    acc = pd_self * v_ref[pl.ds(b, 1), :]
    for p in range(n_pages):
        acc = acc + jnp.dot(pd[p].astype(BF16), vbuf[slot, p].astype(BF16), preferred_element_type=F32)

    om = jnp.where(head_lanes[0:N_HEADS], acc, 0.0)
    ss = jnp.sum(om * om, axis=-1, keepdims=True)
    on = om * lax.rsqrt(ss * (1.0 / HEAD_DIM_V) + NORM_EPS)
    o_ref[pl.ds(b, 1), :] = jnp.sum(on, axis=0, keepdims=True) * gs_ref[...] * (1.0 - lam_init)


def _attn_sample(page_table, q2, k, v, lq1, lk1, lq2, lk2, g_sub_row, cache_k, cache_v, lam_init):
    nb, n_pages = page_table.shape
    const2 = lambda shape: pl.BlockSpec(shape, lambda b, pt: (0, 0))
    grid_spec = pltpu.PrefetchScalarGridSpec(
        num_scalar_prefetch=1,
        grid=(nb,),
        in_specs=[pl.BlockSpec((2, nb, ATTN_WIDTH), lambda b, pt: (0, 0, 0)),
                  const2((nb, ATTN_WIDTH)), const2((nb, ATTN_WIDTH)),
                  const2((1, HEAD_DIM_QK)), const2((1, HEAD_DIM_QK)), const2((1, HEAD_DIM_QK)),
                  const2((1, HEAD_DIM_QK)), const2((1, ATTN_WIDTH)),
                  pl.BlockSpec(memory_space=pl.ANY), pl.BlockSpec(memory_space=pl.ANY)],
        out_specs=const2((nb, ATTN_WIDTH)),
        scratch_shapes=[pltpu.VMEM((2, n_pages, PAGE_SIZE, ATTN_WIDTH), F32),
                        pltpu.VMEM((2, n_pages, PAGE_SIZE, ATTN_WIDTH), F32),
                        pltpu.VMEM((n_pages, 2 * N_HEADS, PAGE_SIZE), F32),
                        pltpu.SemaphoreType.DMA((2, 2))],
    )
    return pl.pallas_call(
        functools.partial(_sattn_kernel, n_pages=n_pages, lam_init=lam_init),
        out_shape=jax.ShapeDtypeStruct((nb, ATTN_WIDTH), F32),
        grid_spec=grid_spec,
        compiler_params=_cparams(("arbitrary",)),
        name="attn_sample",
    )(page_table, q2, k, v, lq1, lk1, lq2, lk2, g_sub_row, cache_k, cache_v)


def _outproj_kernel(on_ref, cy_ref, x_ref, gt_ref, sh_ref, sc_ref, g2_ref, wa_ref, wc_ref, x1_ref, hn_ref, *,
                    per_row):
    mix = (jnp.dot(on_ref[...].astype(BF16), wa_ref[...], preferred_element_type=F32)
           + jnp.dot(cy_ref[...], wc_ref[...], preferred_element_type=F32))
    x1 = x_ref[...] + _mod_rows(gt_ref, per_row) * mix
    x1_ref[...] = x1
    y = _rms(x1) * g2_ref[...]
    hn_ref[...] = y * (1.0 + _mod_rows(sc_ref, per_row)) + _mod_rows(sh_ref, per_row)


def _outproj(on, cy, x, mod, mod_block_rows, mod_row_block, per_row, g2, wo_a, wo_c, tm):
    t, d = x.shape
    row = lambda i: (i, 0)
    mod_spec = lambda chunk: pl.BlockSpec(
        (mod_block_rows, d), (lambda i: (i, chunk)) if per_row else (lambda i: (mod_row_block, chunk)))
    const = lambda shape: pl.BlockSpec(shape, lambda i: (0, 0))
    return pl.pallas_call(
        functools.partial(_outproj_kernel, per_row=per_row),
        out_shape=(jax.ShapeDtypeStruct((t, d), F32), jax.ShapeDtypeStruct((t, d), F32)),
        grid=(t // tm,),
        in_specs=[pl.BlockSpec((tm, ATTN_WIDTH), row), pl.BlockSpec((tm, CONV_CH), row), pl.BlockSpec((tm, d), row),
                  mod_spec(2), mod_spec(3), mod_spec(4), const((1, d)),
                  const((ATTN_WIDTH, d)), const((CONV_CH, d))],
        out_specs=(pl.BlockSpec((tm, d), row), pl.BlockSpec((tm, d), row)),
        compiler_params=_cparams(("arbitrary",)),
        name="outproj",
    )(on, cy, x, mod, mod, mod, g2, wo_a, wo_c)


def _first_index_of_max(x, idx, axis):
    m = jnp.max(x, axis=axis, keepdims=True)
    return m, jnp.min(jnp.where(x == m, idx, float(N_EXPERTS)), axis=axis, keepdims=True)


def _route_kernel(hp_ref, hs_ref, wrh_ref, wrl_ref, bias_ref, tri_ref, idx_ref, wts_ref, pos_ref, cnt_ref, run_scr,
                  *, n_prompt_tiles):
    i = pl.program_id(0)

    @pl.when(i == 0)
    def _():
        run_scr[...] = jnp.zeros(run_scr.shape, F32)

    h = jnp.where(i < n_prompt_tiles, hp_ref[...], hs_ref[...])
    h_hi = h.astype(BF16)
    h_lo = (h - h_hi.astype(F32)).astype(BF16)
    nt = (((1,), (1,)), ((), ()))
    logits = (lax.dot_general(wrh_ref[...], h_hi, nt, preferred_element_type=F32)
              + lax.dot_general(wrh_ref[...], h_lo, nt, preferred_element_type=F32)
              + lax.dot_general(wrl_ref[...], h_hi, nt, preferred_element_type=F32))
    s = jax.nn.sigmoid(logits)
    sc = s + bias_ref[...]
    tm = s.shape[1]
    neg = -jnp.inf

    sc3 = sc.reshape(N_EXPERT_GROUPS, GROUP_SIZE, tm)
    in_grp = lax.broadcasted_iota(I32, sc3.shape, 1).astype(F32)
    m1, i1 = _first_index_of_max(sc3, in_grp, 1)
    m2 = jnp.max(jnp.where(in_grp == i1, neg, sc3), axis=1, keepdims=True)
    gs = (m1 + m2).reshape(N_EXPERT_GROUPS, tm)

    gidx = lax.broadcasted_iota(I32, gs.shape, 0).astype(F32)
    gsel = jnp.zeros(gs.shape, F32)
    for _ in range(TOPK_GROUPS):
        _, gi = _first_index_of_max(gs, gidx, 0)
        hit = gidx == gi
        gsel = jnp.where(hit, 1.0, gsel)
        gs = jnp.where(hit, neg, gs)
    emask = jnp.broadcast_to(gsel[:, None, :], sc3.shape).reshape(N_EXPERTS, tm)

    cand = jnp.where(emask > 0.0, sc, neg)
    eidx = lax.broadcasted_iota(I32, cand.shape, 0).astype(F32)
    onehot = jnp.zeros(cand.shape, F32)
    hits, idxs, ws = [], [], []
    for _ in range(TOP_K):
        _, ei = _first_index_of_max(cand, eidx, 0)
        hit = eidx == ei
        hits.append(hit)
        idxs.append(ei)
        ws.append(jnp.sum(jnp.where(hit, s, 0.0), axis=0, keepdims=True))
        cand = jnp.where(hit, neg, cand)
        onehot = jnp.where(hit, 1.0, onehot)
    w = jnp.concatenate(ws, axis=0)
    wts_ref[...] = w / jnp.sum(w, axis=0, keepdims=True) * ROUTED_SCALE
    idx_ref[...] = jnp.concatenate(idxs, axis=0).astype(I32)

    rank = run_scr[...] + jnp.dot(onehot.astype(BF16), tri_ref[...], preferred_element_type=F32)
    pos = [jnp.sum(jnp.where(hit, rank, 0.0), axis=0, keepdims=True) for hit in hits]
    pos_ref[...] = jnp.concatenate(pos, axis=0).astype(I32)
    run_scr[...] = run_scr[...] + jnp.sum(onehot, axis=1, keepdims=True)
    cnt_ref[...] = run_scr[...].astype(I32)


def _route(hn_p, hn_s, wr_hi, wr_lo, bias_col, tri, tm):
    (tp, d), ts = hn_p.shape, hn_s.shape[0]
    t = tp + ts
    npt, nst = tp // tm, ts // tm
    const = lambda shape: pl.BlockSpec(shape, lambda i: (0, 0))
    tok = pl.BlockSpec((TOP_K, tm), lambda i: (0, i))
    return pl.pallas_call(
        functools.partial(_route_kernel, n_prompt_tiles=npt),
        out_shape=(jax.ShapeDtypeStruct((TOP_K, t), I32), jax.ShapeDtypeStruct((TOP_K, t), F32),
                   jax.ShapeDtypeStruct((TOP_K, t), I32), jax.ShapeDtypeStruct((N_EXPERTS, 1), I32)),
        grid=(npt + nst,),
        in_specs=[pl.BlockSpec((tm, d), lambda i: (jnp.minimum(i, npt - 1), 0)),
                  pl.BlockSpec((tm, d), lambda i: (jnp.clip(i - npt, 0, nst - 1), 0)),
                  const((N_EXPERTS, d)), const((N_EXPERTS, d)), const((N_EXPERTS, 1)), const((tm, tm))],
        out_specs=(tok, tok, tok, const((N_EXPERTS, 1))),
        scratch_shapes=[pltpu.VMEM((N_EXPERTS, 1), F32)],
        compiler_params=_cparams(("arbitrary",)),
        name="route",
    )(hn_p, hn_s, wr_hi, wr_lo, bias_col, tri)


SLOT_TOKEN_BITS = 14


def _expert_kernel(be_ref, nv_ref, slots_ref, nslots_ref, hn_hbm, wg_ref, wu_ref, wd_ref, ys_hbm,
                   xbuf, ybuf, x2d, wgb, wub, wdb, sem, *, rows):
    b = pl.program_id(0)
    nblk = pl.num_programs(0)
    slot = b % 2
    n_cur = nv_ref[b]
    n_next = nv_ref[jnp.minimum(b + 1, nblk - 1)]
    n_prev2 = nv_ref[jnp.maximum(b - 2, 0)]

    def gather(tbl_ref, n, dst_slot):
        def issue(r, carry):
            tok = tbl_ref[0, 0, r] & ((1 << SLOT_TOKEN_BITS) - 1)
            pltpu.make_async_copy(hn_hbm.at[pl.ds(tok, 1)], xbuf.at[pl.ds(dst_slot * rows + r, 1)],
                                  sem.at[0, dst_slot]).start()
            return carry
        lax.fori_loop(0, n, issue, 0)

    def wait_rows(hbm, buf, s, n, which):
        @pl.when(n > 0)
        def _():
            pltpu.make_async_copy(hbm.at[pl.ds(0, n)], buf.at[pl.ds(s * rows, n)], sem.at[which, s]).wait()

    @pl.when(b == 0)
    def _():
        xbuf[...] = jnp.zeros(xbuf.shape, F32)
        gather(slots_ref, n_cur, 0)

    @pl.when(b + 1 < nblk)
    def _():
        gather(nslots_ref, n_next, 1 - slot)

    @pl.when(b >= 2)
    def _():
        wait_rows(ys_hbm, ybuf, slot, n_prev2, 1)

    @pl.when(n_cur > 0)
    def _():
        changed = jnp.logical_or(b == 0, be_ref[b] != be_ref[jnp.maximum(b - 1, 0)])

        @pl.when(changed)
        def _():
            wgb[...] = wg_ref[...].astype(BF16)
            wub[...] = wu_ref[...].astype(BF16)
            wdb[...] = wd_ref[...].astype(BF16)

        wait_rows(hn_hbm, xbuf, slot, n_cur, 0)
        base = pl.multiple_of(slot * rows, rows)
        x2d[...] = xbuf[pl.ds(base, rows)].reshape(rows, x2d.shape[1])
        x = x2d[...].astype(BF16)
        g = jnp.dot(x, wgb[...], preferred_element_type=F32)
        u = jnp.dot(x, wub[...], preferred_element_type=F32)
        hmid = (_silu(g) * u).astype(BF16)
        y = jnp.dot(hmid, wdb[...], preferred_element_type=F32)
        ybuf[pl.ds(base, rows)] = y.reshape(rows, 1, y.shape[1])

        def issue(r, carry):
            dst = lax.shift_right_logical(slots_ref[0, 0, r], SLOT_TOKEN_BITS)
            pltpu.make_async_copy(ybuf.at[pl.ds(base + r, 1)], ys_hbm.at[pl.ds(dst, 1)], sem.at[1, slot]).start()
            return carry
        lax.fori_loop(0, n_cur, issue, 0)

    @pl.when(b == nblk - 1)
    def _():
        wait_rows(ys_hbm, ybuf, slot, n_cur, 1)

        @pl.when(b >= 1)
        def _():
            wait_rows(ys_hbm, ybuf, 1 - slot, nv_ref[jnp.maximum(b - 1, 0)], 1)


def _experts(block_e, n_valid, slots, hn3, w_gate, w_up, w_down, layer, n_rows_out, rows):
    nblk = block_e.shape[0]
    d = hn3.shape[2]
    wspec = lambda shape: pl.BlockSpec((None, None) + shape, lambda b, be, nv: (layer, be[b], 0, 0))
    grid_spec = pltpu.PrefetchScalarGridSpec(
        num_scalar_prefetch=2,
        grid=(nblk,),
        in_specs=[pl.BlockSpec((1, 1, rows), lambda b, be, nv: (b, 0, 0), memory_space=pltpu.SMEM),
                  pl.BlockSpec((1, 1, rows), lambda b, be, nv: (jnp.minimum(b + 1, nblk - 1), 0, 0),
                               memory_space=pltpu.SMEM),
                  pl.BlockSpec(memory_space=pl.ANY),
                  wspec((d, EXPERT_FF)), wspec((d, EXPERT_FF)), wspec((EXPERT_FF, d))],
        out_specs=pl.BlockSpec(memory_space=pl.ANY),
        scratch_shapes=[pltpu.VMEM((2 * rows, 1, d), F32), pltpu.VMEM((2 * rows, 1, d), F32),
                        pltpu.VMEM((rows, d), F32),
                        pltpu.VMEM((d, EXPERT_FF), BF16), pltpu.VMEM((d, EXPERT_FF), BF16),
                        pltpu.VMEM((EXPERT_FF, d), BF16), pltpu.SemaphoreType.DMA((2, 2))],
    )
    return pl.pallas_call(
        functools.partial(_expert_kernel, rows=rows),
        out_shape=jax.ShapeDtypeStruct((n_rows_out, 1, d), F32),
        grid_spec=grid_spec,
        compiler_params=_cparams(("arbitrary",)),
        name="experts",
    )(block_e, n_valid, slots, slots, hn3, w_gate, w_up, w_down)


def _dispatch_tables(idx_t, pos_t, counts, rows, n_blocks, k_stride):
    t = idx_t.shape[1]
    nblk_e = (counts + rows - 1) // rows
    blk_end = jnp.cumsum(nblk_e)
    blk_start = blk_end - nblk_e
    dest = (blk_start * rows)[idx_t] + pos_t
    tok = jnp.broadcast_to(jnp.arange(t, dtype=I32)[None, :], (TOP_K, t))
    out_row = jnp.arange(TOP_K, dtype=I32)[:, None] * k_stride + tok
    packed = out_row * (1 << SLOT_TOKEN_BITS) + tok
    slots = jnp.zeros((n_blocks * rows,), I32).at[dest.reshape(-1)].set(packed.reshape(-1))
    blocks = jnp.arange(n_blocks, dtype=I32)
    block_e = jnp.minimum(jnp.searchsorted(blk_end, blocks, side="right"), N_EXPERTS - 1).astype(I32)
    n_valid = jnp.clip(counts[block_e] - (blocks - blk_start[block_e]) * rows, 0, rows).astype(I32)
    return block_e, n_valid, slots.reshape(n_blocks, 1, rows)


def _combine_kernel(*refs, per_row):
    ys_refs = refs[:TOP_K]
    wt_ref, hn_ref, x1_ref, gt_ref, wg_ref, wu_ref, wd_ref, o_ref, y2d = refs[TOP_K:]
    h = hn_ref[...].astype(BF16)
    g = jnp.dot(h, wg_ref[...], preferred_element_type=F32)
    u = jnp.dot(h, wu_ref[...], preferred_element_type=F32)
    moe = jnp.dot((_silu(g) * u).astype(BF16), wd_ref[...], preferred_element_type=F32)
    wt = wt_ref[...]
    for k in range(TOP_K):
        y2d[...] = ys_refs[k][...].reshape(y2d.shape)
        moe = moe + wt[:, k:k + 1] * y2d[...]
    o_ref[...] = x1_ref[...] + _mod_rows(gt_ref, per_row) * moe


def _combine(ys, wts, hn, x1, mod, mod_block_rows, mod_row_block, per_row, ws_g, ws_u, ws_d, tm, token0):
    t, d = hn.shape
    k_stride = ys.shape[0] // TOP_K
    row = lambda i: (i, 0)
    const = lambda shape: pl.BlockSpec(shape, lambda i: (0, 0))
    gt_spec = pl.BlockSpec((mod_block_rows, d), (lambda i: (i, 5)) if per_row else (lambda i: (mod_row_block, 5)))
    ys_spec = lambda k: pl.BlockSpec((pl.Element(tm), pl.Element(1), pl.Element(d)),
                                     lambda i: (k * k_stride + token0 + i * tm, 0, 0))
    return pl.pallas_call(
        functools.partial(_combine_kernel, per_row=per_row),
        out_shape=jax.ShapeDtypeStruct((t, d), F32),
        grid=(t // tm,),
        in_specs=[ys_spec(k) for k in range(TOP_K)] + [
            pl.BlockSpec((tm, TOP_K), lambda i: (i + token0 // tm, 0)), pl.BlockSpec((tm, d), row),
            pl.BlockSpec((tm, d), row),
            gt_spec, const((d, EXPERT_FF)), const((d, EXPERT_FF)), const((EXPERT_FF, d))],
        out_specs=pl.BlockSpec((tm, d), row),
        scratch_shapes=[pltpu.VMEM((tm, d), F32)],
        compiler_params=_cparams(("arbitrary",)),
        name="combine",
    )(*([ys] * TOP_K), wts, hn, x1, mod, ws_g, ws_u, ws_d)


def _rope_tables(pos):
    half = HEAD_DIM_QK // 2
    inv = ROPE_THETA ** (-jnp.arange(half, dtype=F32) * 2.0 / HEAD_DIM_QK)
    ang = pos.astype(F32)[:, None] * inv[None, :]
    cos, sin = jnp.cos(ang), jnp.sin(ang)
    return jnp.tile(jnp.concatenate([cos, cos], axis=1), (1, 2)), jnp.tile(jnp.concatenate([-sin, sin], axis=1), (1, 2))


def _n_expert_blocks(n_assign, rows):
    return -(-(n_assign + N_EXPERTS * (rows - 1)) // rows)


def kernel(x_prompt, x_sample, c_prompt, c_sample, cache_k, cache_v, state_conv, page_table, w_ada, b_ada, g_norm1, g_norm2, w_in, g_q, g_k, lambda_q1, lambda_k1, lambda_q2, lambda_k2, g_sub, w_dw, b_dw, ln_g, ln_b, w_out, w_router, router_bias, w_e_gate, w_e_up, w_e_down, w_s_gate, w_s_up, w_s_down):
    layer = 0
    lam_init = 0.8 - 0.6 * math.exp(-0.3 * layer)
    d = D_MODEL
    tp = x_prompt.shape[1]
    ns = x_sample.shape[0]
    t_all = tp + ns
    xp = x_prompt.reshape(tp, d)
    xs = x_sample.reshape(ns, d)

    row2 = lambda a: a[layer].reshape(1, -1)
    w_in_bf = w_in[layer].astype(BF16)
    wo_a = w_out[layer, :ATTN_WIDTH].astype(BF16)
    wo_c = w_out[layer, ATTN_WIDTH:].astype(BF16)
    ws_g, ws_u, ws_d = (w[layer].astype(BF16) for w in (w_s_gate, w_s_up, w_s_down))
    wr_t = w_router[layer].T
    wr_hi = wr_t.astype(BF16)
    wr_lo = (wr_t - wr_hi.astype(F32)).astype(BF16)
    gq = jnp.tile(row2(g_q), (1, 2))
    gk = jnp.tile(row2(g_k), (1, 2))
    g_sub_row = jnp.tile(row2(g_sub), (1, N_HEADS))
    lane = jnp.arange(LANES)
    bd = (lane[:, None] // HEAD_DIM_QK == lane[None, :] // HEAD_DIM_QK).astype(BF16)
    lams = [row2(a) for a in (lambda_q1, lambda_k1, lambda_q2, lambda_k2)]

    c_all = jnp.concatenate([c_sample, c_prompt, jnp.zeros((SUBLANES - 1, d), F32)], axis=0)
    mod = _ada(c_all, w_ada[layer], row2(b_ada))
    p_mod = dict(mod_block_rows=SUBLANES, mod_row_block=ns // SUBLANES, per_row=False)
    s_mod = dict(mod_block_rows=ns, mod_row_block=0, per_row=True)

    cos_p, sin_p = _rope_tables(jnp.arange(tp, dtype=I32))
    cos_s, sin_s = _rope_tables(jnp.full((ns,), PAST_LEN, I32))

    q_p, k_p, kb_p, v_p, vb_p, glu_p = _inproj(xp, mod, g1=row2(g_norm1), w_in_bf=w_in_bf, gq=gq, gk=gk,
                                               cos=cos_p, sin=sin_p, bd=bd, tm=512, **p_mod)
    cy_p = _conv_prompt(glu_p, w_dw[layer], row2(b_dw), row2(ln_g), row2(ln_b))
    on_p = _flash_prompt(q_p, kb_p, vb_p, *lams, row2(g_sub), lam_init)

    q_s, k_s, _, v_s, _, glu_s = _inproj(xs, mod, g1=row2(g_norm1), w_in_bf=w_in_bf, gq=gq, gk=gk,
                                         cos=cos_s, sin=sin_s, bd=bd, tm=ns, **s_mod)
    cy_s = _conv_sample(state_conv[layer], glu_s, w_dw[layer], row2(b_dw), row2(ln_g), row2(ln_b))
    n_phys = cache_k.shape[1]
    on_s = _attn_sample(page_table + layer * n_phys, q_s.astype(F32), k_s, v_s, *lams, g_sub_row,
                        cache_k.reshape(-1, PAGE_SIZE, ATTN_WIDTH), cache_v.reshape(-1, PAGE_SIZE, ATTN_WIDTH),
                        lam_init)

    x1_p, hn_p = _outproj(on_p, cy_p, xp, mod, g2=row2(g_norm2), wo_a=wo_a, wo_c=wo_c, tm=512, **p_mod)
    x1_s, hn_s = _outproj(on_s, cy_s, xs, mod, g2=row2(g_norm2), wo_a=wo_a, wo_c=wo_c, tm=ns, **s_mod)

    tm_r = ns
    tri = (jnp.arange(tm_r)[:, None] < jnp.arange(tm_r)[None, :]).astype(BF16)
    idx_t, wts_t, pos_t, counts = _route(hn_p, hn_s, wr_hi, wr_lo, router_bias[layer].reshape(-1, 1), tri, tm_r)
    n_blocks = _n_expert_blocks(TOP_K * t_all, EXPERT_ROWS)
    block_e, n_valid, slots = _dispatch_tables(idx_t, pos_t, counts.reshape(-1), EXPERT_ROWS, n_blocks, t_all)
    hn3 = jnp.concatenate([hn_p, hn_s], axis=0).reshape(t_all, 1, d)
    ys = _experts(block_e, n_valid, slots, hn3, w_e_gate, w_e_up, w_e_down, layer, TOP_K * t_all, EXPERT_ROWS)
    wts = wts_t.T

    y_p = _combine(ys, wts, hn_p, x1_p, mod, ws_g=ws_g, ws_u=ws_u, ws_d=ws_d, tm=256, token0=0, **p_mod)
    y_s = _combine(ys, wts, hn_s, x1_s, mod, ws_g=ws_g, ws_u=ws_u, ws_d=ws_d, tm=ns, token0=tp, **s_mod)

    conv_p = glu_p[tp - CONV_STATE:]
    conv_s = jnp.concatenate([state_conv[layer][:, 1:], glu_s[:, None, :]], axis=1)
    return (y_p.reshape(1, tp, d), y_s.reshape(ns, 1, d),
            k_p.reshape(1, 1, tp, N_HEADS, 2, HEAD_DIM_QK), v_p.reshape(1, 1, tp, N_HEADS, HEAD_DIM_V),
            conv_p.reshape(1, 1, CONV_STATE, CONV_CH),
            k_s.reshape(1, ns, 1, N_HEADS, 2, HEAD_DIM_QK), v_s.reshape(1, ns, 1, N_HEADS, HEAD_DIM_V),
            conv_s.reshape(1, ns, CONV_STATE, CONV_CH))
```

```python
import functools
import math

import jax
import jax.numpy as jnp
from jax import lax
from jax.experimental import pallas as pl
from jax.experimental.pallas import tpu as pltpu

F32 = jnp.float32
BF16 = jnp.bfloat16
I32 = jnp.int32

D_MODEL = 2048
ATTN_WIDTH = 1024
CONV_CH = 1024
N_HEADS = 8
HEAD_DIM_V = 128
HEAD_DIM_QK = 64
CONV_LEN = 31
CONV_STATE = CONV_LEN - 1
N_EXPERTS = 256
TOP_K = 8
N_EXPERT_GROUPS = 8
GROUP_SIZE = N_EXPERTS // N_EXPERT_GROUPS
TOPK_GROUPS = 4
EXPERT_FF = 512
ROUTED_SCALE = 2.5
ROPE_THETA = 10000.0
NORM_EPS = 1e-6
PAST_LEN = 2048
PAGE_SIZE = 128
IN_COLS = 3 * ATTN_WIDTH + 2 * CONV_CH
LANES = 128
SUBLANES = 8
COL_TILE = 1024
CONV_HALO = 32
EXPERT_ROWS = 320
VMEM_LIMIT = 56 * 1024 * 1024
Q_SCALE = HEAD_DIM_QK ** -0.5 * math.log2(math.e)


def _cparams(sem, vmem=VMEM_LIMIT):
    return pltpu.CompilerParams(dimension_semantics=sem, vmem_limit_bytes=vmem)


def _rms(x, eps=NORM_EPS):
    return x * lax.rsqrt(jnp.mean(x * x, axis=-1, keepdims=True) + eps)


def _silu(x):
    return x * jax.nn.sigmoid(x)


def _mod_rows(ref, per_row):
    return ref[...] if per_row else ref[0:1, :]


def _diff_lambda(lq1_ref, lk1_ref, lq2_ref, lk2_ref, lam_init):
    a = jnp.sum(lq1_ref[...] * lk1_ref[...], axis=-1, keepdims=True)
    b = jnp.sum(lq2_ref[...] * lk2_ref[...], axis=-1, keepdims=True)
    return jnp.exp(a) - jnp.exp(b) + lam_init


def _ada_kernel(c_ref, w_ref, b_ref, o_ref):
    s = _silu(c_ref[...]).astype(BF16)
    o_ref[...] = jnp.dot(s, w_ref[...].astype(BF16), preferred_element_type=F32) + b_ref[...]


def _ada(c_all, w_ada, b_ada):
    rows, d = c_all.shape
    n = w_ada.shape[1]
    tn = 1024
    return pl.pallas_call(
        _ada_kernel,
        out_shape=jax.ShapeDtypeStruct((rows, n), F32),
        grid=(n // tn,),
        in_specs=[pl.BlockSpec((rows, d), lambda j: (0, 0)),
                  pl.BlockSpec((d, tn), lambda j: (0, j)),
                  pl.BlockSpec((1, tn), lambda j: (0, j))],
        out_specs=pl.BlockSpec((rows, tn), lambda j: (0, j)),
        compiler_params=_cparams(("arbitrary",)),
        name="adaln",
    )(c_all, w_ada, b_ada)


def _qk_norm_rope(u, g, cos, sin, bd):
    lane = lax.broadcasted_iota(I32, (u.shape[0], LANES), 1)
    first_half = (lane & 32) == 0
    outs = []
    for c in range(u.shape[1] // LANES):
        x = u[:, c * LANES:(c + 1) * LANES]
        ss = jnp.dot((x * x).astype(BF16), bd, preferred_element_type=F32)
        xn = x * lax.rsqrt(ss * (1.0 / HEAD_DIM_QK) + NORM_EPS) * g
        rot = jnp.where(first_half, pltpu.roll(xn, 96, 1), pltpu.roll(xn, 32, 1))
        outs.append(xn * cos + rot * sin)
    return jnp.concatenate(outs, axis=1)


def _inproj_kernel(x_ref, sh_ref, sc_ref, g1_ref, w_ref, gq_ref, gk_ref, cos_ref, sin_ref, bd_ref,
                   q_ref, k_ref, kb_ref, v_ref, vb_ref, glu_ref, xn_scr, val_scr, *, per_row):
    j = pl.program_id(1)

    @pl.when(j == 0)
    def _():
        y = _rms(x_ref[...]) * g1_ref[...]
        xn_scr[...] = (y * (1.0 + _mod_rows(sc_ref, per_row)) + _mod_rows(sh_ref, per_row)).astype(BF16)

    u = jnp.dot(xn_scr[...], w_ref[...], preferred_element_type=F32)

    @pl.when(j == 0)
    def _():
        q = _qk_norm_rope(u, gq_ref[...], cos_ref[...], sin_ref[...], bd_ref[...]) * Q_SCALE
        lane = lax.broadcasted_iota(I32, q.shape, 1)
        map0 = (lane & HEAD_DIM_QK) == 0
        q_ref[0] = jnp.where(map0, q, 0.0).astype(BF16)
        q_ref[1] = jnp.where(map0, 0.0, q).astype(BF16)

    @pl.when(j == 1)
    def _():
        k = _qk_norm_rope(u, gk_ref[...], cos_ref[...], sin_ref[...], bd_ref[...])
        k_ref[...] = k
        kb_ref[...] = k.astype(BF16)

    @pl.when(j == 2)
    def _():
        v_ref[...] = u
        vb_ref[...] = u.astype(BF16)

    @pl.when(j == 3)
    def _():
        val_scr[...] = u

    @pl.when(j == 4)
    def _():
        glu_ref[...] = val_scr[...] * jax.nn.sigmoid(u)


def _inproj(x, mod, mod_block_rows, mod_row_block, per_row, g1, w_in_bf, gq, gk, cos, sin, bd, tm):
    t, d = x.shape
    nj = IN_COLS // COL_TILE
    row = lambda i, j: (i, 0)
    mod_spec = lambda chunk: pl.BlockSpec(
        (mod_block_rows, d), (lambda i, j: (i, chunk)) if per_row else (lambda i, j: (mod_row_block, chunk)))
    const = lambda shape: pl.BlockSpec(shape, lambda i, j: (0, 0))
    return pl.pallas_call(
        functools.partial(_inproj_kernel, per_row=per_row),
        out_shape=(jax.ShapeDtypeStruct((2, t, COL_TILE), BF16),
                   jax.ShapeDtypeStruct((t, COL_TILE), F32), jax.ShapeDtypeStruct((t, COL_TILE), BF16),
                   jax.ShapeDtypeStruct((t, COL_TILE), F32), jax.ShapeDtypeStruct((t, COL_TILE), BF16),
                   jax.ShapeDtypeStruct((t, COL_TILE), F32)),
        grid=(t // tm, nj),
        in_specs=[pl.BlockSpec((tm, d), row), mod_spec(0), mod_spec(1), const((1, d)),
                  pl.BlockSpec((d, COL_TILE), lambda i, j: (0, j)),
                  const((1, LANES)), const((1, LANES)),
                  pl.BlockSpec((tm, LANES), row), pl.BlockSpec((tm, LANES), row), const((LANES, LANES))],
        out_specs=(pl.BlockSpec((2, tm, COL_TILE), lambda i, j: (0, i, 0)),
                   pl.BlockSpec((tm, COL_TILE), row), pl.BlockSpec((tm, COL_TILE), row),
                   pl.BlockSpec((tm, COL_TILE), row), pl.BlockSpec((tm, COL_TILE), row),
                   pl.BlockSpec((tm, COL_TILE), row)),
        scratch_shapes=[pltpu.VMEM((tm, d), BF16), pltpu.VMEM((tm, COL_TILE), F32)],
        compiler_params=_cparams(("arbitrary", "arbitrary")),
        name="inproj",
    )(x, mod, mod, g1, w_in_bf, gq, gk, cos, sin, bd)


def _layernorm_swish(y, g, b):
    mu = jnp.mean(y, axis=-1, keepdims=True)
    yc = y - mu
    var = jnp.mean(yc * yc, axis=-1, keepdims=True)
    yn = yc * lax.rsqrt(var + NORM_EPS) * g + b
    return yn * jax.nn.sigmoid(yn)


def _conv_kernel(cur_ref, prev_ref, w_ref, b_ref, lg_ref, lb_ref, y_ref, xs_scr, sh_scr, *, tm, rc):
    i = pl.program_id(0)
    xs_scr[0:CONV_HALO, :] = jnp.where(i == 0, 0.0, prev_ref[...])
    xs_scr[CONV_HALO:, :] = cur_ref[...]
    n_sh = tm + CONV_HALO - SUBLANES
    for b in range(1, SUBLANES):
        sh_scr[b - 1] = xs_scr[b:b + n_sh, :]

    first = CONV_HALO - CONV_STATE

    def chunk(r, carry):
        r0 = pl.multiple_of(r * rc, rc)
        acc = jnp.zeros((rc, CONV_CH), F32)
        for j in range(CONV_LEN):
            o = first + j
            b, a = o % SUBLANES, (o // SUBLANES) * SUBLANES
            src = xs_scr if b == 0 else sh_scr.at[b - 1]
            acc = acc + w_ref[j:j + 1, :] * src[pl.ds(r0 + a, rc), :]
        y = _layernorm_swish(acc + b_ref[...], lg_ref[...], lb_ref[...])
        y_ref[pl.ds(r0, rc), :] = y.astype(y_ref.dtype)
        return carry

    lax.fori_loop(0, tm // rc, chunk, 0)


def _conv_prompt(glu, w_dw, b_dw, ln_g, ln_b, tm=256, rc=32):
    t, c = glu.shape
    per = tm // CONV_HALO
    const = lambda shape: pl.BlockSpec(shape, lambda i: (0, 0))
    return pl.pallas_call(
        functools.partial(_conv_kernel, tm=tm, rc=rc),
        out_shape=jax.ShapeDtypeStruct((t, c), BF16),
        grid=(t // tm,),
        in_specs=[pl.BlockSpec((tm, c), lambda i: (i, 0)),
                  pl.BlockSpec((CONV_HALO, c), lambda i: (jnp.maximum(i * per - 1, 0), 0)),
                  const((CONV_LEN, c)), const((1, c)), const((1, c)), const((1, c))],
        out_specs=pl.BlockSpec((tm, c), lambda i: (i, 0)),
        scratch_shapes=[pltpu.VMEM((tm + CONV_HALO, c), F32),
                        pltpu.VMEM((SUBLANES - 1, tm + CONV_HALO - SUBLANES, c), F32)],
        compiler_params=_cparams(("arbitrary",)),
        name="conv_prompt",
    )(glu, glu, w_dw, b_dw, ln_g, ln_b)


def _conv_sample_kernel(st_ref, glu_ref, w_ref, b_ref, lg_ref, lb_ref, y_ref):
    w = w_ref[...]
    acc = jnp.sum(st_ref[...] * w[None, 0:CONV_STATE, :], axis=1) + glu_ref[...] * w[CONV_STATE:CONV_LEN, :]
    y_ref[...] = _layernorm_swish(acc + b_ref[...], lg_ref[...], lb_ref[...]).astype(y_ref.dtype)


def _conv_sample(state, glu, w_dw, b_dw, ln_g, ln_b, sb=32):
    b, s, c = state.shape
    const = lambda shape: pl.BlockSpec(shape, lambda i: (0, 0))
    return pl.pallas_call(
        _conv_sample_kernel,
        out_shape=jax.ShapeDtypeStruct((b, c), BF16),
        grid=(b // sb,),
        in_specs=[pl.BlockSpec((sb, s, c), lambda i: (i, 0, 0)), pl.BlockSpec((sb, c), lambda i: (i, 0)),
                  const((CONV_LEN, c)), const((1, c)), const((1, c)), const((1, c))],
        out_specs=pl.BlockSpec((sb, c), lambda i: (i, 0)),
        compiler_params=_cparams(("arbitrary",)),
        name="conv_sample",
    )(state, glu, w_dw, b_dw, ln_g, ln_b)


def _flash_kernel(q_ref, k_ref, v_ref, lq1_ref, lk1_ref, lq2_ref, lk2_ref, gs_ref, o_ref,
                  vx_scr, m_scr, acc_scr, sa_scr, sb_scr, *, tq, tk, lam_init):
    i = pl.program_id(1)
    dv = HEAD_DIM_V

    @pl.when(i == 0)
    def _():
        vx_scr[:, 0:dv] = v_ref[...]
        vx_scr[:, dv:2 * dv] = jnp.ones((vx_scr.shape[0], dv), BF16)

    q = q_ref[...].reshape(2 * tq, LANES)
    m_scr[...] = jnp.full(m_scr.shape, -jnp.inf, F32)
    acc_scr[...] = jnp.zeros(acc_scr.shape, F32)

    def scores(j, s_ref):
        k = k_ref[pl.ds(pl.multiple_of(j * tk, tk), tk), :]
        s_ref[...] = lax.dot_general(q, k, (((1,), (1,)), ((), ())), preferred_element_type=F32)

    def consume(j, s_ref, masked):
        k0 = pl.multiple_of(j * tk, tk)
        s = s_ref[...]
        if masked:
            row = lax.broadcasted_iota(I32, s.shape, 0)
            col = lax.broadcasted_iota(I32, s.shape, 1)
            qpos = i * tq + jnp.where(row >= tq, row - tq, row)
            s = jnp.where(k0 + col <= qpos, s, -jnp.inf)
        m_prev = m_scr[...]
        m_new = jnp.maximum(m_prev, jnp.max(s, axis=-1, keepdims=True))
        alpha = jnp.exp2(m_prev - m_new)
        p = jnp.exp2(s - m_new).astype(BF16)
        acc_scr[...] = alpha * acc_scr[...] + jnp.dot(p, vx_scr[pl.ds(k0, tk), :], preferred_element_type=F32)
        m_scr[...] = m_new

    n_full = (i * tq) // tk
    scores(0, sa_scr)

    def pair(t, carry):
        scores(2 * t + 1, sb_scr)
        consume(2 * t, sa_scr, False)
        scores(2 * t + 2, sa_scr)
        consume(2 * t + 1, sb_scr, False)
        return carry

    lax.fori_loop(0, n_full // 2, pair, 0)

    @pl.when(n_full % 2 == 1)
    def _():
        scores(n_full, sb_scr)
        consume(n_full - 1, sa_scr, False)
        consume(n_full, sb_scr, True)

    @pl.when(n_full % 2 == 0)
    def _():
        consume(n_full, sa_scr, True)

    acc = acc_scr[...]
    o = acc[:, 0:dv] / acc[:, dv:2 * dv]
    lam = _diff_lambda(lq1_ref, lk1_ref, lq2_ref, lk2_ref, lam_init)
    od = o[0:tq] - lam * o[tq:2 * tq]
    o_ref[...] = (_rms(od) * gs_ref[...] * (1.0 - lam_init)).astype(o_ref.dtype)


def _flash_prompt(q2, kb, vb, lq1, lk1, lq2, lk2, g_sub, lam_init, tq=256, tk=1024):
    _, t, _ = q2.shape
    tk = min(tk, t)
    assert tk % tq == 0 and t % tk == 0
    const = lambda shape: pl.BlockSpec(shape, lambda h, i: (0, 0))
    return pl.pallas_call(
        functools.partial(_flash_kernel, tq=tq, tk=tk, lam_init=lam_init),
        out_shape=jax.ShapeDtypeStruct((t, ATTN_WIDTH), BF16),
        grid=(N_HEADS, t // tq),
        in_specs=[pl.BlockSpec((2, tq, LANES), lambda h, i: (0, i, h)),
                  pl.BlockSpec((t, LANES), lambda h, i: (0, h)),
                  pl.BlockSpec((t, LANES), lambda h, i: (0, h)),
                  const((1, HEAD_DIM_QK)), const((1, HEAD_DIM_QK)), const((1, HEAD_DIM_QK)),
                  const((1, HEAD_DIM_QK)), const((1, LANES))],
        out_specs=pl.BlockSpec((tq, LANES), lambda h, i: (i, h)),
        scratch_shapes=[pltpu.VMEM((t, 2 * HEAD_DIM_V), BF16), pltpu.VMEM((2 * tq, 1), F32),
                        pltpu.VMEM((2 * tq, 2 * HEAD_DIM_V), F32),
                        pltpu.VMEM((2 * tq, tk), F32), pltpu.VMEM((2 * tq, tk), F32)],
        compiler_params=_cparams(("arbitrary", "arbitrary")),
        name="flash_prompt",
    )(q2, kb, vb, lq1, lk1, lq2, lk2, g_sub)


def _paged_copies(pt_ref, ck_hbm, cv_hbm, kbuf, vbuf, sem, b, slot, n_pages):
    copies = []
    for p in range(n_pages):
        pg = pt_ref[b, p]
        copies.append(pltpu.make_async_copy(ck_hbm.at[pg], kbuf.at[slot, p], sem.at[0, slot]))
        copies.append(pltpu.make_async_copy(cv_hbm.at[pg], vbuf.at[slot, p], sem.at[1, slot]))
    return copies


def _sattn_kernel(pt_ref, qt_ref, kts_ref, vs_ref, lq1_ref, lk1_ref, lq2_ref, lk2_ref, gs_ref,
                  ck_hbm, cv_hbm, o_ref, kbuf, vbuf, qb_scr, s_scr, sem, *, n_pages, lam_init):
    b = pl.program_id(0)
    nb = pl.num_programs(0)
    slot = b % 2

    @pl.when(b == 0)
    def _():
        for c in _paged_copies(pt_ref, ck_hbm, cv_hbm, kbuf, vbuf, sem, 0, 0, n_pages):
            c.start()

    @pl.when(b + 1 < nb)
    def _():
        for c in _paged_copies(pt_ref, ck_hbm, cv_hbm, kbuf, vbuf, sem, b + 1, 1 - slot, n_pages):
            c.start()

    nr = 2 * N_HEADS
    dk = HEAD_DIM_QK
    qt = qt_ref[...]
    for r in range(nr):
        qb_scr[r] = jnp.broadcast_to(qt[:, r:r + 1], (dk, LANES))
    s_self_lanes = jnp.sum(qt * kts_ref[...], axis=0, keepdims=True)
    rr = lax.broadcasted_iota(I32, (nr, LANES), 0)
    ll = lax.broadcasted_iota(I32, (nr, LANES), 1)
    s_self = jnp.sum(jnp.where(ll == (rr % N_HEADS) * 2 + rr // N_HEADS, s_self_lanes, 0.0),
                     axis=-1, keepdims=True)

    for c in _paged_copies(pt_ref, ck_hbm, cv_hbm, kbuf, vbuf, sem, b, slot, n_pages):
        c.wait()

    for p in range(n_pages):
        for r in range(nr):
            h, c = divmod(r, 2)
            kt = kbuf[slot, p, pl.ds(r * dk, dk), :]
            row = c * N_HEADS + h
            s_scr[row:row + 1, p * PAGE_SIZE:(p + 1) * PAGE_SIZE] = jnp.sum(qb_scr[r] * kt, axis=0, keepdims=True)

    s = s_scr[...]
    m = jnp.maximum(jnp.max(s, axis=-1, keepdims=True), s_self)
    e = jnp.exp2(s - m)
    e_self = jnp.exp2(s_self - m)
    inv_l = 1.0 / (jnp.sum(e, axis=-1, keepdims=True) + e_self)
    lam = _diff_lambda(lq1_ref, lk1_ref, lq2_ref, lk2_ref, lam_init)
    w = inv_l[0:N_HEADS]
    w1 = lam * inv_l[N_HEADS:nr]
    pd = (e[0:N_HEADS] * w - e[N_HEADS:nr] * w1).astype(BF16)
    pd_self = e_self[0:N_HEADS] * w - e_self[N_HEADS:nr] * w1

    head_row = lax.broadcasted_iota(I32, (N_HEADS, HEAD_DIM_V), 0)
    o = pd_self * vs_ref[...]
    for h in range(N_HEADS):
        acc = jnp.zeros((N_HEADS, HEAD_DIM_V), F32)
        for p in range(n_pages):
            vh = vbuf[slot, p, pl.ds(h, PAGE_SIZE, stride=N_HEADS), :].astype(BF16)
            acc = acc + jnp.dot(pd[:, p * PAGE_SIZE:(p + 1) * PAGE_SIZE], vh, preferred_element_type=F32)
        o = o + jnp.where(head_row == h, acc, 0.0)
    o_ref[...] = _rms(o) * gs_ref[...] * (1.0 - lam_init)


def _attn_sample(page_table, qt, kts, vs, lq1, lk1, lq2, lk2, g_sub, ck_pages, cv_pages, lam_init):
    nb, n_pages = page_table.shape
    const2 = lambda shape: pl.BlockSpec(shape, lambda b, pt: (0, 0))
    per_seq = lambda shape: pl.BlockSpec((None,) + shape, lambda b, pt: (b, 0, 0))
    page_rows = ck_pages.shape[1]
    grid_spec = pltpu.PrefetchScalarGridSpec(
        num_scalar_prefetch=1,
        grid=(nb,),
        in_specs=[per_seq((HEAD_DIM_QK, LANES)), per_seq((HEAD_DIM_QK, LANES)), per_seq((N_HEADS, HEAD_DIM_V)),
                  const2((1, HEAD_DIM_QK)), const2((1, HEAD_DIM_QK)), const2((1, HEAD_DIM_QK)),
                  const2((1, HEAD_DIM_QK)), const2((1, HEAD_DIM_V)),
                  pl.BlockSpec(memory_space=pl.ANY), pl.BlockSpec(memory_space=pl.ANY)],
        out_specs=per_seq((N_HEADS, HEAD_DIM_V)),
        scratch_shapes=[pltpu.VMEM((2, n_pages, page_rows, LANES), F32),
                        pltpu.VMEM((2, n_pages, page_rows, LANES), F32),
                        pltpu.VMEM((2 * N_HEADS, HEAD_DIM_QK, LANES), F32),
                        pltpu.VMEM((2 * N_HEADS, n_pages * PAGE_SIZE), F32),
                        pltpu.SemaphoreType.DMA((2, 2))],
    )
    return pl.pallas_call(
        functools.partial(_sattn_kernel, n_pages=n_pages, lam_init=lam_init),
        out_shape=jax.ShapeDtypeStruct((nb, N_HEADS, HEAD_DIM_V), F32),
        grid_spec=grid_spec,
        compiler_params=_cparams(("arbitrary",)),
        name="attn_sample",
    )(page_table, qt, kts, vs, lq1, lk1, lq2, lk2, g_sub, ck_pages, cv_pages)


def _outproj_kernel(on_ref, cy_ref, x_ref, gt_ref, sh_ref, sc_ref, g2_ref, wa_ref, wc_ref, x1_ref, hn_ref, *,
                    per_row):
    mix = (jnp.dot(on_ref[...].astype(BF16), wa_ref[...], preferred_element_type=F32)
           + jnp.dot(cy_ref[...], wc_ref[...], preferred_element_type=F32))
    x1 = x_ref[...] + _mod_rows(gt_ref, per_row) * mix
    x1_ref[...] = x1
    y = _rms(x1) * g2_ref[...]
    hn_ref[...] = y * (1.0 + _mod_rows(sc_ref, per_row)) + _mod_rows(sh_ref, per_row)


def _outproj(on, cy, x, mod, mod_block_rows, mod_row_block, per_row, g2, wo_a, wo_c, tm):
    t, d = x.shape
    row = lambda i: (i, 0)
    mod_spec = lambda chunk: pl.BlockSpec(
        (mod_block_rows, d), (lambda i: (i, chunk)) if per_row else (lambda i: (mod_row_block, chunk)))
    const = lambda shape: pl.BlockSpec(shape, lambda i: (0, 0))
    return pl.pallas_call(
        functools.partial(_outproj_kernel, per_row=per_row),
        out_shape=(jax.ShapeDtypeStruct((t, d), F32), jax.ShapeDtypeStruct((t, d), F32)),
        grid=(t // tm,),
        in_specs=[pl.BlockSpec((tm, ATTN_WIDTH), row), pl.BlockSpec((tm, CONV_CH), row), pl.BlockSpec((tm, d), row),
                  mod_spec(2), mod_spec(3), mod_spec(4), const((1, d)),
                  const((ATTN_WIDTH, d)), const((CONV_CH, d))],
        out_specs=(pl.BlockSpec((tm, d), row), pl.BlockSpec((tm, d), row)),
        compiler_params=_cparams(("arbitrary",)),
        name="outproj",
    )(on, cy, x, mod, mod, mod, g2, wo_a, wo_c)


def _first_index_of_max(x, idx, axis):
    m = jnp.max(x, axis=axis, keepdims=True)
    return m, jnp.min(jnp.where(x == m, idx, float(N_EXPERTS)), axis=axis, keepdims=True)


def _route_kernel(hp_ref, hs_ref, wrh_ref, wrl_ref, bias_ref, tri_ref, idx_ref, wts_ref, pos_ref, cnt_ref, run_scr,
                  *, n_prompt_tiles):
    i = pl.program_id(0)

    @pl.when(i == 0)
    def _():
        run_scr[...] = jnp.zeros(run_scr.shape, F32)

    h = jnp.where(i < n_prompt_tiles, hp_ref[...], hs_ref[...])
    h_hi = h.astype(BF16)
    h_lo = (h - h_hi.astype(F32)).astype(BF16)
    nt = (((1,), (1,)), ((), ()))
    logits = (lax.dot_general(wrh_ref[...], h_hi, nt, preferred_element_type=F32)
              + lax.dot_general(wrh_ref[...], h_lo, nt, preferred_element_type=F32)
              + lax.dot_general(wrl_ref[...], h_hi, nt, preferred_element_type=F32))
    s = jax.nn.sigmoid(logits)
    sc = s + bias_ref[...]
    tm = s.shape[1]
    neg = -jnp.inf

    sc3 = sc.reshape(N_EXPERT_GROUPS, GROUP_SIZE, tm)
    in_grp = lax.broadcasted_iota(I32, sc3.shape, 1).astype(F32)
    m1, i1 = _first_index_of_max(sc3, in_grp, 1)
    m2 = jnp.max(jnp.where(in_grp == i1, neg, sc3), axis=1, keepdims=True)
    gs = (m1 + m2).reshape(N_EXPERT_GROUPS, tm)

    gidx = lax.broadcasted_iota(I32, gs.shape, 0).astype(F32)
    gsel = jnp.zeros(gs.shape, F32)
    for _ in range(TOPK_GROUPS):
        _, gi = _first_index_of_max(gs, gidx, 0)
        hit = gidx == gi
        gsel = jnp.where(hit, 1.0, gsel)
        gs = jnp.where(hit, neg, gs)
    emask = jnp.broadcast_to(gsel[:, None, :], sc3.shape).reshape(N_EXPERTS, tm)

    cand = jnp.where(emask > 0.0, sc, neg)
    eidx = lax.broadcasted_iota(I32, cand.shape, 0).astype(F32)
    onehot = jnp.zeros(cand.shape, F32)
    hits, idxs, ws = [], [], []
    for _ in range(TOP_K):
        _, ei = _first_index_of_max(cand, eidx, 0)
        hit = eidx == ei
        hits.append(hit)
        idxs.append(ei)
        ws.append(jnp.sum(jnp.where(hit, s, 0.0), axis=0, keepdims=True))
        cand = jnp.where(hit, neg, cand)
        onehot = jnp.where(hit, 1.0, onehot)
    w = jnp.concatenate(ws, axis=0)
    wts_ref[...] = w / jnp.sum(w, axis=0, keepdims=True) * ROUTED_SCALE
    idx_ref[...] = jnp.concatenate(idxs, axis=0).astype(I32)

    rank = run_scr[...] + jnp.dot(onehot.astype(BF16), tri_ref[...], preferred_element_type=F32)
    pos = [jnp.sum(jnp.where(hit, rank, 0.0), axis=0, keepdims=True) for hit in hits]
    pos_ref[...] = jnp.concatenate(pos, axis=0).astype(I32)
    run_scr[...] = run_scr[...] + jnp.sum(onehot, axis=1, keepdims=True)
    cnt_ref[...] = run_scr[...].astype(I32)


def _route(hn_p, hn_s, wr_hi, wr_lo, bias_col, tri, tm):
    (tp, d), ts = hn_p.shape, hn_s.shape[0]
    t = tp + ts
    npt, nst = tp // tm, ts // tm
    const = lambda shape: pl.BlockSpec(shape, lambda i: (0, 0))
    tok = pl.BlockSpec((TOP_K, tm), lambda i: (0, i))
    return pl.pallas_call(
        functools.partial(_route_kernel, n_prompt_tiles=npt),
        out_shape=(jax.ShapeDtypeStruct((TOP_K, t), I32), jax.ShapeDtypeStruct((TOP_K, t), F32),
                   jax.ShapeDtypeStruct((TOP_K, t), I32), jax.ShapeDtypeStruct((N_EXPERTS, 1), I32)),
        grid=(npt + nst,),
        in_specs=[pl.BlockSpec((tm, d), lambda i: (jnp.minimum(i, npt - 1), 0)),
                  pl.BlockSpec((tm, d), lambda i: (jnp.clip(i - npt, 0, nst - 1), 0)),
                  const((N_EXPERTS, d)), const((N_EXPERTS, d)), const((N_EXPERTS, 1)), const((tm, tm))],
        out_specs=(tok, tok, tok, const((N_EXPERTS, 1))),
        scratch_shapes=[pltpu.VMEM((N_EXPERTS, 1), F32)],
        compiler_params=_cparams(("arbitrary",)),
        name="route",
    )(hn_p, hn_s, wr_hi, wr_lo, bias_col, tri)


SLOT_TOKEN_BITS = 14


def _expert_kernel(be_ref, nv_ref, slots_ref, nslots_ref, hn_hbm, wg_ref, wu_ref, wd_ref, ys_hbm,
                   xbuf, ybuf, x2d, wgb, wub, wdb, sem, *, rows):
    b = pl.program_id(0)
    nblk = pl.num_programs(0)
    slot = b % 2
    n_cur = nv_ref[b]
    n_next = nv_ref[jnp.minimum(b + 1, nblk - 1)]
    n_prev2 = nv_ref[jnp.maximum(b - 2, 0)]

    def gather(tbl_ref, n, dst_slot):
        def issue(r, carry):
            tok = tbl_ref[0, 0, r] & ((1 << SLOT_TOKEN_BITS) - 1)
            pltpu.make_async_copy(hn_hbm.at[pl.ds(tok, 1)], xbuf.at[pl.ds(dst_slot * rows + r, 1)],
                                  sem.at[0, dst_slot]).start()
            return carry
        lax.fori_loop(0, n, issue, 0)

    def wait_rows(hbm, buf, s, n, which):
        @pl.when(n > 0)
        def _():
            pltpu.make_async_copy(hbm.at[pl.ds(0, n)], buf.at[pl.ds(s * rows, n)], sem.at[which, s]).wait()

    @pl.when(b == 0)
    def _():
        xbuf[...] = jnp.zeros(xbuf.shape, F32)
        gather(slots_ref, n_cur, 0)

    @pl.when(b + 1 < nblk)
    def _():
        gather(nslots_ref, n_next, 1 - slot)

    @pl.when(b >= 2)
    def _():
        wait_rows(ys_hbm, ybuf, slot, n_prev2, 1)

    @pl.when(n_cur > 0)
    def _():
        changed = jnp.logical_or(b == 0, be_ref[b] != be_ref[jnp.maximum(b - 1, 0)])

        @pl.when(changed)
        def _():
            wgb[...] = wg_ref[...].astype(BF16)
            wub[...] = wu_ref[...].astype(BF16)
            wdb[...] = wd_ref[...].astype(BF16)

        wait_rows(hn_hbm, xbuf, slot, n_cur, 0)
        base = pl.multiple_of(slot * rows, rows)
        x2d[...] = xbuf[pl.ds(base, rows)].reshape(rows, x2d.shape[1])
        x = x2d[...].astype(BF16)
        g = jnp.dot(x, wgb[...], preferred_element_type=F32)
        u = jnp.dot(x, wub[...], preferred_element_type=F32)
        hmid = (_silu(g) * u).astype(BF16)
        y = jnp.dot(hmid, wdb[...], preferred_element_type=F32)
        ybuf[pl.ds(base, rows)] = y.reshape(rows, 1, y.shape[1])

        def issue(r, carry):
            dst = lax.shift_right_logical(slots_ref[0, 0, r], SLOT_TOKEN_BITS)
            pltpu.make_async_copy(ybuf.at[pl.ds(base + r, 1)], ys_hbm.at[pl.ds(dst, 1)], sem.at[1, slot]).start()
            return carry
        lax.fori_loop(0, n_cur, issue, 0)

    @pl.when(b == nblk - 1)
    def _():
        wait_rows(ys_hbm, ybuf, slot, n_cur, 1)

        @pl.when(b >= 1)
        def _():
            wait_rows(ys_hbm, ybuf, 1 - slot, nv_ref[jnp.maximum(b - 1, 0)], 1)


def _experts(block_e, n_valid, slots, hn3, w_gate, w_up, w_down, layer, n_rows_out, rows):
    nblk = block_e.shape[0]
    d = hn3.shape[2]
    wspec = lambda shape: pl.BlockSpec((None, None) + shape, lambda b, be, nv: (layer, be[b], 0, 0))
    grid_spec = pltpu.PrefetchScalarGridSpec(
        num_scalar_prefetch=2,
        grid=(nblk,),
        in_specs=[pl.BlockSpec((1, 1, rows), lambda b, be, nv: (b, 0, 0), memory_space=pltpu.SMEM),
                  pl.BlockSpec((1, 1, rows), lambda b, be, nv: (jnp.minimum(b + 1, nblk - 1), 0, 0),
                               memory_space=pltpu.SMEM),
                  pl.BlockSpec(memory_space=pl.ANY),
                  wspec((d, EXPERT_FF)), wspec((d, EXPERT_FF)), wspec((EXPERT_FF, d))],
        out_specs=pl.BlockSpec(memory_space=pl.ANY),
        scratch_shapes=[pltpu.VMEM((2 * rows, 1, d), F32), pltpu.VMEM((2 * rows, 1, d), F32),
                        pltpu.VMEM((rows, d), F32),
                        pltpu.VMEM((d, EXPERT_FF), BF16), pltpu.VMEM((d, EXPERT_FF), BF16),
                        pltpu.VMEM((EXPERT_FF, d), BF16), pltpu.SemaphoreType.DMA((2, 2))],
    )
    return pl.pallas_call(
        functools.partial(_expert_kernel, rows=rows),
        out_shape=jax.ShapeDtypeStruct((n_rows_out, 1, d), F32),
        grid_spec=grid_spec,
        compiler_params=_cparams(("arbitrary",)),
        name="experts",
    )(block_e, n_valid, slots, slots, hn3, w_gate, w_up, w_down)


def _dispatch_tables(idx_t, pos_t, counts, rows, n_blocks, k_stride):
    t = idx_t.shape[1]
    nblk_e = (counts + rows - 1) // rows
    blk_end = jnp.cumsum(nblk_e)
    blk_start = blk_end - nblk_e
    is_e = idx_t[:, :, None] == jnp.arange(N_EXPERTS, dtype=I32)[None, None, :]
    dest = jnp.sum(jnp.where(is_e, (blk_start * rows)[None, None, :], 0), axis=-1) + pos_t
    tok = jnp.broadcast_to(jnp.arange(t, dtype=I32)[None, :], (TOP_K, t))
    out_row = jnp.arange(TOP_K, dtype=I32)[:, None] * k_stride + tok
    packed = out_row * (1 << SLOT_TOKEN_BITS) + tok
    slots = jnp.zeros((n_blocks * rows,), I32).at[dest.reshape(-1)].set(packed.reshape(-1))
    blocks = jnp.arange(n_blocks, dtype=I32)
    block_e = jnp.minimum(jnp.searchsorted(blk_end, blocks, side="right"), N_EXPERTS - 1).astype(I32)
    n_valid = jnp.clip(counts[block_e] - (blocks - blk_start[block_e]) * rows, 0, rows).astype(I32)
    return block_e, n_valid, slots.reshape(n_blocks, 1, rows)


def _combine_kernel(*refs, per_row):
    ys_refs = refs[:TOP_K]
    wt_ref, hn_ref, x1_ref, gt_ref, wg_ref, wu_ref, wd_ref, o_ref, y2d = refs[TOP_K:]
    h = hn_ref[...].astype(BF16)
    g = jnp.dot(h, wg_ref[...], preferred_element_type=F32)
    u = jnp.dot(h, wu_ref[...], preferred_element_type=F32)
    moe = jnp.dot((_silu(g) * u).astype(BF16), wd_ref[...], preferred_element_type=F32)
    wt = wt_ref[...]
    for k in range(TOP_K):
        y2d[...] = ys_refs[k][...].reshape(y2d.shape)
        moe = moe + wt[:, k:k + 1] * y2d[...]
    o_ref[...] = x1_ref[...] + _mod_rows(gt_ref, per_row) * moe


def _combine(ys, wts, hn, x1, mod, mod_block_rows, mod_row_block, per_row, ws_g, ws_u, ws_d, tm, token0):
    t, d = hn.shape
    k_stride = ys.shape[0] // TOP_K
    row = lambda i: (i, 0)
    const = lambda shape: pl.BlockSpec(shape, lambda i: (0, 0))
    gt_spec = pl.BlockSpec((mod_block_rows, d), (lambda i: (i, 5)) if per_row else (lambda i: (mod_row_block, 5)))
    ys_spec = lambda k: pl.BlockSpec((pl.Element(tm), pl.Element(1), pl.Element(d)),
                                     lambda i: (k * k_stride + token0 + i * tm, 0, 0))
    return pl.pallas_call(
        functools.partial(_combine_kernel, per_row=per_row),
        out_shape=jax.ShapeDtypeStruct((t, d), F32),
        grid=(t // tm,),
        in_specs=[ys_spec(k) for k in range(TOP_K)] + [
            pl.BlockSpec((tm, TOP_K), lambda i: (i + token0 // tm, 0)), pl.BlockSpec((tm, d), row),
            pl.BlockSpec((tm, d), row),
            gt_spec, const((d, EXPERT_FF)), const((d, EXPERT_FF)), const((EXPERT_FF, d))],
        out_specs=pl.BlockSpec((tm, d), row),
        scratch_shapes=[pltpu.VMEM((tm, d), F32)],
        compiler_params=_cparams(("arbitrary",)),
        name="combine",
    )(*([ys] * TOP_K), wts, hn, x1, mod, ws_g, ws_u, ws_d)


def _rope_tables(pos):
    half = HEAD_DIM_QK // 2
    inv = ROPE_THETA ** (-jnp.arange(half, dtype=F32) * 2.0 / HEAD_DIM_QK)
    ang = pos.astype(F32)[:, None] * inv[None, :]
    cos, sin = jnp.cos(ang), jnp.sin(ang)
    return jnp.tile(jnp.concatenate([cos, cos], axis=1), (1, 2)), jnp.tile(jnp.concatenate([-sin, sin], axis=1), (1, 2))


def _n_expert_blocks(n_assign, rows):
    return -(-(n_assign + N_EXPERTS * (rows - 1)) // rows)


def kernel(x_prompt, x_sample, c_prompt, c_sample, cache_k, cache_v, state_conv, page_table, w_ada, b_ada, g_norm1, g_norm2, w_in, g_q, g_k, lambda_q1, lambda_k1, lambda_q2, lambda_k2, g_sub, w_dw, b_dw, ln_g, ln_b, w_out, w_router, router_bias, w_e_gate, w_e_up, w_e_down, w_s_gate, w_s_up, w_s_down):
    layer = 0
    lam_init = 0.8 - 0.6 * math.exp(-0.3 * layer)
    d = D_MODEL
    tp = x_prompt.shape[1]
    ns = x_sample.shape[0]
    t_all = tp + ns
    xp = x_prompt.reshape(tp, d)
    xs = x_sample.reshape(ns, d)

    row2 = lambda a: a[layer].reshape(1, -1)
    w_in_bf = w_in[layer].astype(BF16)
    wo_a = w_out[layer, :ATTN_WIDTH].astype(BF16)
    wo_c = w_out[layer, ATTN_WIDTH:].astype(BF16)
    ws_g, ws_u, ws_d = (w[layer].astype(BF16) for w in (w_s_gate, w_s_up, w_s_down))
    wr_t = w_router[layer].T
    wr_hi = wr_t.astype(BF16)
    wr_lo = (wr_t - wr_hi.astype(F32)).astype(BF16)
    gq = jnp.tile(row2(g_q), (1, 2))
    gk = jnp.tile(row2(g_k), (1, 2))
    lane = jnp.arange(LANES)
    bd = (lane[:, None] // HEAD_DIM_QK == lane[None, :] // HEAD_DIM_QK).astype(BF16)
    lams = [row2(a) for a in (lambda_q1, lambda_k1, lambda_q2, lambda_k2)]

    c_all = jnp.concatenate([c_sample, c_prompt, jnp.zeros((SUBLANES - 1, d), F32)], axis=0)
    mod = _ada(c_all, w_ada[layer], row2(b_ada))
    p_mod = dict(mod_block_rows=SUBLANES, mod_row_block=ns // SUBLANES, per_row=False)
    s_mod = dict(mod_block_rows=ns, mod_row_block=0, per_row=True)

    cos_p, sin_p = _rope_tables(jnp.arange(tp, dtype=I32))
    cos_s, sin_s = _rope_tables(jnp.full((ns,), PAST_LEN, I32))

    q_p, k_p, kb_p, v_p, vb_p, glu_p = _inproj(xp, mod, g1=row2(g_norm1), w_in_bf=w_in_bf, gq=gq, gk=gk,
                                               cos=cos_p, sin=sin_p, bd=bd, tm=512, **p_mod)
    cy_p = _conv_prompt(glu_p, w_dw[layer], row2(b_dw), row2(ln_g), row2(ln_b))
    on_p = _flash_prompt(q_p, kb_p, vb_p, *lams, row2(g_sub), lam_init)

    q_s, k_s, _, v_s, _, glu_s = _inproj(xs, mod, g1=row2(g_norm1), w_in_bf=w_in_bf, gq=gq, gk=gk,
                                         cos=cos_s, sin=sin_s, bd=bd, tm=ns, **s_mod)
    cy_s = _conv_sample(state_conv[layer], glu_s, w_dw[layer], row2(b_dw), row2(ln_g), row2(ln_b))
    n_phys = cache_k.shape[1]
    ck_pages = jnp.transpose(cache_k, (0, 1, 3, 4, 5, 2)).reshape(-1, ATTN_WIDTH, PAGE_SIZE)
    cv_pages = cache_v.reshape(-1, PAGE_SIZE * N_HEADS, HEAD_DIM_V)
    to_cols = lambda a: jnp.pad(jnp.swapaxes(a.reshape(ns, 2 * N_HEADS, HEAD_DIM_QK), 1, 2),
                                ((0, 0), (0, 0), (0, LANES - 2 * N_HEADS)))
    qt_s = to_cols(q_s[0].astype(F32) + q_s[1].astype(F32))
    on_s = _attn_sample(page_table + layer * n_phys, qt_s, to_cols(k_s), v_s.reshape(ns, N_HEADS, HEAD_DIM_V),
                        *lams, row2(g_sub), ck_pages, cv_pages, lam_init).reshape(ns, ATTN_WIDTH)

    x1_p, hn_p = _outproj(on_p, cy_p, xp, mod, g2=row2(g_norm2), wo_a=wo_a, wo_c=wo_c, tm=512, **p_mod)
    x1_s, hn_s = _outproj(on_s, cy_s, xs, mod, g2=row2(g_norm2), wo_a=wo_a, wo_c=wo_c, tm=ns, **s_mod)

    tm_r = ns
    tri = (jnp.arange(tm_r)[:, None] < jnp.arange(tm_r)[None, :]).astype(BF16)
    idx_t, wts_t, pos_t, counts = _route(hn_p, hn_s, wr_hi, wr_lo, router_bias[layer].reshape(-1, 1), tri, tm_r)
    n_blocks = _n_expert_blocks(TOP_K * t_all, EXPERT_ROWS)
    block_e, n_valid, slots = _dispatch_tables(idx_t, pos_t, counts.reshape(-1), EXPERT_ROWS, n_blocks, t_all)
    hn3 = jnp.concatenate([hn_p, hn_s], axis=0).reshape(t_all, 1, d)
    ys = _experts(block_e, n_valid, slots, hn3, w_e_gate, w_e_up, w_e_down, layer, TOP_K * t_all, EXPERT_ROWS)
    wts = wts_t.T

    y_p = _combine(ys, wts, hn_p, x1_p, mod, ws_g=ws_g, ws_u=ws_u, ws_d=ws_d, tm=256, token0=0, **p_mod)
    y_s = _combine(ys, wts, hn_s, x1_s, mod, ws_g=ws_g, ws_u=ws_u, ws_d=ws_d, tm=ns, token0=tp, **s_mod)

    conv_p = glu_p[tp - CONV_STATE:]
    conv_s = jnp.concatenate([state_conv[layer][:, 1:], glu_s[:, None, :]], axis=1)
    return (y_p.reshape(1, tp, d), y_s.reshape(ns, 1, d),
            k_p.reshape(1, 1, tp, N_HEADS, 2, HEAD_DIM_QK), v_p.reshape(1, 1, tp, N_HEADS, HEAD_DIM_V),
            conv_p.reshape(1, 1, CONV_STATE, CONV_CH),
            k_s.reshape(1, ns, 1, N_HEADS, 2, HEAD_DIM_QK), v_s.reshape(1, ns, 1, N_HEADS, HEAD_DIM_V),
            conv_s.reshape(1, ns, CONV_STATE, CONV_CH))
```

```python
import functools
import math

import jax
import jax.numpy as jnp
from jax import lax
from jax.experimental import pallas as pl
from jax.experimental.pallas import tpu as pltpu

F32 = jnp.float32
BF16 = jnp.bfloat16
I32 = jnp.int32

D_MODEL = 2048
ATTN_WIDTH = 1024
CONV_CH = 1024
N_HEADS = 8
HEAD_DIM_V = 128
HEAD_DIM_QK = 64
CONV_LEN = 31
CONV_STATE = CONV_LEN - 1
N_EXPERTS = 256
TOP_K = 8
N_EXPERT_GROUPS = 8
GROUP_SIZE = N_EXPERTS // N_EXPERT_GROUPS
TOPK_GROUPS = 4
EXPERT_FF = 512
ROUTED_SCALE = 2.5
ROPE_THETA = 10000.0
NORM_EPS = 1e-6
PAST_LEN = 2048
PAGE_SIZE = 128
IN_COLS = 3 * ATTN_WIDTH + 2 * CONV_CH
LANES = 128
SUBLANES = 8
COL_TILE = 1024
CONV_HALO = 32
EXPERT_ROWS = 320
VMEM_LIMIT = 56 * 1024 * 1024
Q_SCALE = HEAD_DIM_QK ** -0.5 * math.log2(math.e)


def _cparams(sem, vmem=VMEM_LIMIT):
    return pltpu.CompilerParams(dimension_semantics=sem, vmem_limit_bytes=vmem)


def _rms(x, eps=NORM_EPS):
    return x * lax.rsqrt(jnp.mean(x * x, axis=-1, keepdims=True) + eps)


def _silu(x):
    return x * jax.nn.sigmoid(x)


def _mod_rows(ref, per_row):
    return ref[...] if per_row else ref[0:1, :]


def _diff_lambda(lq1_ref, lk1_ref, lq2_ref, lk2_ref, lam_init):
    a = jnp.sum(lq1_ref[...] * lk1_ref[...], axis=-1, keepdims=True)
    b = jnp.sum(lq2_ref[...] * lk2_ref[...], axis=-1, keepdims=True)
    return jnp.exp(a) - jnp.exp(b) + lam_init


def _ada_kernel(c_ref, w_ref, b_ref, o_ref):
    s = _silu(c_ref[...]).astype(BF16)
    o_ref[...] = jnp.dot(s, w_ref[...].astype(BF16), preferred_element_type=F32) + b_ref[...]


def _ada(c_all, w_ada, b_ada):
    rows, d = c_all.shape
    n = w_ada.shape[1]
    tn = 1024
    return pl.pallas_call(
        _ada_kernel,
        out_shape=jax.ShapeDtypeStruct((rows, n), F32),
        grid=(n // tn,),
        in_specs=[pl.BlockSpec((rows, d), lambda j: (0, 0)),
                  pl.BlockSpec((d, tn), lambda j: (0, j)),
                  pl.BlockSpec((1, tn), lambda j: (0, j))],
        out_specs=pl.BlockSpec((rows, tn), lambda j: (0, j)),
        compiler_params=_cparams(("arbitrary",)),
        name="adaln",
    )(c_all, w_ada, b_ada)


def _qk_norm_rope(u, g, cos, sin, bd):
    lane = lax.broadcasted_iota(I32, (u.shape[0], LANES), 1)
    first_half = (lane & 32) == 0
    outs = []
    for c in range(u.shape[1] // LANES):
        x = u[:, c * LANES:(c + 1) * LANES]
        ss = jnp.dot((x * x).astype(BF16), bd, preferred_element_type=F32)
        xn = x * lax.rsqrt(ss * (1.0 / HEAD_DIM_QK) + NORM_EPS) * g
        rot = jnp.where(first_half, pltpu.roll(xn, 96, 1), pltpu.roll(xn, 32, 1))
        outs.append(xn * cos + rot * sin)
    return jnp.concatenate(outs, axis=1)


def _inproj_kernel(x_ref, sh_ref, sc_ref, g1_ref, w_ref, gq_ref, gk_ref, cos_ref, sin_ref, bd_ref,
                   q_ref, k_ref, kb_ref, v_ref, vb_ref, glu_ref, xn_scr, val_scr, *, per_row):
    j = pl.program_id(1)

    @pl.when(j == 0)
    def _():
        y = _rms(x_ref[...]) * g1_ref[...]
        xn_scr[...] = (y * (1.0 + _mod_rows(sc_ref, per_row)) + _mod_rows(sh_ref, per_row)).astype(BF16)

    u = jnp.dot(xn_scr[...], w_ref[...], preferred_element_type=F32)

    @pl.when(j == 0)
    def _():
        q = _qk_norm_rope(u, gq_ref[...], cos_ref[...], sin_ref[...], bd_ref[...]) * Q_SCALE
        lane = lax.broadcasted_iota(I32, q.shape, 1)
        map0 = (lane & HEAD_DIM_QK) == 0
        q_ref[0] = jnp.where(map0, q, 0.0).astype(BF16)
        q_ref[1] = jnp.where(map0, 0.0, q).astype(BF16)

    @pl.when(j == 1)
    def _():
        k = _qk_norm_rope(u, gk_ref[...], cos_ref[...], sin_ref[...], bd_ref[...])
        k_ref[...] = k
        kb_ref[...] = k.astype(BF16)

    @pl.when(j == 2)
    def _():
        v_ref[...] = u
        vb_ref[...] = u.astype(BF16)

    @pl.when(j == 3)
    def _():
        val_scr[...] = u

    @pl.when(j == 4)
    def _():
        glu_ref[...] = val_scr[...] * jax.nn.sigmoid(u)


def _inproj(x, mod, mod_block_rows, mod_row_block, per_row, g1, w_in_bf, gq, gk, cos, sin, bd, tm):
    t, d = x.shape
    nj = IN_COLS // COL_TILE
    row = lambda i, j: (i, 0)
    mod_spec = lambda chunk: pl.BlockSpec(
        (mod_block_rows, d), (lambda i, j: (i, chunk)) if per_row else (lambda i, j: (mod_row_block, chunk)))
    const = lambda shape: pl.BlockSpec(shape, lambda i, j: (0, 0))
    return pl.pallas_call(
        functools.partial(_inproj_kernel, per_row=per_row),
        out_shape=(jax.ShapeDtypeStruct((2, t, COL_TILE), BF16),
                   jax.ShapeDtypeStruct((t, COL_TILE), F32), jax.ShapeDtypeStruct((t, COL_TILE), BF16),
                   jax.ShapeDtypeStruct((t, COL_TILE), F32), jax.ShapeDtypeStruct((t, COL_TILE), BF16),
                   jax.ShapeDtypeStruct((t, COL_TILE), F32)),
        grid=(t // tm, nj),
        in_specs=[pl.BlockSpec((tm, d), row), mod_spec(0), mod_spec(1), const((1, d)),
                  pl.BlockSpec((d, COL_TILE), lambda i, j: (0, j)),
                  const((1, LANES)), const((1, LANES)),
                  pl.BlockSpec((tm, LANES), row), pl.BlockSpec((tm, LANES), row), const((LANES, LANES))],
        out_specs=(pl.BlockSpec((2, tm, COL_TILE), lambda i, j: (0, i, 0)),
                   pl.BlockSpec((tm, COL_TILE), row), pl.BlockSpec((tm, COL_TILE), row),
                   pl.BlockSpec((tm, COL_TILE), row), pl.BlockSpec((tm, COL_TILE), row),
                   pl.BlockSpec((tm, COL_TILE), row)),
        scratch_shapes=[pltpu.VMEM((tm, d), BF16), pltpu.VMEM((tm, COL_TILE), F32)],
        compiler_params=_cparams(("arbitrary", "arbitrary")),
        name="inproj",
    )(x, mod, mod, g1, w_in_bf, gq, gk, cos, sin, bd)


def _layernorm_swish(y, g, b):
    mu = jnp.mean(y, axis=-1, keepdims=True)
    yc = y - mu
    var = jnp.mean(yc * yc, axis=-1, keepdims=True)
    yn = yc * lax.rsqrt(var + NORM_EPS) * g + b
    return yn * jax.nn.sigmoid(yn)


def _conv_kernel(cur_ref, prev_ref, w_ref, b_ref, lg_ref, lb_ref, y_ref, xs_scr, sh_scr, *, tm, rc):
    i = pl.program_id(0)
    xs_scr[0:CONV_HALO, :] = jnp.where(i == 0, 0.0, prev_ref[...])
    xs_scr[CONV_HALO:, :] = cur_ref[...]
    n_sh = tm + CONV_HALO - SUBLANES
    for b in range(1, SUBLANES):
        sh_scr[b - 1] = xs_scr[b:b + n_sh, :]

    first = CONV_HALO - CONV_STATE

    def chunk(r, carry):
        r0 = pl.multiple_of(r * rc, rc)
        acc = jnp.zeros((rc, CONV_CH), F32)
        for j in range(CONV_LEN):
            o = first + j
            b, a = o % SUBLANES, (o // SUBLANES) * SUBLANES
            src = xs_scr if b == 0 else sh_scr.at[b - 1]
            acc = acc + w_ref[j:j + 1, :] * src[pl.ds(r0 + a, rc), :]
        y = _layernorm_swish(acc + b_ref[...], lg_ref[...], lb_ref[...])
        y_ref[pl.ds(r0, rc), :] = y.astype(y_ref.dtype)
        return carry

    lax.fori_loop(0, tm // rc, chunk, 0)


def _conv_prompt(glu, w_dw, b_dw, ln_g, ln_b, tm=256, rc=32):
    t, c = glu.shape
    per = tm // CONV_HALO
    const = lambda shape: pl.BlockSpec(shape, lambda i: (0, 0))
    return pl.pallas_call(
        functools.partial(_conv_kernel, tm=tm, rc=rc),
        out_shape=jax.ShapeDtypeStruct((t, c), BF16),
        grid=(t // tm,),
        in_specs=[pl.BlockSpec((tm, c), lambda i: (i, 0)),
                  pl.BlockSpec((CONV_HALO, c), lambda i: (jnp.maximum(i * per - 1, 0), 0)),
                  const((CONV_LEN, c)), const((1, c)), const((1, c)), const((1, c))],
        out_specs=pl.BlockSpec((tm, c), lambda i: (i, 0)),
        scratch_shapes=[pltpu.VMEM((tm + CONV_HALO, c), F32),
                        pltpu.VMEM((SUBLANES - 1, tm + CONV_HALO - SUBLANES, c), F32)],
        compiler_params=_cparams(("arbitrary",)),
        name="conv_prompt",
    )(glu, glu, w_dw, b_dw, ln_g, ln_b)


def _conv_sample_kernel(st_ref, glu_ref, w_ref, b_ref, lg_ref, lb_ref, y_ref):
    w = w_ref[...]
    acc = jnp.sum(st_ref[...] * w[None, 0:CONV_STATE, :], axis=1) + glu_ref[...] * w[CONV_STATE:CONV_LEN, :]
    y_ref[...] = _layernorm_swish(acc + b_ref[...], lg_ref[...], lb_ref[...]).astype(y_ref.dtype)


def _conv_sample(state, glu, w_dw, b_dw, ln_g, ln_b, sb=32):
    b, s, c = state.shape
    const = lambda shape: pl.BlockSpec(shape, lambda i: (0, 0))
    return pl.pallas_call(
        _conv_sample_kernel,
        out_shape=jax.ShapeDtypeStruct((b, c), BF16),
        grid=(b // sb,),
        in_specs=[pl.BlockSpec((sb, s, c), lambda i: (i, 0, 0)), pl.BlockSpec((sb, c), lambda i: (i, 0)),
                  const((CONV_LEN, c)), const((1, c)), const((1, c)), const((1, c))],
        out_specs=pl.BlockSpec((sb, c), lambda i: (i, 0)),
        compiler_params=_cparams(("arbitrary",)),
        name="conv_sample",
    )(state, glu, w_dw, b_dw, ln_g, ln_b)


def _flash_kernel(q_ref, k_ref, v_ref, lq1_ref, lk1_ref, lq2_ref, lk2_ref, gs_ref, o_ref,
                  vx_scr, m_scr, acc_scr, sa_scr, sb_scr, *, tq, tk, lam_init):
    i = pl.program_id(1)
    dv = HEAD_DIM_V

    @pl.when(i == 0)
    def _():
        vx_scr[:, 0:dv] = v_ref[...]
        vx_scr[:, dv:2 * dv] = jnp.ones((vx_scr.shape[0], dv), BF16)

    q = q_ref[...].reshape(2 * tq, LANES)
    m_scr[...] = jnp.full(m_scr.shape, -jnp.inf, F32)
    acc_scr[...] = jnp.zeros(acc_scr.shape, F32)

    def scores(j, s_ref):
        k = k_ref[pl.ds(pl.multiple_of(j * tk, tk), tk), :]
        s_ref[...] = lax.dot_general(q, k, (((1,), (1,)), ((), ())), preferred_element_type=F32)

    def consume(j, s_ref, masked):
        k0 = pl.multiple_of(j * tk, tk)
        s = s_ref[...]
        if masked:
            row = lax.broadcasted_iota(I32, s.shape, 0)
            col = lax.broadcasted_iota(I32, s.shape, 1)
            qpos = i * tq + jnp.where(row >= tq, row - tq, row)
            s = jnp.where(k0 + col <= qpos, s, -jnp.inf)
        m_prev = m_scr[...]
        m_new = jnp.maximum(m_prev, jnp.max(s, axis=-1, keepdims=True))
        alpha = jnp.exp2(m_prev - m_new)
        p = jnp.exp2(s - m_new).astype(BF16)
        acc_scr[...] = alpha * acc_scr[...] + jnp.dot(p, vx_scr[pl.ds(k0, tk), :], preferred_element_type=F32)
        m_scr[...] = m_new

    n_full = (i * tq) // tk
    scores(0, sa_scr)

    def pair(t, carry):
        scores(2 * t + 1, sb_scr)
        consume(2 * t, sa_scr, False)
        scores(2 * t + 2, sa_scr)
        consume(2 * t + 1, sb_scr, False)
        return carry

    lax.fori_loop(0, n_full // 2, pair, 0)

    @pl.when(n_full % 2 == 1)
    def _():
        scores(n_full, sb_scr)
        consume(n_full - 1, sa_scr, False)
        consume(n_full, sb_scr, True)

    @pl.when(n_full % 2 == 0)
    def _():
        consume(n_full, sa_scr, True)

    acc = acc_scr[...]
    o = acc[:, 0:dv] / acc[:, dv:2 * dv]
    lam = _diff_lambda(lq1_ref, lk1_ref, lq2_ref, lk2_ref, lam_init)
    od = o[0:tq] - lam * o[tq:2 * tq]
    o_ref[...] = (_rms(od) * gs_ref[...] * (1.0 - lam_init)).astype(o_ref.dtype)


def _flash_prompt(q2, kb, vb, lq1, lk1, lq2, lk2, g_sub, lam_init, tq=256, tk=1024):
    _, t, _ = q2.shape
    tk = min(tk, t)
    assert tk % tq == 0 and t % tk == 0
    const = lambda shape: pl.BlockSpec(shape, lambda h, i: (0, 0))
    return pl.pallas_call(
        functools.partial(_flash_kernel, tq=tq, tk=tk, lam_init=lam_init),
        out_shape=jax.ShapeDtypeStruct((t, ATTN_WIDTH), BF16),
        grid=(N_HEADS, t // tq),
        in_specs=[pl.BlockSpec((2, tq, LANES), lambda h, i: (0, i, h)),
                  pl.BlockSpec((t, LANES), lambda h, i: (0, h)),
                  pl.BlockSpec((t, LANES), lambda h, i: (0, h)),
                  const((1, HEAD_DIM_QK)), const((1, HEAD_DIM_QK)), const((1, HEAD_DIM_QK)),
                  const((1, HEAD_DIM_QK)), const((1, LANES))],
        out_specs=pl.BlockSpec((tq, LANES), lambda h, i: (i, h)),
        scratch_shapes=[pltpu.VMEM((t, 2 * HEAD_DIM_V), BF16), pltpu.VMEM((2 * tq, 1), F32),
                        pltpu.VMEM((2 * tq, 2 * HEAD_DIM_V), F32),
                        pltpu.VMEM((2 * tq, tk), F32), pltpu.VMEM((2 * tq, tk), F32)],
        compiler_params=_cparams(("arbitrary", "arbitrary")),
        name="flash_prompt",
    )(q2, kb, vb, lq1, lk1, lq2, lk2, g_sub)


def _paged_copies(pt_ref, ck_hbm, cv_hbm, kbuf, vbuf, sem, b, slot, n_pages):
    copies = []
    for p in range(n_pages):
        pg = pt_ref[b, p]
        copies.append(pltpu.make_async_copy(ck_hbm.at[pg], kbuf.at[slot, p], sem.at[0, slot]))
        copies.append(pltpu.make_async_copy(cv_hbm.at[pg], vbuf.at[slot, p], sem.at[1, slot]))
    return copies


def _sattn_kernel(pt_ref, qt_ref, kts_ref, vs_ref, lq1_ref, lk1_ref, lq2_ref, lk2_ref, gs_ref,
                  ck_hbm, cv_hbm, o_ref, kbuf, vbuf, qb_scr, s_scr, sem, *, n_pages, lam_init):
    b = pl.program_id(0)
    nb = pl.num_programs(0)
    slot = b % 2

    @pl.when(b == 0)
    def _():
        for c in _paged_copies(pt_ref, ck_hbm, cv_hbm, kbuf, vbuf, sem, 0, 0, n_pages):
            c.start()

    @pl.when(b + 1 < nb)
    def _():
        for c in _paged_copies(pt_ref, ck_hbm, cv_hbm, kbuf, vbuf, sem, b + 1, 1 - slot, n_pages):
            c.start()

    nr = 2 * N_HEADS
    dk = HEAD_DIM_QK
    qt = qt_ref[...]
    for r in range(nr):
        qb_scr[r] = jnp.broadcast_to(qt[:, r:r + 1], (dk, LANES))
    s_self_lanes = jnp.sum(qt * kts_ref[...], axis=0, keepdims=True)
    rr = lax.broadcasted_iota(I32, (nr, LANES), 0)
    ll = lax.broadcasted_iota(I32, (nr, LANES), 1)
    s_self = jnp.sum(jnp.where(ll == (rr % N_HEADS) * 2 + rr // N_HEADS, s_self_lanes, 0.0),
                     axis=-1, keepdims=True)

    for c in _paged_copies(pt_ref, ck_hbm, cv_hbm, kbuf, vbuf, sem, b, slot, n_pages):
        c.wait()

    for p in range(n_pages):
        for r in range(nr):
            h, c = divmod(r, 2)
            kt = kbuf[slot, p, pl.ds(r * dk, dk), :]
            row = c * N_HEADS + h
            s_scr[row:row + 1, p * PAGE_SIZE:(p + 1) * PAGE_SIZE] = jnp.sum(qb_scr[r] * kt, axis=0, keepdims=True)

    s = s_scr[...]
    m = jnp.maximum(jnp.max(s, axis=-1, keepdims=True), s_self)
    e = jnp.exp2(s - m)
    e_self = jnp.exp2(s_self - m)
    inv_l = 1.0 / (jnp.sum(e, axis=-1, keepdims=True) + e_self)
    lam = _diff_lambda(lq1_ref, lk1_ref, lq2_ref, lk2_ref, lam_init)
    w = inv_l[0:N_HEADS]
    w1 = lam * inv_l[N_HEADS:nr]
    pd = (e[0:N_HEADS] * w - e[N_HEADS:nr] * w1).astype(BF16)
    pd_self = e_self[0:N_HEADS] * w - e_self[N_HEADS:nr] * w1

    head_row = lax.broadcasted_iota(I32, (N_HEADS, HEAD_DIM_V), 0)
    o = pd_self * vs_ref[...]
    for h in range(N_HEADS):
        acc = jnp.zeros((N_HEADS, HEAD_DIM_V), F32)
        for p in range(n_pages):
            vh = vbuf[slot, p, pl.ds(h, PAGE_SIZE, stride=N_HEADS), :].astype(BF16)
            acc = acc + jnp.dot(pd[:, p * PAGE_SIZE:(p + 1) * PAGE_SIZE], vh, preferred_element_type=F32)
        o = o + jnp.where(head_row == h, acc, 0.0)
    o_ref[...] = _rms(o) * gs_ref[...] * (1.0 - lam_init)


def _attn_sample(page_table, qt, kts, vs, lq1, lk1, lq2, lk2, g_sub, ck_pages, cv_pages, lam_init):
    nb, n_pages = page_table.shape
    const2 = lambda shape: pl.BlockSpec(shape, lambda b, pt: (0, 0))
    per_seq = lambda shape: pl.BlockSpec((None,) + shape, lambda b, pt: (b, 0, 0))
    page_rows = ck_pages.shape[1]
    grid_spec = pltpu.PrefetchScalarGridSpec(
        num_scalar_prefetch=1,
        grid=(nb,),
        in_specs=[per_seq((HEAD_DIM_QK, LANES)), per_seq((HEAD_DIM_QK, LANES)), per_seq((N_HEADS, HEAD_DIM_V)),
                  const2((1, HEAD_DIM_QK)), const2((1, HEAD_DIM_QK)), const2((1, HEAD_DIM_QK)),
                  const2((1, HEAD_DIM_QK)), const2((1, HEAD_DIM_V)),
                  pl.BlockSpec(memory_space=pl.ANY), pl.BlockSpec(memory_space=pl.ANY)],
        out_specs=per_seq((N_HEADS, HEAD_DIM_V)),
        scratch_shapes=[pltpu.VMEM((2, n_pages, page_rows, LANES), F32),
                        pltpu.VMEM((2, n_pages, page_rows, LANES), F32),
                        pltpu.VMEM((2 * N_HEADS, HEAD_DIM_QK, LANES), F32),
                        pltpu.VMEM((2 * N_HEADS, n_pages * PAGE_SIZE), F32),
                        pltpu.SemaphoreType.DMA((2, 2))],
    )
    return pl.pallas_call(
        functools.partial(_sattn_kernel, n_pages=n_pages, lam_init=lam_init),
        out_shape=jax.ShapeDtypeStruct((nb, N_HEADS, HEAD_DIM_V), F32),
        grid_spec=grid_spec,
        compiler_params=_cparams(("arbitrary",)),
        name="attn_sample",
    )(page_table, qt, kts, vs, lq1, lk1, lq2, lk2, g_sub, ck_pages, cv_pages)


def _outproj_kernel(on_ref, cy_ref, x_ref, gt_ref, sh_ref, sc_ref, g2_ref, wa_ref, wc_ref, x1_ref, hn_ref, hnp_ref,
                    *, per_row):
    mix = (jnp.dot(on_ref[...].astype(BF16), wa_ref[...], preferred_element_type=F32)
           + jnp.dot(cy_ref[...], wc_ref[...], preferred_element_type=F32))
    x1 = x_ref[...] + _mod_rows(gt_ref, per_row) * mix
    x1_ref[...] = x1
    y = _rms(x1) * g2_ref[...]
    hn = y * (1.0 + _mod_rows(sc_ref, per_row)) + _mod_rows(sh_ref, per_row)
    hn_ref[...] = hn
    hnp_ref[...] = _pack_bf16_pairs(hn).reshape(hnp_ref.shape)


def _outproj(on, cy, x, mod, mod_block_rows, mod_row_block, per_row, g2, wo_a, wo_c, tm):
    t, d = x.shape
    row = lambda i: (i, 0)
    mod_spec = lambda chunk: pl.BlockSpec(
        (mod_block_rows, d), (lambda i: (i, chunk)) if per_row else (lambda i: (mod_row_block, chunk)))
    const = lambda shape: pl.BlockSpec(shape, lambda i: (0, 0))
    return pl.pallas_call(
        functools.partial(_outproj_kernel, per_row=per_row),
        out_shape=(jax.ShapeDtypeStruct((t, d), F32), jax.ShapeDtypeStruct((t, d), F32),
                   jax.ShapeDtypeStruct((t, 1, d // 2), I32)),
        grid=(t // tm,),
        in_specs=[pl.BlockSpec((tm, ATTN_WIDTH), row), pl.BlockSpec((tm, CONV_CH), row), pl.BlockSpec((tm, d), row),
                  mod_spec(2), mod_spec(3), mod_spec(4), const((1, d)),
                  const((ATTN_WIDTH, d)), const((CONV_CH, d))],
        out_specs=(pl.BlockSpec((tm, d), row), pl.BlockSpec((tm, d), row),
                   pl.BlockSpec((tm, 1, d // 2), lambda i: (i, 0, 0))),
        compiler_params=_cparams(("arbitrary",)),
        name="outproj",
    )(on, cy, x, mod, mod, mod, g2, wo_a, wo_c)


def _first_index_of_max(x, idx, axis):
    m = jnp.max(x, axis=axis, keepdims=True)
    return m, jnp.min(jnp.where(x == m, idx, float(N_EXPERTS)), axis=axis, keepdims=True)


def _route_kernel(hp_ref, hs_ref, wrh_ref, wrl_ref, bias_ref, tri_ref, idx_ref, wts_ref, pos_ref, cnt_ref, run_scr,
                  *, n_prompt_tiles):
    i = pl.program_id(0)

    @pl.when(i == 0)
    def _():
        run_scr[...] = jnp.zeros(run_scr.shape, F32)

    h = jnp.where(i < n_prompt_tiles, hp_ref[...], hs_ref[...])
    h_hi = h.astype(BF16)
    h_lo = (h - h_hi.astype(F32)).astype(BF16)
    nt = (((1,), (1,)), ((), ()))
    logits = (lax.dot_general(wrh_ref[...], h_hi, nt, preferred_element_type=F32)
              + lax.dot_general(wrh_ref[...], h_lo, nt, preferred_element_type=F32)
              + lax.dot_general(wrl_ref[...], h_hi, nt, preferred_element_type=F32))
    s = jax.nn.sigmoid(logits)
    sc = s + bias_ref[...]
    tm = s.shape[1]
    neg = -jnp.inf

    sc3 = sc.reshape(N_EXPERT_GROUPS, GROUP_SIZE, tm)
    in_grp = lax.broadcasted_iota(I32, sc3.shape, 1).astype(F32)
    m1, i1 = _first_index_of_max(sc3, in_grp, 1)
    m2 = jnp.max(jnp.where(in_grp == i1, neg, sc3), axis=1, keepdims=True)
    gs = (m1 + m2).reshape(N_EXPERT_GROUPS, tm)

    gidx = lax.broadcasted_iota(I32, gs.shape, 0).astype(F32)
    gsel = jnp.zeros(gs.shape, F32)
    for _ in range(TOPK_GROUPS):
        _, gi = _first_index_of_max(gs, gidx, 0)
        hit = gidx == gi
        gsel = jnp.where(hit, 1.0, gsel)
        gs = jnp.where(hit, neg, gs)
    emask = jnp.broadcast_to(gsel[:, None, :], sc3.shape).reshape(N_EXPERTS, tm)

    cand = jnp.where(emask > 0.0, sc, neg)
    eidx = lax.broadcasted_iota(I32, cand.shape, 0).astype(F32)
    onehot = jnp.zeros(cand.shape, F32)
    hits, idxs, ws = [], [], []
    for _ in range(TOP_K):
        _, ei = _first_index_of_max(cand, eidx, 0)
        hit = eidx == ei
        hits.append(hit)
        idxs.append(ei)
        ws.append(jnp.sum(jnp.where(hit, s, 0.0), axis=0, keepdims=True))
        cand = jnp.where(hit, neg, cand)
        onehot = jnp.where(hit, 1.0, onehot)
    w = jnp.concatenate(ws, axis=0)
    wts_ref[...] = w / jnp.sum(w, axis=0, keepdims=True) * ROUTED_SCALE
    idx_ref[...] = jnp.concatenate(idxs, axis=0).astype(I32)

    rank = run_scr[...] + jnp.dot(onehot.astype(BF16), tri_ref[...], preferred_element_type=F32)
    pos = [jnp.sum(jnp.where(hit, rank, 0.0), axis=0, keepdims=True) for hit in hits]
    pos_ref[...] = jnp.concatenate(pos, axis=0).astype(I32)
    run_scr[...] = run_scr[...] + jnp.sum(onehot, axis=1, keepdims=True)
    cnt_ref[...] = run_scr[...].astype(I32)


def _route(hn_p, hn_s, wr_hi, wr_lo, bias_col, tri, tm):
    (tp, d), ts = hn_p.shape, hn_s.shape[0]
    t = tp + ts
    npt, nst = tp // tm, ts // tm
    const = lambda shape: pl.BlockSpec(shape, lambda i: (0, 0))
    tok = pl.BlockSpec((TOP_K, tm), lambda i: (0, i))
    return pl.pallas_call(
        functools.partial(_route_kernel, n_prompt_tiles=npt),
        out_shape=(jax.ShapeDtypeStruct((TOP_K, t), I32), jax.ShapeDtypeStruct((TOP_K, t), F32),
                   jax.ShapeDtypeStruct((TOP_K, t), I32), jax.ShapeDtypeStruct((N_EXPERTS, 1), I32)),
        grid=(npt + nst,),
        in_specs=[pl.BlockSpec((tm, d), lambda i: (jnp.minimum(i, npt - 1), 0)),
                  pl.BlockSpec((tm, d), lambda i: (jnp.clip(i - npt, 0, nst - 1), 0)),
                  const((N_EXPERTS, d)), const((N_EXPERTS, d)), const((N_EXPERTS, 1)), const((tm, tm))],
        out_specs=(tok, tok, tok, const((N_EXPERTS, 1))),
        scratch_shapes=[pltpu.VMEM((N_EXPERTS, 1), F32)],
        compiler_params=_cparams(("arbitrary",)),
        name="route",
    )(hn_p, hn_s, wr_hi, wr_lo, bias_col, tri)


HI_HALF = -65536


def _pack_bf16_pairs(x):
    w = x.shape[1] // 2
    bits = lambda a: lax.bitcast_convert_type(a.astype(BF16).astype(F32), I32)
    return lax.shift_right_logical(bits(x[:, :w]), 16) | (bits(x[:, w:]) & HI_HALF)


def _unpack_bf16_pairs(u):
    lo = lax.bitcast_convert_type(lax.shift_left(u, 16), F32)
    hi = lax.bitcast_convert_type(u & HI_HALF, F32)
    return jnp.concatenate([lo, hi], axis=1)


def _wait_rows(src, dst, sem, n):
    pltpu.make_async_copy(src.at[pl.ds(0, n)], dst.at[pl.ds(0, n)], sem).wait()


def _dispatch_kernel(lo_ref, n_ref, nu_ref, dest_ref, hp_hbm, hs_hbm, xs_hbm, zbuf, sem, zsem, *, tm,
                     n_prompt_tiles, rows):
    i = pl.program_id(0)
    n = pl.num_programs(0)
    par = i % 2
    n_blocks = xs_hbm.shape[0] // rows

    def issue(src_hbm, row0):
        def body(t, carry):
            for k in range(TOP_K):
                slot = dest_ref[0, 0, k * tm + t]
                pltpu.make_async_copy(src_hbm.at[pl.ds(row0 + t, 1)], xs_hbm.at[pl.ds(slot, 1)],
                                      sem.at[par]).start()
            return carry
        lax.fori_loop(0, tm, body, 0)

    @pl.when(i < n_prompt_tiles)
    def _():
        issue(hp_hbm, i * tm)

    @pl.when(i >= n_prompt_tiles)
    def _():
        issue(hs_hbm, (i - n_prompt_tiles) * tm)

    def tail_copy(e):
        return pltpu.make_async_copy(zbuf.at[pl.ds(0, n_ref[e])], xs_hbm.at[pl.ds(lo_ref[e], n_ref[e])],
                                     zsem.at[0])

    def block_copy(b):
        return pltpu.make_async_copy(zbuf, xs_hbm.at[pl.ds(b * rows, rows)], zsem.at[0])

    def for_each_fill(act):
        def tail(e, carry):
            @pl.when(n_ref[e] > 0)
            def _():
                act(tail_copy(e))
            return carry

        def block(b, carry):
            act(block_copy(b))
            return carry

        lax.fori_loop(0, N_EXPERTS, tail, 0)
        lax.fori_loop(nu_ref[0], n_blocks, block, 0)

    @pl.when(i == 0)
    def _():
        zbuf[...] = jnp.zeros(zbuf.shape, zbuf.dtype)
        for_each_fill(lambda c: c.start())

    @pl.when(i >= 1)
    def _():
        _wait_rows(hp_hbm, xs_hbm, sem.at[1 - par], TOP_K * tm)

    @pl.when(i == n - 1)
    def _():
        _wait_rows(hp_hbm, xs_hbm, sem.at[par], TOP_K * tm)
        for_each_fill(lambda c: c.wait())


def _dispatch(pad_lo, pad_n, n_used, dest_tiles, hnp_p, hnp_s, n_slots, rows, tm):
    n_tiles = dest_tiles.shape[0]
    w = hnp_p.shape[2]
    grid_spec = pltpu.PrefetchScalarGridSpec(
        num_scalar_prefetch=3,
        grid=(n_tiles,),
        in_specs=[pl.BlockSpec((1, 1, TOP_K * tm), lambda i, lo, n, nu: (i, 0, 0), memory_space=pltpu.SMEM),
                  pl.BlockSpec(memory_space=pl.ANY), pl.BlockSpec(memory_space=pl.ANY)],
        out_specs=pl.BlockSpec(memory_space=pl.ANY),
        scratch_shapes=[pltpu.VMEM((rows, 1, w), I32), pltpu.SemaphoreType.DMA((2,)),
                        pltpu.SemaphoreType.DMA((1,))],
    )
    return pl.pallas_call(
        functools.partial(_dispatch_kernel, tm=tm, n_prompt_tiles=hnp_p.shape[0] // tm, rows=rows),
        out_shape=jax.ShapeDtypeStruct((n_slots, 1, w), I32),
        grid_spec=grid_spec,
        compiler_params=_cparams(("arbitrary",)),
        name="dispatch",
    )(pad_lo, pad_n, n_used, dest_tiles, hnp_p, hnp_s)


def _expert_kernel(be_ref, nu_ref, xs_ref, wg_ref, wu_ref, wd_ref, ys_ref, x2d, wgb, wub, wdb):
    b = pl.program_id(0)

    @pl.when(b >= nu_ref[0])
    def _():
        ys_ref[...] = jnp.zeros(ys_ref.shape, ys_ref.dtype)

    @pl.when(b < nu_ref[0])
    def _():
        changed = jnp.logical_or(b == 0, be_ref[b] != be_ref[jnp.maximum(b - 1, 0)])

        @pl.when(changed)
        def _():
            wgb[...] = wg_ref[...].astype(BF16)
            wub[...] = wu_ref[...].astype(BF16)
            wdb[...] = wd_ref[...].astype(BF16)

        x2d[...] = xs_ref[...].reshape(x2d.shape)
        x = _unpack_bf16_pairs(x2d[...]).astype(BF16)
        g = jnp.dot(x, wgb[...], preferred_element_type=F32)
        u = jnp.dot(x, wub[...], preferred_element_type=F32)
        hmid = (_silu(g) * u).astype(BF16)
        y = jnp.dot(hmid, wdb[...], preferred_element_type=F32)
        ys_ref[...] = _pack_bf16_pairs(y).reshape(ys_ref.shape)


def _experts(block_e, n_used, xs, w_gate, w_up, w_down, layer, rows):
    n_slots, _, w = xs.shape
    nblk = n_slots // rows
    d = 2 * w
    used = lambda b, nu: jnp.minimum(b, nu[0] - 1)
    wspec = lambda shape: pl.BlockSpec((None, None) + shape, lambda b, be, nu: (layer, be[used(b, nu)], 0, 0))
    grid_spec = pltpu.PrefetchScalarGridSpec(
        num_scalar_prefetch=2,
        grid=(nblk,),
        in_specs=[pl.BlockSpec((rows, 1, w), lambda b, be, nu: (used(b, nu), 0, 0)),
                  wspec((d, EXPERT_FF)), wspec((d, EXPERT_FF)), wspec((EXPERT_FF, d))],
        out_specs=pl.BlockSpec((rows, 1, w), lambda b, be, nu: (b, 0, 0)),
        scratch_shapes=[pltpu.VMEM((rows, w), I32),
                        pltpu.VMEM((d, EXPERT_FF), BF16), pltpu.VMEM((d, EXPERT_FF), BF16),
                        pltpu.VMEM((EXPERT_FF, d), BF16)],
    )
    return pl.pallas_call(
        _expert_kernel,
        out_shape=jax.ShapeDtypeStruct((n_slots, 1, w), I32),
        grid_spec=grid_spec,
        compiler_params=_cparams(("arbitrary",)),
        name="experts",
    )(block_e, n_used, xs, w_gate, w_up, w_down)


def _slot_plan(idx_t, pos_t, counts, rows, n_blocks):
    nblk_e = (counts + rows - 1) // rows
    blk_end = jnp.cumsum(nblk_e)
    blk_start = blk_end - nblk_e
    is_e = idx_t[:, :, None] == jnp.arange(N_EXPERTS, dtype=I32)[None, None, :]
    dest = jnp.sum(jnp.where(is_e, (blk_start * rows)[None, None, :], 0), axis=-1) + pos_t
    blocks = jnp.arange(n_blocks, dtype=I32)
    block_e = jnp.minimum(jnp.searchsorted(blk_end, blocks, side="right"), N_EXPERTS - 1).astype(I32)
    pad_lo = (blk_start * rows + counts).astype(I32)
    pad_n = (nblk_e * rows - counts).astype(I32)
    return dest.astype(I32), block_e, blk_end[-1:].astype(I32), pad_lo, pad_n


def _dest_tiles(dest, tm):
    t = dest.shape[1]
    return dest.reshape(TOP_K, t // tm, tm).transpose(1, 0, 2).reshape(t // tm, 1, TOP_K * tm)


def _combine_kernel(dest_ref, ndest_ref, wt_ref, hn_ref, x1_ref, gt_ref, wg_ref, wu_ref, wd_ref, ys_hbm, o_ref,
                    ybuf, y2d, sem, *, per_row, tm):
    i = pl.program_id(0)
    n = pl.num_programs(0)
    par = i % 2
    per_tile = TOP_K * tm

    def gather(tbl_ref, tile_slot):
        def body(t, carry):
            for k in range(TOP_K):
                slot = tbl_ref[0, 0, k * tm + t]
                pltpu.make_async_copy(ys_hbm.at[pl.ds(slot, 1)],
                                      ybuf.at[pl.ds(tile_slot * per_tile + k * tm + t, 1)],
                                      sem.at[tile_slot]).start()
            return carry
        lax.fori_loop(0, tm, body, 0)

    @pl.when(i == 0)
    def _():
        gather(dest_ref, 0)

    @pl.when(i + 1 < n)
    def _():
        gather(ndest_ref, 1 - par)

    h = hn_ref[...].astype(BF16)
    g = jnp.dot(h, wg_ref[...], preferred_element_type=F32)
    u = jnp.dot(h, wu_ref[...], preferred_element_type=F32)
    moe = jnp.dot((_silu(g) * u).astype(BF16), wd_ref[...], preferred_element_type=F32)

    _wait_rows(ys_hbm, ybuf.at[pl.ds(par * per_tile, per_tile)], sem.at[par], per_tile)
    wt = wt_ref[...]
    for k in range(TOP_K):
        y2d[...] = ybuf[pl.ds(pl.multiple_of(par * per_tile + k * tm, tm), tm)].reshape(y2d.shape)
        moe = moe + wt[:, k:k + 1] * _unpack_bf16_pairs(y2d[...])
    o_ref[...] = x1_ref[...] + _mod_rows(gt_ref, per_row) * moe


def _combine(ys, dest_tiles, wts, hn, x1, mod, mod_block_rows, mod_row_block, per_row, ws_g, ws_u, ws_d, tm):
    t, d = hn.shape
    w = ys.shape[2]
    n_tiles = t // tm
    row = lambda i: (i, 0)
    const = lambda shape: pl.BlockSpec(shape, lambda i: (0, 0))
    gt_spec = pl.BlockSpec((mod_block_rows, d), (lambda i: (i, 5)) if per_row else (lambda i: (mod_row_block, 5)))
    tile = lambda f: pl.BlockSpec((1, 1, TOP_K * tm), lambda i: (f(i), 0, 0), memory_space=pltpu.SMEM)
    return pl.pallas_call(
        functools.partial(_combine_kernel, per_row=per_row, tm=tm),
        out_shape=jax.ShapeDtypeStruct((t, d), F32),
        grid=(n_tiles,),
        in_specs=[tile(lambda i: i), tile(lambda i: jnp.minimum(i + 1, n_tiles - 1)),
                  pl.BlockSpec((tm, TOP_K), row), pl.BlockSpec((tm, d), row), pl.BlockSpec((tm, d), row),
                  gt_spec, const((d, EXPERT_FF)), const((d, EXPERT_FF)), const((EXPERT_FF, d)),
                  pl.BlockSpec(memory_space=pl.ANY)],
        out_specs=pl.BlockSpec((tm, d), row),
        scratch_shapes=[pltpu.VMEM((2 * TOP_K * tm, 1, w), I32), pltpu.VMEM((tm, w), I32),
                        pltpu.SemaphoreType.DMA((2,))],
        compiler_params=_cparams(("arbitrary",)),
        name="combine",
    )(dest_tiles, dest_tiles, wts, hn, x1, mod, ws_g, ws_u, ws_d, ys)


def _rope_tables(pos):
    half = HEAD_DIM_QK // 2
    inv = ROPE_THETA ** (-jnp.arange(half, dtype=F32) * 2.0 / HEAD_DIM_QK)
    ang = pos.astype(F32)[:, None] * inv[None, :]
    cos, sin = jnp.cos(ang), jnp.sin(ang)
    return jnp.tile(jnp.concatenate([cos, cos], axis=1), (1, 2)), jnp.tile(jnp.concatenate([-sin, sin], axis=1), (1, 2))


def _n_expert_blocks(n_assign, rows):
    return -(-(n_assign + N_EXPERTS * (rows - 1)) // rows)


def kernel(x_prompt, x_sample, c_prompt, c_sample, cache_k, cache_v, state_conv, page_table, w_ada, b_ada, g_norm1, g_norm2, w_in, g_q, g_k, lambda_q1, lambda_k1, lambda_q2, lambda_k2, g_sub, w_dw, b_dw, ln_g, ln_b, w_out, w_router, router_bias, w_e_gate, w_e_up, w_e_down, w_s_gate, w_s_up, w_s_down):
    layer = 0
    lam_init = 0.8 - 0.6 * math.exp(-0.3 * layer)
    d = D_MODEL
    tp = x_prompt.shape[1]
    ns = x_sample.shape[0]
    t_all = tp + ns
    xp = x_prompt.reshape(tp, d)
    xs = x_sample.reshape(ns, d)

    row2 = lambda a: a[layer].reshape(1, -1)
    w_in_bf = w_in[layer].astype(BF16)
    wo_a = w_out[layer, :ATTN_WIDTH].astype(BF16)
    wo_c = w_out[layer, ATTN_WIDTH:].astype(BF16)
    ws_g, ws_u, ws_d = (w[layer].astype(BF16) for w in (w_s_gate, w_s_up, w_s_down))
    wr_t = w_router[layer].T
    wr_hi = wr_t.astype(BF16)
    wr_lo = (wr_t - wr_hi.astype(F32)).astype(BF16)
    gq = jnp.tile(row2(g_q), (1, 2))
    gk = jnp.tile(row2(g_k), (1, 2))
    lane = jnp.arange(LANES)
    bd = (lane[:, None] // HEAD_DIM_QK == lane[None, :] // HEAD_DIM_QK).astype(BF16)
    lams = [row2(a) for a in (lambda_q1, lambda_k1, lambda_q2, lambda_k2)]

    c_all = jnp.concatenate([c_sample, c_prompt, jnp.zeros((SUBLANES - 1, d), F32)], axis=0)
    mod = _ada(c_all, w_ada[layer], row2(b_ada))
    p_mod = dict(mod_block_rows=SUBLANES, mod_row_block=ns // SUBLANES, per_row=False)
    s_mod = dict(mod_block_rows=ns, mod_row_block=0, per_row=True)

    cos_p, sin_p = _rope_tables(jnp.arange(tp, dtype=I32))
    cos_s, sin_s = _rope_tables(jnp.full((ns,), PAST_LEN, I32))

    q_p, k_p, kb_p, v_p, vb_p, glu_p = _inproj(xp, mod, g1=row2(g_norm1), w_in_bf=w_in_bf, gq=gq, gk=gk,
                                               cos=cos_p, sin=sin_p, bd=bd, tm=512, **p_mod)
    cy_p = _conv_prompt(glu_p, w_dw[layer], row2(b_dw), row2(ln_g), row2(ln_b))
    on_p = _flash_prompt(q_p, kb_p, vb_p, *lams, row2(g_sub), lam_init)

    q_s, k_s, _, v_s, _, glu_s = _inproj(xs, mod, g1=row2(g_norm1), w_in_bf=w_in_bf, gq=gq, gk=gk,
                                         cos=cos_s, sin=sin_s, bd=bd, tm=ns, **s_mod)
    cy_s = _conv_sample(state_conv[layer], glu_s, w_dw[layer], row2(b_dw), row2(ln_g), row2(ln_b))
    n_phys = cache_k.shape[1]
    ck_pages = jnp.transpose(cache_k, (0, 1, 3, 4, 5, 2)).reshape(-1, ATTN_WIDTH, PAGE_SIZE)
    cv_pages = cache_v.reshape(-1, PAGE_SIZE * N_HEADS, HEAD_DIM_V)
    to_cols = lambda a: jnp.pad(jnp.swapaxes(a.reshape(ns, 2 * N_HEADS, HEAD_DIM_QK), 1, 2),
                                ((0, 0), (0, 0), (0, LANES - 2 * N_HEADS)))
    qt_s = to_cols(q_s[0].astype(F32) + q_s[1].astype(F32))
    on_s = _attn_sample(page_table + layer * n_phys, qt_s, to_cols(k_s), v_s.reshape(ns, N_HEADS, HEAD_DIM_V),
                        *lams, row2(g_sub), ck_pages, cv_pages, lam_init).reshape(ns, ATTN_WIDTH)

    x1_p, hn_p, hnp_p = _outproj(on_p, cy_p, xp, mod, g2=row2(g_norm2), wo_a=wo_a, wo_c=wo_c, tm=256, **p_mod)
    x1_s, hn_s, hnp_s = _outproj(on_s, cy_s, xs, mod, g2=row2(g_norm2), wo_a=wo_a, wo_c=wo_c, tm=ns, **s_mod)

    tm_r = ns
    tri = (jnp.arange(tm_r)[:, None] < jnp.arange(tm_r)[None, :]).astype(BF16)
    idx_t, wts_t, pos_t, counts = _route(hn_p, hn_s, wr_hi, wr_lo, router_bias[layer].reshape(-1, 1), tri, tm_r)
    n_blocks = _n_expert_blocks(TOP_K * t_all, EXPERT_ROWS)
    dest, block_e, n_used, pad_lo, pad_n = _slot_plan(idx_t, pos_t, counts.reshape(-1), EXPERT_ROWS, n_blocks)

    slots_x = _dispatch(pad_lo, pad_n, n_used, _dest_tiles(dest, tm_r), hnp_p, hnp_s, n_blocks * EXPERT_ROWS,
                        EXPERT_ROWS, tm_r)
    slots_y = _experts(block_e, n_used, slots_x, w_e_gate, w_e_up, w_e_down, layer, EXPERT_ROWS)
    tm_c = 256
    y_p = _combine(slots_y, _dest_tiles(dest[:, :tp], tm_c), wts_t[:, :tp].T, hn_p, x1_p, mod,
                   ws_g=ws_g, ws_u=ws_u, ws_d=ws_d, tm=tm_c, **p_mod)
    y_s = _combine(slots_y, _dest_tiles(dest[:, tp:], ns), wts_t[:, tp:].T, hn_s, x1_s, mod,
                   ws_g=ws_g, ws_u=ws_u, ws_d=ws_d, tm=ns, **s_mod)

    conv_p = glu_p[tp - CONV_STATE:]
    conv_s = jnp.concatenate([state_conv[layer][:, 1:], glu_s[:, None, :]], axis=1)
    return (y_p.reshape(1, tp, d), y_s.reshape(ns, 1, d),
            k_p.reshape(1, 1, tp, N_HEADS, 2, HEAD_DIM_QK), v_p.reshape(1, 1, tp, N_HEADS, HEAD_DIM_V),
            conv_p.reshape(1, 1, CONV_STATE, CONV_CH),
            k_s.reshape(1, ns, 1, N_HEADS, 2, HEAD_DIM_QK), v_s.reshape(1, ns, 1, N_HEADS, HEAD_DIM_V),
            conv_s.reshape(1, ns, CONV_STATE, CONV_CH))
```

```python
import functools
import math

import jax
import jax.numpy as jnp
from jax import lax
from jax.experimental import pallas as pl
from jax.experimental.pallas import tpu as pltpu

F32 = jnp.float32
BF16 = jnp.bfloat16
I32 = jnp.int32

D_MODEL = 2048
ATTN_WIDTH = 1024
CONV_CH = 1024
N_HEADS = 8
HEAD_DIM_V = 128
HEAD_DIM_QK = 64
CONV_LEN = 31
CONV_STATE = CONV_LEN - 1
N_EXPERTS = 256
TOP_K = 8
N_EXPERT_GROUPS = 8
GROUP_SIZE = N_EXPERTS // N_EXPERT_GROUPS
TOPK_GROUPS = 4
EXPERT_FF = 512
ROUTED_SCALE = 2.5
ROPE_THETA = 10000.0
NORM_EPS = 1e-6
PAST_LEN = 2048
PAGE_SIZE = 128
IN_COLS = 3 * ATTN_WIDTH + 2 * CONV_CH
LANES = 128
SUBLANES = 8
COL_TILE = 1024
CONV_HALO = 32
EXPERT_ROWS = 320
VMEM_LIMIT = 56 * 1024 * 1024
Q_SCALE = HEAD_DIM_QK ** -0.5 * math.log2(math.e)


def _cparams(sem, vmem=VMEM_LIMIT):
    return pltpu.CompilerParams(dimension_semantics=sem, vmem_limit_bytes=vmem)


def _rms(x, eps=NORM_EPS):
    return x * lax.rsqrt(jnp.mean(x * x, axis=-1, keepdims=True) + eps)


def _silu(x):
    return x * jax.nn.sigmoid(x)


def _mod_rows(ref, per_row):
    return ref[...] if per_row else ref[0:1, :]


def _diff_lambda(lq1_ref, lk1_ref, lq2_ref, lk2_ref, lam_init):
    a = jnp.sum(lq1_ref[...] * lk1_ref[...], axis=-1, keepdims=True)
    b = jnp.sum(lq2_ref[...] * lk2_ref[...], axis=-1, keepdims=True)
    return jnp.exp(a) - jnp.exp(b) + lam_init


def _ada_kernel(c_ref, w_ref, b_ref, o_ref):
    s = _silu(c_ref[...]).astype(BF16)
    o_ref[...] = jnp.dot(s, w_ref[...].astype(BF16), preferred_element_type=F32) + b_ref[...]


def _ada(c_all, w_ada, b_ada):
    rows, d = c_all.shape
    n = w_ada.shape[1]
    tn = 1024
    return pl.pallas_call(
        _ada_kernel,
        out_shape=jax.ShapeDtypeStruct((rows, n), F32),
        grid=(n // tn,),
        in_specs=[pl.BlockSpec((rows, d), lambda j: (0, 0)),
                  pl.BlockSpec((d, tn), lambda j: (0, j)),
                  pl.BlockSpec((1, tn), lambda j: (0, j))],
        out_specs=pl.BlockSpec((rows, tn), lambda j: (0, j)),
        compiler_params=_cparams(("arbitrary",)),
        name="adaln",
    )(c_all, w_ada, b_ada)


def _qk_norm_rope(u, g, cos, sin, bd):
    lane = lax.broadcasted_iota(I32, (u.shape[0], LANES), 1)
    first_half = (lane & 32) == 0
    outs = []
    for c in range(u.shape[1] // LANES):
        x = u[:, c * LANES:(c + 1) * LANES]
        ss = jnp.dot((x * x).astype(BF16), bd, preferred_element_type=F32)
        xn = x * lax.rsqrt(ss * (1.0 / HEAD_DIM_QK) + NORM_EPS) * g
        rot = jnp.where(first_half, pltpu.roll(xn, 96, 1), pltpu.roll(xn, 32, 1))
        outs.append(xn * cos + rot * sin)
    return jnp.concatenate(outs, axis=1)


def _inproj_kernel(x_ref, sh_ref, sc_ref, g1_ref, w_ref, gq_ref, gk_ref, cos_ref, sin_ref, bd_ref,
                   q_ref, k_ref, kb_ref, v_ref, vb_ref, glu_ref, xn_scr, val_scr, *, per_row):
    j = pl.program_id(1)

    @pl.when(j == 0)
    def _():
        y = _rms(x_ref[...]) * g1_ref[...]
        xn_scr[...] = (y * (1.0 + _mod_rows(sc_ref, per_row)) + _mod_rows(sh_ref, per_row)).astype(BF16)

    u = jnp.dot(xn_scr[...], w_ref[...], preferred_element_type=F32)

    @pl.when(j == 0)
    def _():
        q = _qk_norm_rope(u, gq_ref[...], cos_ref[...], sin_ref[...], bd_ref[...]) * Q_SCALE
        lane = lax.broadcasted_iota(I32, q.shape, 1)
        map0 = (lane & HEAD_DIM_QK) == 0
        q_ref[0] = jnp.where(map0, q, 0.0).astype(BF16)
        q_ref[1] = jnp.where(map0, 0.0, q).astype(BF16)

    @pl.when(j == 1)
    def _():
        k = _qk_norm_rope(u, gk_ref[...], cos_ref[...], sin_ref[...], bd_ref[...])
        k_ref[...] = k
        kb_ref[...] = k.astype(BF16)

    @pl.when(j == 2)
    def _():
        v_ref[...] = u
        vb_ref[...] = u.astype(BF16)

    @pl.when(j == 3)
    def _():
        val_scr[...] = u

    @pl.when(j == 4)
    def _():
        glu_ref[...] = val_scr[...] * jax.nn.sigmoid(u)


def _inproj(x, mod, mod_block_rows, mod_row_block, per_row, g1, w_in_bf, gq, gk, cos, sin, bd, tm):
    t, d = x.shape
    nj = IN_COLS // COL_TILE
    row = lambda i, j: (i, 0)
    mod_spec = lambda chunk: pl.BlockSpec(
        (mod_block_rows, d), (lambda i, j: (i, chunk)) if per_row else (lambda i, j: (mod_row_block, chunk)))
    const = lambda shape: pl.BlockSpec(shape, lambda i, j: (0, 0))
    return pl.pallas_call(
        functools.partial(_inproj_kernel, per_row=per_row),
        out_shape=(jax.ShapeDtypeStruct((2, t, COL_TILE), BF16),
                   jax.ShapeDtypeStruct((t, COL_TILE), F32), jax.ShapeDtypeStruct((t, COL_TILE), BF16),
                   jax.ShapeDtypeStruct((t, COL_TILE), F32), jax.ShapeDtypeStruct((t, COL_TILE), BF16),
                   jax.ShapeDtypeStruct((t, COL_TILE), F32)),
        grid=(t // tm, nj),
        in_specs=[pl.BlockSpec((tm, d), row), mod_spec(0), mod_spec(1), const((1, d)),
                  pl.BlockSpec((d, COL_TILE), lambda i, j: (0, j)),
                  const((1, LANES)), const((1, LANES)),
                  pl.BlockSpec((tm, LANES), row), pl.BlockSpec((tm, LANES), row), const((LANES, LANES))],
        out_specs=(pl.BlockSpec((2, tm, COL_TILE), lambda i, j: (0, i, 0)),
                   pl.BlockSpec((tm, COL_TILE), row), pl.BlockSpec((tm, COL_TILE), row),
                   pl.BlockSpec((tm, COL_TILE), row), pl.BlockSpec((tm, COL_TILE), row),
                   pl.BlockSpec((tm, COL_TILE), row)),
        scratch_shapes=[pltpu.VMEM((tm, d), BF16), pltpu.VMEM((tm, COL_TILE), F32)],
        compiler_params=_cparams(("arbitrary", "arbitrary")),
        name="inproj",
    )(x, mod, mod, g1, w_in_bf, gq, gk, cos, sin, bd)


def _layernorm_swish(y, g, b):
    mu = jnp.mean(y, axis=-1, keepdims=True)
    yc = y - mu
    var = jnp.mean(yc * yc, axis=-1, keepdims=True)
    yn = yc * lax.rsqrt(var + NORM_EPS) * g + b
    return yn * jax.nn.sigmoid(yn)


def _conv_kernel(cur_ref, prev_ref, w_ref, b_ref, lg_ref, lb_ref, y_ref, xs_scr, sh_scr, *, tm, rc):
    i = pl.program_id(0)
    xs_scr[0:CONV_HALO, :] = jnp.where(i == 0, 0.0, prev_ref[...])
    xs_scr[CONV_HALO:, :] = cur_ref[...]
    n_sh = tm + CONV_HALO - SUBLANES
    for b in range(1, SUBLANES):
        sh_scr[b - 1] = xs_scr[b:b + n_sh, :]

    first = CONV_HALO - CONV_STATE

    def chunk(r, carry):
        r0 = pl.multiple_of(r * rc, rc)
        acc = jnp.zeros((rc, CONV_CH), F32)
        for j in range(CONV_LEN):
            o = first + j
            b, a = o % SUBLANES, (o // SUBLANES) * SUBLANES
            src = xs_scr if b == 0 else sh_scr.at[b - 1]
            acc = acc + w_ref[j:j + 1, :] * src[pl.ds(r0 + a, rc), :]
        y = _layernorm_swish(acc + b_ref[...], lg_ref[...], lb_ref[...])
        y_ref[pl.ds(r0, rc), :] = y.astype(y_ref.dtype)
        return carry

    lax.fori_loop(0, tm // rc, chunk, 0)


def _conv_prompt(glu, w_dw, b_dw, ln_g, ln_b, tm=256, rc=32):
    t, c = glu.shape
    per = tm // CONV_HALO
    const = lambda shape: pl.BlockSpec(shape, lambda i: (0, 0))
    return pl.pallas_call(
        functools.partial(_conv_kernel, tm=tm, rc=rc),
        out_shape=jax.ShapeDtypeStruct((t, c), BF16),
        grid=(t // tm,),
        in_specs=[pl.BlockSpec((tm, c), lambda i: (i, 0)),
                  pl.BlockSpec((CONV_HALO, c), lambda i: (jnp.maximum(i * per - 1, 0), 0)),
                  const((CONV_LEN, c)), const((1, c)), const((1, c)), const((1, c))],
        out_specs=pl.BlockSpec((tm, c), lambda i: (i, 0)),
        scratch_shapes=[pltpu.VMEM((tm + CONV_HALO, c), F32),
                        pltpu.VMEM((SUBLANES - 1, tm + CONV_HALO - SUBLANES, c), F32)],
        compiler_params=_cparams(("arbitrary",)),
        name="conv_prompt",
    )(glu, glu, w_dw, b_dw, ln_g, ln_b)


def _conv_sample_kernel(st_ref, glu_ref, w_ref, b_ref, lg_ref, lb_ref, y_ref):
    w = w_ref[...]
    acc = jnp.sum(st_ref[...] * w[None, 0:CONV_STATE, :], axis=1) + glu_ref[...] * w[CONV_STATE:CONV_LEN, :]
    y_ref[...] = _layernorm_swish(acc + b_ref[...], lg_ref[...], lb_ref[...]).astype(y_ref.dtype)


def _conv_sample(state, glu, w_dw, b_dw, ln_g, ln_b, sb=32):
    b, s, c = state.shape
    const = lambda shape: pl.BlockSpec(shape, lambda i: (0, 0))
    return pl.pallas_call(
        _conv_sample_kernel,
        out_shape=jax.ShapeDtypeStruct((b, c), BF16),
        grid=(b // sb,),
        in_specs=[pl.BlockSpec((sb, s, c), lambda i: (i, 0, 0)), pl.BlockSpec((sb, c), lambda i: (i, 0)),
                  const((CONV_LEN, c)), const((1, c)), const((1, c)), const((1, c))],
        out_specs=pl.BlockSpec((sb, c), lambda i: (i, 0)),
        compiler_params=_cparams(("arbitrary",)),
        name="conv_sample",
    )(state, glu, w_dw, b_dw, ln_g, ln_b)


def _flash_kernel(q_ref, k_ref, v_ref, lq1_ref, lk1_ref, lq2_ref, lk2_ref, gs_ref, o_ref,
                  vx_scr, m_scr, acc_scr, sa_scr, sb_scr, *, tq, tk, lam_init):
    i = pl.program_id(1)
    dv = HEAD_DIM_V

    @pl.when(i == 0)
    def _():
        vx_scr[:, 0:dv] = v_ref[...]
        vx_scr[:, dv:2 * dv] = jnp.ones((vx_scr.shape[0], dv), BF16)

    q = q_ref[...].reshape(2 * tq, LANES)
    m_scr[...] = jnp.full(m_scr.shape, -jnp.inf, F32)
    acc_scr[...] = jnp.zeros(acc_scr.shape, F32)

    def scores(j, s_ref):
        k = k_ref[pl.ds(pl.multiple_of(j * tk, tk), tk), :]
        s_ref[...] = lax.dot_general(q, k, (((1,), (1,)), ((), ())), preferred_element_type=F32)

    def consume(j, s_ref, masked):
        k0 = pl.multiple_of(j * tk, tk)
        s = s_ref[...]
        if masked:
            row = lax.broadcasted_iota(I32, s.shape, 0)
            col = lax.broadcasted_iota(I32, s.shape, 1)
            qpos = i * tq + jnp.where(row >= tq, row - tq, row)
            s = jnp.where(k0 + col <= qpos, s, -jnp.inf)
        m_prev = m_scr[...]
        m_new = jnp.maximum(m_prev, jnp.max(s, axis=-1, keepdims=True))
        alpha = jnp.exp2(m_prev - m_new)
        p = jnp.exp2(s - m_new).astype(BF16)
        acc_scr[...] = alpha * acc_scr[...] + jnp.dot(p, vx_scr[pl.ds(k0, tk), :], preferred_element_type=F32)
        m_scr[...] = m_new

    n_full = (i * tq) // tk
    scores(0, sa_scr)

    def pair(t, carry):
        scores(2 * t + 1, sb_scr)
        consume(2 * t, sa_scr, False)
        scores(2 * t + 2, sa_scr)
        consume(2 * t + 1, sb_scr, False)
        return carry

    lax.fori_loop(0, n_full // 2, pair, 0)

    @pl.when(n_full % 2 == 1)
    def _():
        scores(n_full, sb_scr)
        consume(n_full - 1, sa_scr, False)
        consume(n_full, sb_scr, True)

    @pl.when(n_full % 2 == 0)
    def _():
        consume(n_full, sa_scr, True)

    acc = acc_scr[...]
    o = acc[:, 0:dv] / acc[:, dv:2 * dv]
    lam = _diff_lambda(lq1_ref, lk1_ref, lq2_ref, lk2_ref, lam_init)
    od = o[0:tq] - lam * o[tq:2 * tq]
    o_ref[...] = (_rms(od) * gs_ref[...] * (1.0 - lam_init)).astype(o_ref.dtype)


def _flash_prompt(q2, kb, vb, lq1, lk1, lq2, lk2, g_sub, lam_init, tq=256, tk=1024):
    _, t, _ = q2.shape
    tk = min(tk, t)
    assert tk % tq == 0 and t % tk == 0
    const = lambda shape: pl.BlockSpec(shape, lambda h, i: (0, 0))
    return pl.pallas_call(
        functools.partial(_flash_kernel, tq=tq, tk=tk, lam_init=lam_init),
        out_shape=jax.ShapeDtypeStruct((t, ATTN_WIDTH), BF16),
        grid=(N_HEADS, t // tq),
        in_specs=[pl.BlockSpec((2, tq, LANES), lambda h, i: (0, i, h)),
                  pl.BlockSpec((t, LANES), lambda h, i: (0, h)),
                  pl.BlockSpec((t, LANES), lambda h, i: (0, h)),
                  const((1, HEAD_DIM_QK)), const((1, HEAD_DIM_QK)), const((1, HEAD_DIM_QK)),
                  const((1, HEAD_DIM_QK)), const((1, LANES))],
        out_specs=pl.BlockSpec((tq, LANES), lambda h, i: (i, h)),
        scratch_shapes=[pltpu.VMEM((t, 2 * HEAD_DIM_V), BF16), pltpu.VMEM((2 * tq, 1), F32),
                        pltpu.VMEM((2 * tq, 2 * HEAD_DIM_V), F32),
                        pltpu.VMEM((2 * tq, tk), F32), pltpu.VMEM((2 * tq, tk), F32)],
        compiler_params=_cparams(("arbitrary", "arbitrary")),
        name="flash_prompt",
    )(q2, kb, vb, lq1, lk1, lq2, lk2, g_sub)


def _paged_copies(pt_ref, ck_hbm, cv_hbm, kbuf, vbuf, sem, b, slot, n_pages):
    copies = []
    for p in range(n_pages):
        pg = pt_ref[b, p]
        copies.append(pltpu.make_async_copy(ck_hbm.at[pg], kbuf.at[slot, p], sem.at[0, slot]))
        copies.append(pltpu.make_async_copy(cv_hbm.at[pg], vbuf.at[slot, p], sem.at[1, slot]))
    return copies


def _sattn_kernel(pt_ref, qt_ref, kts_ref, vs_ref, lq1_ref, lk1_ref, lq2_ref, lk2_ref, gs_ref,
                  ck_hbm, cv_hbm, o_ref, kbuf, vbuf, qb_scr, s_scr, sem, *, n_pages, lam_init):
    b = pl.program_id(0)
    nb = pl.num_programs(0)
    slot = b % 2

    @pl.when(b == 0)
    def _():
        for c in _paged_copies(pt_ref, ck_hbm, cv_hbm, kbuf, vbuf, sem, 0, 0, n_pages):
            c.start()

    @pl.when(b + 1 < nb)
    def _():
        for c in _paged_copies(pt_ref, ck_hbm, cv_hbm, kbuf, vbuf, sem, b + 1, 1 - slot, n_pages):
            c.start()

    nr = 2 * N_HEADS
    dk = HEAD_DIM_QK
    qt = qt_ref[...]
    for r in range(nr):
        qb_scr[r] = jnp.broadcast_to(qt[:, r:r + 1], (dk, LANES))
    s_self_lanes = jnp.sum(qt * kts_ref[...], axis=0, keepdims=True)
    rr = lax.broadcasted_iota(I32, (nr, LANES), 0)
    ll = lax.broadcasted_iota(I32, (nr, LANES), 1)
    s_self = jnp.sum(jnp.where(ll == (rr % N_HEADS) * 2 + rr // N_HEADS, s_self_lanes, 0.0),
                     axis=-1, keepdims=True)

    for c in _paged_copies(pt_ref, ck_hbm, cv_hbm, kbuf, vbuf, sem, b, slot, n_pages):
        c.wait()

    for p in range(n_pages):
        for r in range(nr):
            h, c = divmod(r, 2)
            kt = kbuf[slot, p, pl.ds(r * dk, dk), :]
            row = c * N_HEADS + h
            s_scr[row:row + 1, p * PAGE_SIZE:(p + 1) * PAGE_SIZE] = jnp.sum(qb_scr[r] * kt, axis=0, keepdims=True)

    s = s_scr[...]
    m = jnp.maximum(jnp.max(s, axis=-1, keepdims=True), s_self)
    e = jnp.exp2(s - m)
    e_self = jnp.exp2(s_self - m)
    inv_l = 1.0 / (jnp.sum(e, axis=-1, keepdims=True) + e_self)
    lam = _diff_lambda(lq1_ref, lk1_ref, lq2_ref, lk2_ref, lam_init)
    w = inv_l[0:N_HEADS]
    w1 = lam * inv_l[N_HEADS:nr]
    pd = (e[0:N_HEADS] * w - e[N_HEADS:nr] * w1).astype(BF16)
    pd_self = e_self[0:N_HEADS] * w - e_self[N_HEADS:nr] * w1

    head_row = lax.broadcasted_iota(I32, (N_HEADS, HEAD_DIM_V), 0)
    o = pd_self * vs_ref[...]
    for h in range(N_HEADS):
        acc = jnp.zeros((N_HEADS, HEAD_DIM_V), F32)
        for p in range(n_pages):
            vh = vbuf[slot, p, pl.ds(h, PAGE_SIZE, stride=N_HEADS), :].astype(BF16)
            acc = acc + jnp.dot(pd[:, p * PAGE_SIZE:(p + 1) * PAGE_SIZE], vh, preferred_element_type=F32)
        o = o + jnp.where(head_row == h, acc, 0.0)
    o_ref[...] = _rms(o) * gs_ref[...] * (1.0 - lam_init)


def _attn_sample(page_table, qt, kts, vs, lq1, lk1, lq2, lk2, g_sub, ck_pages, cv_pages, lam_init):
    nb, n_pages = page_table.shape
    const2 = lambda shape: pl.BlockSpec(shape, lambda b, pt: (0, 0))
    per_seq = lambda shape: pl.BlockSpec((None,) + shape, lambda b, pt: (b, 0, 0))
    page_rows = ck_pages.shape[1]
    grid_spec = pltpu.PrefetchScalarGridSpec(
        num_scalar_prefetch=1,
        grid=(nb,),
        in_specs=[per_seq((HEAD_DIM_QK, LANES)), per_seq((HEAD_DIM_QK, LANES)), per_seq((N_HEADS, HEAD_DIM_V)),
                  const2((1, HEAD_DIM_QK)), const2((1, HEAD_DIM_QK)), const2((1, HEAD_DIM_QK)),
                  const2((1, HEAD_DIM_QK)), const2((1, HEAD_DIM_V)),
                  pl.BlockSpec(memory_space=pl.ANY), pl.BlockSpec(memory_space=pl.ANY)],
        out_specs=per_seq((N_HEADS, HEAD_DIM_V)),
        scratch_shapes=[pltpu.VMEM((2, n_pages, page_rows, LANES), F32),
                        pltpu.VMEM((2, n_pages, page_rows, LANES), F32),
                        pltpu.VMEM((2 * N_HEADS, HEAD_DIM_QK, LANES), F32),
                        pltpu.VMEM((2 * N_HEADS, n_pages * PAGE_SIZE), F32),
                        pltpu.SemaphoreType.DMA((2, 2))],
    )
    return pl.pallas_call(
        functools.partial(_sattn_kernel, n_pages=n_pages, lam_init=lam_init),
        out_shape=jax.ShapeDtypeStruct((nb, N_HEADS, HEAD_DIM_V), F32),
        grid_spec=grid_spec,
        compiler_params=_cparams(("arbitrary",)),
        name="attn_sample",
    )(page_table, qt, kts, vs, lq1, lk1, lq2, lk2, g_sub, ck_pages, cv_pages)


def _outproj_kernel(on_ref, cy_ref, x_ref, gt_ref, sh_ref, sc_ref, g2_ref, wa_ref, wc_ref, x1_ref, hn_ref, hnp_ref,
                    *, per_row):
    mix = (jnp.dot(on_ref[...].astype(BF16), wa_ref[...], preferred_element_type=F32)
           + jnp.dot(cy_ref[...], wc_ref[...], preferred_element_type=F32))
    x1 = x_ref[...] + _mod_rows(gt_ref, per_row) * mix
    x1_ref[...] = x1
    y = _rms(x1) * g2_ref[...]
    hn = y * (1.0 + _mod_rows(sc_ref, per_row)) + _mod_rows(sh_ref, per_row)
    hn_ref[...] = hn
    hnp_ref[...] = _pack_bf16_pairs(hn).reshape(hnp_ref.shape)


def _outproj(on, cy, x, mod, mod_block_rows, mod_row_block, per_row, g2, wo_a, wo_c, tm):
    t, d = x.shape
    row = lambda i: (i, 0)
    mod_spec = lambda chunk: pl.BlockSpec(
        (mod_block_rows, d), (lambda i: (i, chunk)) if per_row else (lambda i: (mod_row_block, chunk)))
    const = lambda shape: pl.BlockSpec(shape, lambda i: (0, 0))
    return pl.pallas_call(
        functools.partial(_outproj_kernel, per_row=per_row),
        out_shape=(jax.ShapeDtypeStruct((t, d), F32), jax.ShapeDtypeStruct((t, d), F32),
                   jax.ShapeDtypeStruct((t, 1, d // 2), I32)),
        grid=(t // tm,),
        in_specs=[pl.BlockSpec((tm, ATTN_WIDTH), row), pl.BlockSpec((tm, CONV_CH), row), pl.BlockSpec((tm, d), row),
                  mod_spec(2), mod_spec(3), mod_spec(4), const((1, d)),
                  const((ATTN_WIDTH, d)), const((CONV_CH, d))],
        out_specs=(pl.BlockSpec((tm, d), row), pl.BlockSpec((tm, d), row),
                   pl.BlockSpec((tm, 1, d // 2), lambda i: (i, 0, 0))),
        compiler_params=_cparams(("arbitrary",)),
        name="outproj",
    )(on, cy, x, mod, mod, mod, g2, wo_a, wo_c)


def _first_index_of_max(x, idx, axis):
    m = jnp.max(x, axis=axis, keepdims=True)
    return m, jnp.min(jnp.where(x == m, idx, float(N_EXPERTS)), axis=axis, keepdims=True)


def _route_kernel(hp_ref, hs_ref, wrh_ref, wrl_ref, bias_ref, tri_ref, idx_ref, wts_ref, pos_ref, cnt_ref, run_scr,
                  *, n_prompt_tiles):
    i = pl.program_id(0)

    @pl.when(i == 0)
    def _():
        run_scr[...] = jnp.zeros(run_scr.shape, F32)

    h = jnp.where(i < n_prompt_tiles, hp_ref[...], hs_ref[...])
    h_hi = h.astype(BF16)
    h_lo = (h - h_hi.astype(F32)).astype(BF16)
    nt = (((1,), (1,)), ((), ()))
    logits = (lax.dot_general(wrh_ref[...], h_hi, nt, preferred_element_type=F32)
              + lax.dot_general(wrh_ref[...], h_lo, nt, preferred_element_type=F32)
              + lax.dot_general(wrl_ref[...], h_hi, nt, preferred_element_type=F32))
    s = jax.nn.sigmoid(logits)
    sc = s + bias_ref[...]
    tm = s.shape[1]
    neg = -jnp.inf

    sc3 = sc.reshape(N_EXPERT_GROUPS, GROUP_SIZE, tm)
    in_grp = lax.broadcasted_iota(I32, sc3.shape, 1).astype(F32)
    m1, i1 = _first_index_of_max(sc3, in_grp, 1)
    m2 = jnp.max(jnp.where(in_grp == i1, neg, sc3), axis=1, keepdims=True)
    gs = (m1 + m2).reshape(N_EXPERT_GROUPS, tm)

    gidx = lax.broadcasted_iota(I32, gs.shape, 0).astype(F32)
    gsel = jnp.zeros(gs.shape, F32)
    for _ in range(TOPK_GROUPS):
        _, gi = _first_index_of_max(gs, gidx, 0)
        hit = gidx == gi
        gsel = jnp.where(hit, 1.0, gsel)
        gs = jnp.where(hit, neg, gs)
    emask = jnp.broadcast_to(gsel[:, None, :], sc3.shape).reshape(N_EXPERTS, tm)

    cand = jnp.where(emask > 0.0, sc, neg)
    eidx = lax.broadcasted_iota(I32, cand.shape, 0).astype(F32)
    onehot = jnp.zeros(cand.shape, F32)
    hits, idxs, ws = [], [], []
    for _ in range(TOP_K):
        _, ei = _first_index_of_max(cand, eidx, 0)
        hit = eidx == ei
        hits.append(hit)
        idxs.append(ei)
        ws.append(jnp.sum(jnp.where(hit, s, 0.0), axis=0, keepdims=True))
        cand = jnp.where(hit, neg, cand)
        onehot = jnp.where(hit, 1.0, onehot)
    w = jnp.concatenate(ws, axis=0)
    wts_ref[...] = w / jnp.sum(w, axis=0, keepdims=True) * ROUTED_SCALE
    idx_ref[...] = jnp.concatenate(idxs, axis=0).astype(I32)

    rank = run_scr[...] + jnp.dot(onehot.astype(BF16), tri_ref[...], preferred_element_type=F32)
    pos = [jnp.sum(jnp.where(hit, rank, 0.0), axis=0, keepdims=True) for hit in hits]
    pos_ref[...] = jnp.concatenate(pos, axis=0).astype(I32)
    run_scr[...] = run_scr[...] + jnp.sum(onehot, axis=1, keepdims=True)
    cnt_ref[...] = run_scr[...].astype(I32)


def _route(hn_p, hn_s, wr_hi, wr_lo, bias_col, tri, tm):
    (tp, d), ts = hn_p.shape, hn_s.shape[0]
    t = tp + ts
    npt, nst = tp // tm, ts // tm
    const = lambda shape: pl.BlockSpec(shape, lambda i: (0, 0))
    tok = pl.BlockSpec((TOP_K, tm), lambda i: (0, i))
    return pl.pallas_call(
        functools.partial(_route_kernel, n_prompt_tiles=npt),
        out_shape=(jax.ShapeDtypeStruct((TOP_K, t), I32), jax.ShapeDtypeStruct((TOP_K, t), F32),
                   jax.ShapeDtypeStruct((TOP_K, t), I32), jax.ShapeDtypeStruct((N_EXPERTS, 1), I32)),
        grid=(npt + nst,),
        in_specs=[pl.BlockSpec((tm, d), lambda i: (jnp.minimum(i, npt - 1), 0)),
                  pl.BlockSpec((tm, d), lambda i: (jnp.clip(i - npt, 0, nst - 1), 0)),
                  const((N_EXPERTS, d)), const((N_EXPERTS, d)), const((N_EXPERTS, 1)), const((tm, tm))],
        out_specs=(tok, tok, tok, const((N_EXPERTS, 1))),
        scratch_shapes=[pltpu.VMEM((N_EXPERTS, 1), F32)],
        compiler_params=_cparams(("arbitrary",)),
        name="route",
    )(hn_p, hn_s, wr_hi, wr_lo, bias_col, tri)


HI_HALF = -65536


def _pack_bf16_pairs(x):
    w = x.shape[1] // 2
    bits = lambda a: lax.bitcast_convert_type(a.astype(BF16).astype(F32), I32)
    return lax.shift_right_logical(bits(x[:, :w]), 16) | (bits(x[:, w:]) & HI_HALF)


def _unpack_bf16_pairs(u):
    lo = lax.bitcast_convert_type(lax.shift_left(u, 16), F32)
    hi = lax.bitcast_convert_type(u & HI_HALF, F32)
    return jnp.concatenate([lo, hi], axis=1)


def _wait_rows(src, dst, sem, n):
    pltpu.make_async_copy(src.at[pl.ds(0, n)], dst.at[pl.ds(0, n)], sem).wait()


def _dispatch_kernel(lo_ref, n_ref, nu_ref, dest_ref, hp_ref, hs_ref, xs_hbm, rowbuf, zbuf, sem, zsem, *, tm,
                     n_prompt_tiles, rows):
    i = pl.program_id(0)
    n = pl.num_programs(0)
    par = i % 2
    n_blocks = xs_hbm.shape[0] // rows
    base = pl.multiple_of(par * tm, tm)

    @pl.when(i < n_prompt_tiles)
    def _():
        rowbuf[pl.ds(base, tm)] = hp_ref[...]

    @pl.when(i >= n_prompt_tiles)
    def _():
        rowbuf[pl.ds(base, tm)] = hs_ref[...]

    def issue(t, carry):
        for k in range(TOP_K):
            slot = dest_ref[0, 0, k * tm + t]
            pltpu.make_async_copy(rowbuf.at[pl.ds(base + t, 1)], xs_hbm.at[pl.ds(slot, 1)], sem.at[par]).start()
        return carry

    lax.fori_loop(0, tm, issue, 0)

    def tail_copy(e):
        return pltpu.make_async_copy(zbuf.at[pl.ds(0, n_ref[e])], xs_hbm.at[pl.ds(lo_ref[e], n_ref[e])],
                                     zsem.at[0])

    def block_copy(b):
        return pltpu.make_async_copy(zbuf, xs_hbm.at[pl.ds(b * rows, rows)], zsem.at[0])

    def for_each_fill(act):
        def tail(e, carry):
            @pl.when(n_ref[e] > 0)
            def _():
                act(tail_copy(e))
            return carry

        def block(b, carry):
            act(block_copy(b))
            return carry

        lax.fori_loop(0, N_EXPERTS, tail, 0)
        lax.fori_loop(nu_ref[0], n_blocks, block, 0)

    @pl.when(i == 0)
    def _():
        zbuf[...] = jnp.zeros(zbuf.shape, zbuf.dtype)
        for_each_fill(lambda c: c.start())

    def wait_tile(p):
        for _ in range(TOP_K):
            _wait_rows(rowbuf, xs_hbm, sem.at[p], tm)

    @pl.when(i >= 1)
    def _():
        wait_tile(1 - par)

    @pl.when(i == n - 1)
    def _():
        wait_tile(par)
        for_each_fill(lambda c: c.wait())


def _dispatch(pad_lo, pad_n, n_used, dest_tiles, hnp_p, hnp_s, n_slots, rows, tm):
    n_tiles = dest_tiles.shape[0]
    w = hnp_p.shape[2]
    npt, nst = hnp_p.shape[0] // tm, hnp_s.shape[0] // tm
    grid_spec = pltpu.PrefetchScalarGridSpec(
        num_scalar_prefetch=3,
        grid=(n_tiles,),
        in_specs=[pl.BlockSpec((1, 1, TOP_K * tm), lambda i, lo, n, nu: (i, 0, 0), memory_space=pltpu.SMEM),
                  pl.BlockSpec((tm, 1, w), lambda i, lo, n, nu: (jnp.minimum(i, npt - 1), 0, 0)),
                  pl.BlockSpec((tm, 1, w), lambda i, lo, n, nu: (jnp.clip(i - npt, 0, nst - 1), 0, 0))],
        out_specs=pl.BlockSpec(memory_space=pl.ANY),
        scratch_shapes=[pltpu.VMEM((2 * tm, 1, w), I32), pltpu.VMEM((rows, 1, w), I32),
                        pltpu.SemaphoreType.DMA((2,)), pltpu.SemaphoreType.DMA((1,))],
    )
    return pl.pallas_call(
        functools.partial(_dispatch_kernel, tm=tm, n_prompt_tiles=hnp_p.shape[0] // tm, rows=rows),
        out_shape=jax.ShapeDtypeStruct((n_slots, 1, w), I32),
        grid_spec=grid_spec,
        compiler_params=_cparams(("arbitrary",)),
        name="dispatch",
    )(pad_lo, pad_n, n_used, dest_tiles, hnp_p, hnp_s)


def _expert_kernel(be_ref, nu_ref, xs_ref, wg_ref, wu_ref, wd_ref, ys_ref, x2d, wgb, wub, wdb):
    b = pl.program_id(0)

    @pl.when(b < nu_ref[0])
    def _():
        changed = jnp.logical_or(b == 0, be_ref[b] != be_ref[jnp.maximum(b - 1, 0)])

        @pl.when(changed)
        def _():
            wgb[...] = wg_ref[...].astype(BF16)
            wub[...] = wu_ref[...].astype(BF16)
            wdb[...] = wd_ref[...].astype(BF16)

        x2d[...] = xs_ref[...].reshape(x2d.shape)
        x = _unpack_bf16_pairs(x2d[...]).astype(BF16)
        g = jnp.dot(x, wgb[...], preferred_element_type=F32)
        u = jnp.dot(x, wub[...], preferred_element_type=F32)
        hmid = (_silu(g) * u).astype(BF16)
        y = jnp.dot(hmid, wdb[...], preferred_element_type=F32)
        ys_ref[...] = _pack_bf16_pairs(y).reshape(ys_ref.shape)


def _experts(block_e, n_used, xs, w_gate, w_up, w_down, layer, rows):
    n_slots, _, w = xs.shape
    nblk = n_slots // rows
    d = 2 * w
    used = lambda b, nu: jnp.minimum(b, nu[0] - 1)
    wspec = lambda shape: pl.BlockSpec((None, None) + shape, lambda b, be, nu: (layer, be[used(b, nu)], 0, 0))
    slot_spec = pl.BlockSpec((rows, 1, w), lambda b, be, nu: (used(b, nu), 0, 0))
    grid_spec = pltpu.PrefetchScalarGridSpec(
        num_scalar_prefetch=2,
        grid=(nblk,),
        in_specs=[slot_spec, wspec((d, EXPERT_FF)), wspec((d, EXPERT_FF)), wspec((EXPERT_FF, d))],
        out_specs=slot_spec,
        scratch_shapes=[pltpu.VMEM((rows, w), I32),
                        pltpu.VMEM((d, EXPERT_FF), BF16), pltpu.VMEM((d, EXPERT_FF), BF16),
                        pltpu.VMEM((EXPERT_FF, d), BF16)],
    )
    return pl.pallas_call(
        _expert_kernel,
        out_shape=jax.ShapeDtypeStruct((n_slots, 1, w), I32),
        grid_spec=grid_spec,
        input_output_aliases={2: 0},
        compiler_params=_cparams(("arbitrary",)),
        name="experts",
    )(block_e, n_used, xs, w_gate, w_up, w_down)


def _slot_plan(idx_t, pos_t, counts, rows, n_blocks):
    nblk_e = (counts + rows - 1) // rows
    blk_end = jnp.cumsum(nblk_e)
    blk_start = blk_end - nblk_e
    is_e = idx_t[:, :, None] == jnp.arange(N_EXPERTS, dtype=I32)[None, None, :]
    dest = jnp.sum(jnp.where(is_e, (blk_start * rows)[None, None, :], 0), axis=-1) + pos_t
    blocks = jnp.arange(n_blocks, dtype=I32)
    block_e = jnp.minimum(jnp.searchsorted(blk_end, blocks, side="right"), N_EXPERTS - 1).astype(I32)
    pad_lo = (blk_start * rows + counts).astype(I32)
    pad_n = (nblk_e * rows - counts).astype(I32)
    return dest.astype(I32), block_e, blk_end[-1:].astype(I32), pad_lo, pad_n


def _dest_tiles(dest, tm):
    t = dest.shape[1]
    return dest.reshape(TOP_K, t // tm, tm).transpose(1, 0, 2).reshape(t // tm, 1, TOP_K * tm)


def _combine_kernel(dest_ref, ndest_ref, wt_ref, hn_ref, x1_ref, gt_ref, wg_ref, wu_ref, wd_ref, ys_hbm, o_ref,
                    ybuf, y2d, sem, *, per_row, tm):
    i = pl.program_id(0)
    n = pl.num_programs(0)
    par = i % 2
    per_tile = TOP_K * tm

    def gather(tbl_ref, tile_slot):
        def body(t, carry):
            for k in range(TOP_K):
                slot = tbl_ref[0, 0, k * tm + t]
                pltpu.make_async_copy(ys_hbm.at[pl.ds(slot, 1)],
                                      ybuf.at[pl.ds(tile_slot * per_tile + k * tm + t, 1)],
                                      sem.at[tile_slot]).start()
            return carry
        lax.fori_loop(0, tm, body, 0)

    @pl.when(i == 0)
    def _():
        gather(dest_ref, 0)

    @pl.when(i + 1 < n)
    def _():
        gather(ndest_ref, 1 - par)

    h = hn_ref[...].astype(BF16)
    g = jnp.dot(h, wg_ref[...], preferred_element_type=F32)
    u = jnp.dot(h, wu_ref[...], preferred_element_type=F32)
    moe = jnp.dot((_silu(g) * u).astype(BF16), wd_ref[...], preferred_element_type=F32)

    _wait_rows(ys_hbm, ybuf.at[pl.ds(par * per_tile, per_tile)], sem.at[par], per_tile)
    wt = wt_ref[...]
    for k in range(TOP_K):
        y2d[...] = ybuf[pl.ds(pl.multiple_of(par * per_tile + k * tm, tm), tm)].reshape(y2d.shape)
        moe = moe + wt[:, k:k + 1] * _unpack_bf16_pairs(y2d[...])
    o_ref[...] = x1_ref[...] + _mod_rows(gt_ref, per_row) * moe


def _combine(ys, dest_tiles, wts, hn, x1, mod, mod_block_rows, mod_row_block, per_row, ws_g, ws_u, ws_d, tm):
    t, d = hn.shape
    w = ys.shape[2]
    n_tiles = t // tm
    row = lambda i: (i, 0)
    const = lambda shape: pl.BlockSpec(shape, lambda i: (0, 0))
    gt_spec = pl.BlockSpec((mod_block_rows, d), (lambda i: (i, 5)) if per_row else (lambda i: (mod_row_block, 5)))
    tile = lambda f: pl.BlockSpec((1, 1, TOP_K * tm), lambda i: (f(i), 0, 0), memory_space=pltpu.SMEM)
    return pl.pallas_call(
        functools.partial(_combine_kernel, per_row=per_row, tm=tm),
        out_shape=jax.ShapeDtypeStruct((t, d), F32),
        grid=(n_tiles,),
        in_specs=[tile(lambda i: i), tile(lambda i: jnp.minimum(i + 1, n_tiles - 1)),
                  pl.BlockSpec((tm, TOP_K), row), pl.BlockSpec((tm, d), row), pl.BlockSpec((tm, d), row),
                  gt_spec, const((d, EXPERT_FF)), const((d, EXPERT_FF)), const((EXPERT_FF, d)),
                  pl.BlockSpec(memory_space=pl.ANY)],
        out_specs=pl.BlockSpec((tm, d), row),
        scratch_shapes=[pltpu.VMEM((2 * TOP_K * tm, 1, w), I32), pltpu.VMEM((tm, w), I32),
                        pltpu.SemaphoreType.DMA((2,))],
        compiler_params=_cparams(("arbitrary",)),
        name="combine",
    )(dest_tiles, dest_tiles, wts, hn, x1, mod, ws_g, ws_u, ws_d, ys)


def _rope_tables(pos):
    half = HEAD_DIM_QK // 2
    inv = ROPE_THETA ** (-jnp.arange(half, dtype=F32) * 2.0 / HEAD_DIM_QK)
    ang = pos.astype(F32)[:, None] * inv[None, :]
    cos, sin = jnp.cos(ang), jnp.sin(ang)
    return jnp.tile(jnp.concatenate([cos, cos], axis=1), (1, 2)), jnp.tile(jnp.concatenate([-sin, sin], axis=1), (1, 2))


def _n_expert_blocks(n_assign, rows):
    return -(-(n_assign + N_EXPERTS * (rows - 1)) // rows)


def kernel(x_prompt, x_sample, c_prompt, c_sample, cache_k, cache_v, state_conv, page_table, w_ada, b_ada, g_norm1, g_norm2, w_in, g_q, g_k, lambda_q1, lambda_k1, lambda_q2, lambda_k2, g_sub, w_dw, b_dw, ln_g, ln_b, w_out, w_router, router_bias, w_e_gate, w_e_up, w_e_down, w_s_gate, w_s_up, w_s_down):
    layer = 0
    lam_init = 0.8 - 0.6 * math.exp(-0.3 * layer)
    d = D_MODEL
    tp = x_prompt.shape[1]
    ns = x_sample.shape[0]
    t_all = tp + ns
    xp = x_prompt.reshape(tp, d)
    xs = x_sample.reshape(ns, d)

    row2 = lambda a: a[layer].reshape(1, -1)
    w_in_bf = w_in[layer].astype(BF16)
    wo_a = w_out[layer, :ATTN_WIDTH].astype(BF16)
    wo_c = w_out[layer, ATTN_WIDTH:].astype(BF16)
    ws_g, ws_u, ws_d = (w[layer].astype(BF16) for w in (w_s_gate, w_s_up, w_s_down))
    wr_t = w_router[layer].T
    wr_hi = wr_t.astype(BF16)
    wr_lo = (wr_t - wr_hi.astype(F32)).astype(BF16)
    gq = jnp.tile(row2(g_q), (1, 2))
    gk = jnp.tile(row2(g_k), (1, 2))
    lane = jnp.arange(LANES)
    bd = (lane[:, None] // HEAD_DIM_QK == lane[None, :] // HEAD_DIM_QK).astype(BF16)
    lams = [row2(a) for a in (lambda_q1, lambda_k1, lambda_q2, lambda_k2)]

    c_all = jnp.concatenate([c_sample, c_prompt, jnp.zeros((SUBLANES - 1, d), F32)], axis=0)
    mod = _ada(c_all, w_ada[layer], row2(b_ada))
    p_mod = dict(mod_block_rows=SUBLANES, mod_row_block=ns // SUBLANES, per_row=False)
    s_mod = dict(mod_block_rows=ns, mod_row_block=0, per_row=True)

    cos_p, sin_p = _rope_tables(jnp.arange(tp, dtype=I32))
    cos_s, sin_s = _rope_tables(jnp.full((ns,), PAST_LEN, I32))

    q_p, k_p, kb_p, v_p, vb_p, glu_p = _inproj(xp, mod, g1=row2(g_norm1), w_in_bf=w_in_bf, gq=gq, gk=gk,
                                               cos=cos_p, sin=sin_p, bd=bd, tm=512, **p_mod)
    cy_p = _conv_prompt(glu_p, w_dw[layer], row2(b_dw), row2(ln_g), row2(ln_b))
    on_p = _flash_prompt(q_p, kb_p, vb_p, *lams, row2(g_sub), lam_init)

    q_s, k_s, _, v_s, _, glu_s = _inproj(xs, mod, g1=row2(g_norm1), w_in_bf=w_in_bf, gq=gq, gk=gk,
                                         cos=cos_s, sin=sin_s, bd=bd, tm=ns, **s_mod)
    cy_s = _conv_sample(state_conv[layer], glu_s, w_dw[layer], row2(b_dw), row2(ln_g), row2(ln_b))
    n_phys = cache_k.shape[1]
    ck_pages = jnp.transpose(cache_k, (0, 1, 3, 4, 5, 2)).reshape(-1, ATTN_WIDTH, PAGE_SIZE)
    cv_pages = cache_v.reshape(-1, PAGE_SIZE * N_HEADS, HEAD_DIM_V)
    to_cols = lambda a: jnp.pad(jnp.swapaxes(a.reshape(ns, 2 * N_HEADS, HEAD_DIM_QK), 1, 2),
                                ((0, 0), (0, 0), (0, LANES - 2 * N_HEADS)))
    qt_s = to_cols(q_s[0].astype(F32) + q_s[1].astype(F32))
    on_s = _attn_sample(page_table + layer * n_phys, qt_s, to_cols(k_s), v_s.reshape(ns, N_HEADS, HEAD_DIM_V),
                        *lams, row2(g_sub), ck_pages, cv_pages, lam_init).reshape(ns, ATTN_WIDTH)

    x1_p, hn_p, hnp_p = _outproj(on_p, cy_p, xp, mod, g2=row2(g_norm2), wo_a=wo_a, wo_c=wo_c, tm=256, **p_mod)
    x1_s, hn_s, hnp_s = _outproj(on_s, cy_s, xs, mod, g2=row2(g_norm2), wo_a=wo_a, wo_c=wo_c, tm=ns, **s_mod)

    tm_r = ns
    tri = (jnp.arange(tm_r)[:, None] < jnp.arange(tm_r)[None, :]).astype(BF16)
    idx_t, wts_t, pos_t, counts = _route(hn_p, hn_s, wr_hi, wr_lo, router_bias[layer].reshape(-1, 1), tri, tm_r)
    n_blocks = _n_expert_blocks(TOP_K * t_all, EXPERT_ROWS)
    dest, block_e, n_used, pad_lo, pad_n = _slot_plan(idx_t, pos_t, counts.reshape(-1), EXPERT_ROWS, n_blocks)

    slots_x = _dispatch(pad_lo, pad_n, n_used, _dest_tiles(dest, tm_r), hnp_p, hnp_s, n_blocks * EXPERT_ROWS,
                        EXPERT_ROWS, tm_r)
    slots_y = _experts(block_e, n_used, slots_x, w_e_gate, w_e_up, w_e_down, layer, EXPERT_ROWS)
    tm_c = 256
    y_p = _combine(slots_y, _dest_tiles(dest[:, :tp], tm_c), wts_t[:, :tp].T, hn_p, x1_p, mod,
                   ws_g=ws_g, ws_u=ws_u, ws_d=ws_d, tm=tm_c, **p_mod)
    y_s = _combine(slots_y, _dest_tiles(dest[:, tp:], ns), wts_t[:, tp:].T, hn_s, x1_s, mod,
                   ws_g=ws_g, ws_u=ws_u, ws_d=ws_d, tm=ns, **s_mod)

    conv_p = glu_p[tp - CONV_STATE:]
    conv_s = jnp.concatenate([state_conv[layer][:, 1:], glu_s[:, None, :]], axis=1)
    return (y_p.reshape(1, tp, d), y_s.reshape(ns, 1, d),
            k_p.reshape(1, 1, tp, N_HEADS, 2, HEAD_DIM_QK), v_p.reshape(1, 1, tp, N_HEADS, HEAD_DIM_V),
            conv_p.reshape(1, 1, CONV_STATE, CONV_CH),
            k_s.reshape(1, ns, 1, N_HEADS, 2, HEAD_DIM_QK), v_s.reshape(1, ns, 1, N_HEADS, HEAD_DIM_V),
            conv_s.reshape(1, ns, CONV_STATE, CONV_CH))
```

```python
import functools
import math

import jax
import jax.numpy as jnp
from jax import lax
from jax.experimental import pallas as pl
from jax.experimental.pallas import tpu as pltpu

F32 = jnp.float32
BF16 = jnp.bfloat16
I32 = jnp.int32

D_MODEL = 2048
ATTN_WIDTH = 1024
CONV_CH = 1024
N_HEADS = 8
HEAD_DIM_V = 128
HEAD_DIM_QK = 64
CONV_LEN = 31
CONV_STATE = CONV_LEN - 1
N_EXPERTS = 256
TOP_K = 8
N_EXPERT_GROUPS = 8
GROUP_SIZE = N_EXPERTS // N_EXPERT_GROUPS
TOPK_GROUPS = 4
EXPERT_FF = 512
ROUTED_SCALE = 2.5
ROPE_THETA = 10000.0
NORM_EPS = 1e-6
PAST_LEN = 2048
PAGE_SIZE = 128
IN_COLS = 3 * ATTN_WIDTH + 2 * CONV_CH
LANES = 128
SUBLANES = 8
COL_TILE = 1024
MXU_COLS = 256
CONV_HALO = 32
EXPERT_ROWS = 320
VMEM_LIMIT = 56 * 1024 * 1024
Q_SCALE = HEAD_DIM_QK ** -0.5 * math.log2(math.e)


def _cparams(sem, vmem=VMEM_LIMIT):
    return pltpu.CompilerParams(dimension_semantics=sem, vmem_limit_bytes=vmem)


def _rms(x, eps=NORM_EPS):
    return x * lax.rsqrt(jnp.mean(x * x, axis=-1, keepdims=True) + eps)


def _silu(x):
    return x * jax.nn.sigmoid(x)


def _mod_rows(ref, per_row):
    return ref[...] if per_row else ref[0:1, :]


def _diff_lambda(lq1_ref, lk1_ref, lq2_ref, lk2_ref, lam_init):
    a = jnp.sum(lq1_ref[...] * lk1_ref[...], axis=-1, keepdims=True)
    b = jnp.sum(lq2_ref[...] * lk2_ref[...], axis=-1, keepdims=True)
    return jnp.exp(a) - jnp.exp(b) + lam_init


def _ada_kernel(c_ref, w_ref, b_ref, o_ref):
    s = _silu(c_ref[...]).astype(BF16)
    o_ref[...] = jnp.dot(s, w_ref[...].astype(BF16), preferred_element_type=F32) + b_ref[...]


def _ada(c_all, w_ada, b_ada):
    rows, d = c_all.shape
    n = w_ada.shape[1]
    tn = 1024
    return pl.pallas_call(
        _ada_kernel,
        out_shape=jax.ShapeDtypeStruct((rows, n), F32),
        grid=(n // tn,),
        in_specs=[pl.BlockSpec((rows, d), lambda j: (0, 0)),
                  pl.BlockSpec((d, tn), lambda j: (0, j)),
                  pl.BlockSpec((1, tn), lambda j: (0, j))],
        out_specs=pl.BlockSpec((rows, tn), lambda j: (0, j)),
        compiler_params=_cparams(("arbitrary",)),
        name="adaln",
    )(c_all, w_ada, b_ada)


def _qk_norm_rope(u, g, cos, sin, bd):
    lane = lax.broadcasted_iota(I32, (u.shape[0], LANES), 1)
    first_half = (lane & 32) == 0
    outs = []
    for c in range(u.shape[1] // LANES):
        x = u[:, c * LANES:(c + 1) * LANES]
        ss = jnp.dot((x * x).astype(BF16), bd, preferred_element_type=F32)
        xn = x * lax.rsqrt(ss * (1.0 / HEAD_DIM_QK) + NORM_EPS) * g
        rot = jnp.where(first_half, pltpu.roll(xn, 96, 1), pltpu.roll(xn, 32, 1))
        outs.append(xn * cos + rot * sin)
    return jnp.concatenate(outs, axis=1)


def _inproj_kernel(x_ref, sh_ref, sc_ref, g1_ref, w_ref, gq_ref, gk_ref, cos_ref, sin_ref, bd_ref,
                   q_ref, k_ref, kb_ref, v_ref, vb_ref, glu_ref, xn_scr, val_scr, *, per_row):
    j = pl.program_id(1)

    @pl.when(j == 0)
    def _():
        y = _rms(x_ref[...]) * g1_ref[...]
        xn_scr[...] = (y * (1.0 + _mod_rows(sc_ref, per_row)) + _mod_rows(sh_ref, per_row)).astype(BF16)

    u = jnp.dot(xn_scr[...], w_ref[...], preferred_element_type=F32)

    @pl.when(j == 0)
    def _():
        q = _qk_norm_rope(u, gq_ref[...], cos_ref[...], sin_ref[...], bd_ref[...]) * Q_SCALE
        lane = lax.broadcasted_iota(I32, q.shape, 1)
        map0 = (lane & HEAD_DIM_QK) == 0
        q_ref[0] = jnp.where(map0, q, 0.0).astype(BF16)
        q_ref[1] = jnp.where(map0, 0.0, q).astype(BF16)

    @pl.when(j == 1)
    def _():
        k = _qk_norm_rope(u, gk_ref[...], cos_ref[...], sin_ref[...], bd_ref[...])
        k_ref[...] = k
        kb_ref[...] = k.astype(BF16)

    @pl.when(j == 2)
    def _():
        v_ref[...] = u
        vb_ref[...] = u.astype(BF16)

    @pl.when(j == 3)
    def _():
        val_scr[...] = u

    @pl.when(j == 4)
    def _():
        glu_ref[...] = val_scr[...] * jax.nn.sigmoid(u)


def _inproj(x, mod, mod_block_rows, mod_row_block, per_row, g1, w_in_bf, gq, gk, cos, sin, bd, tm):
    t, d = x.shape
    nj = IN_COLS // COL_TILE
    row = lambda i, j: (i, 0)
    mod_spec = lambda chunk: pl.BlockSpec(
        (mod_block_rows, d), (lambda i, j: (i, chunk)) if per_row else (lambda i, j: (mod_row_block, chunk)))
    const = lambda shape: pl.BlockSpec(shape, lambda i, j: (0, 0))
    return pl.pallas_call(
        functools.partial(_inproj_kernel, per_row=per_row),
        out_shape=(jax.ShapeDtypeStruct((2, t, COL_TILE), BF16),
                   jax.ShapeDtypeStruct((t, COL_TILE), F32), jax.ShapeDtypeStruct((t, COL_TILE), BF16),
                   jax.ShapeDtypeStruct((t, COL_TILE), F32), jax.ShapeDtypeStruct((t, COL_TILE), BF16),
                   jax.ShapeDtypeStruct((t, COL_TILE), F32)),
        grid=(t // tm, nj),
        in_specs=[pl.BlockSpec((tm, d), row), mod_spec(0), mod_spec(1), const((1, d)),
                  pl.BlockSpec((d, COL_TILE), lambda i, j: (0, j)),
                  const((1, LANES)), const((1, LANES)),
                  pl.BlockSpec((tm, LANES), row), pl.BlockSpec((tm, LANES), row), const((LANES, LANES))],
        out_specs=(pl.BlockSpec((2, tm, COL_TILE), lambda i, j: (0, i, 0)),
                   pl.BlockSpec((tm, COL_TILE), row), pl.BlockSpec((tm, COL_TILE), row),
                   pl.BlockSpec((tm, COL_TILE), row), pl.BlockSpec((tm, COL_TILE), row),
                   pl.BlockSpec((tm, COL_TILE), row)),
        scratch_shapes=[pltpu.VMEM((tm, d), BF16), pltpu.VMEM((tm, COL_TILE), F32)],
        compiler_params=_cparams(("arbitrary", "arbitrary")),
        name="inproj",
    )(x, mod, mod, g1, w_in_bf, gq, gk, cos, sin, bd)


def _layernorm_swish(y, g, b):
    mu = jnp.mean(y, axis=-1, keepdims=True)
    yc = y - mu
    var = jnp.mean(yc * yc, axis=-1, keepdims=True)
    yn = yc * lax.rsqrt(var + NORM_EPS) * g + b
    return yn * jax.nn.sigmoid(yn)


def _conv_kernel(cur_ref, prev_ref, w_ref, b_ref, lg_ref, lb_ref, y_ref, xs_scr, sh_scr, *, tm, rc):
    i = pl.program_id(0)
    xs_scr[0:CONV_HALO, :] = jnp.where(i == 0, 0.0, prev_ref[...])
    xs_scr[CONV_HALO:, :] = cur_ref[...]
    n_sh = tm + CONV_HALO - SUBLANES
    for b in range(1, SUBLANES):
        sh_scr[b - 1] = xs_scr[b:b + n_sh, :]

    first = CONV_HALO - CONV_STATE

    def chunk(r, carry):
        r0 = pl.multiple_of(r * rc, rc)
        acc = jnp.zeros((rc, CONV_CH), F32)
        for j in range(CONV_LEN):
            o = first + j
            b, a = o % SUBLANES, (o // SUBLANES) * SUBLANES
            src = xs_scr if b == 0 else sh_scr.at[b - 1]
            acc = acc + w_ref[j:j + 1, :] * src[pl.ds(r0 + a, rc), :]
        y = _layernorm_swish(acc + b_ref[...], lg_ref[...], lb_ref[...])
        y_ref[pl.ds(r0, rc), :] = y.astype(y_ref.dtype)
        return carry

    lax.fori_loop(0, tm // rc, chunk, 0)


def _conv_prompt(glu, w_dw, b_dw, ln_g, ln_b, tm=256, rc=32):
    t, c = glu.shape
    per = tm // CONV_HALO
    const = lambda shape: pl.BlockSpec(shape, lambda i: (0, 0))
    return pl.pallas_call(
        functools.partial(_conv_kernel, tm=tm, rc=rc),
        out_shape=jax.ShapeDtypeStruct((t, c), BF16),
        grid=(t // tm,),
        in_specs=[pl.BlockSpec((tm, c), lambda i: (i, 0)),
                  pl.BlockSpec((CONV_HALO, c), lambda i: (jnp.maximum(i * per - 1, 0), 0)),
                  const((CONV_LEN, c)), const((1, c)), const((1, c)), const((1, c))],
        out_specs=pl.BlockSpec((tm, c), lambda i: (i, 0)),
        scratch_shapes=[pltpu.VMEM((tm + CONV_HALO, c), F32),
                        pltpu.VMEM((SUBLANES - 1, tm + CONV_HALO - SUBLANES, c), F32)],
        compiler_params=_cparams(("arbitrary",)),
        name="conv_prompt",
    )(glu, glu, w_dw, b_dw, ln_g, ln_b)


def _conv_sample_kernel(st_ref, glu_ref, w_ref, b_ref, lg_ref, lb_ref, y_ref):
    w = w_ref[...]
    acc = jnp.sum(st_ref[...] * w[None, 0:CONV_STATE, :], axis=1) + glu_ref[...] * w[CONV_STATE:CONV_LEN, :]
    y_ref[...] = _layernorm_swish(acc + b_ref[...], lg_ref[...], lb_ref[...]).astype(y_ref.dtype)


def _conv_sample(state, glu, w_dw, b_dw, ln_g, ln_b, sb=32):
    b, s, c = state.shape
    const = lambda shape: pl.BlockSpec(shape, lambda i: (0, 0))
    return pl.pallas_call(
        _conv_sample_kernel,
        out_shape=jax.ShapeDtypeStruct((b, c), BF16),
        grid=(b // sb,),
        in_specs=[pl.BlockSpec((sb, s, c), lambda i: (i, 0, 0)), pl.BlockSpec((sb, c), lambda i: (i, 0)),
                  const((CONV_LEN, c)), const((1, c)), const((1, c)), const((1, c))],
        out_specs=pl.BlockSpec((sb, c), lambda i: (i, 0)),
        compiler_params=_cparams(("arbitrary",)),
        name="conv_sample",
    )(state, glu, w_dw, b_dw, ln_g, ln_b)


def _flash_kernel(q_ref, k_ref, v_ref, lq1_ref, lk1_ref, lq2_ref, lk2_ref, gs_ref, o_ref,
                  vx_scr, m_scr, acc_scr, sa_scr, sb_scr, *, tq, tk, lam_init):
    i = pl.program_id(1)
    dv = HEAD_DIM_V

    @pl.when(i == 0)
    def _():
        vx_scr[:, 0:dv] = v_ref[...]
        vx_scr[:, dv:2 * dv] = jnp.ones((vx_scr.shape[0], dv), BF16)

    q = q_ref[...].reshape(2 * tq, LANES)
    m_scr[...] = jnp.full(m_scr.shape, -jnp.inf, F32)
    acc_scr[...] = jnp.zeros(acc_scr.shape, F32)

    def scores(j, s_ref):
        k = k_ref[pl.ds(pl.multiple_of(j * tk, tk), tk), :]
        s_ref[...] = lax.dot_general(q, k, (((1,), (1,)), ((), ())), preferred_element_type=F32)

    def consume(j, s_ref, masked):
        k0 = pl.multiple_of(j * tk, tk)
        s = s_ref[...]
        if masked:
            row = lax.broadcasted_iota(I32, s.shape, 0)
            col = lax.broadcasted_iota(I32, s.shape, 1)
            qpos = i * tq + jnp.where(row >= tq, row - tq, row)
            s = jnp.where(k0 + col <= qpos, s, -jnp.inf)
        m_prev = m_scr[...]
        m_new = jnp.maximum(m_prev, jnp.max(s, axis=-1, keepdims=True))
        alpha = jnp.exp2(m_prev - m_new)
        p = jnp.exp2(s - m_new).astype(BF16)
        acc_scr[...] = alpha * acc_scr[...] + jnp.dot(p, vx_scr[pl.ds(k0, tk), :], preferred_element_type=F32)
        m_scr[...] = m_new

    n_full = (i * tq) // tk
    scores(0, sa_scr)

    def pair(t, carry):
        scores(2 * t + 1, sb_scr)
        consume(2 * t, sa_scr, False)
        scores(2 * t + 2, sa_scr)
        consume(2 * t + 1, sb_scr, False)
        return carry

    lax.fori_loop(0, n_full // 2, pair, 0)

    @pl.when(n_full % 2 == 1)
    def _():
        scores(n_full, sb_scr)
        consume(n_full - 1, sa_scr, False)
        consume(n_full, sb_scr, True)

    @pl.when(n_full % 2 == 0)
    def _():
        consume(n_full, sa_scr, True)

    acc = acc_scr[...]
    o = acc[:, 0:dv] / acc[:, dv:2 * dv]
    lam = _diff_lambda(lq1_ref, lk1_ref, lq2_ref, lk2_ref, lam_init)
    od = o[0:tq] - lam * o[tq:2 * tq]
    o_ref[...] = (_rms(od) * gs_ref[...] * (1.0 - lam_init)).astype(o_ref.dtype)


def _flash_prompt(q2, kb, vb, lq1, lk1, lq2, lk2, g_sub, lam_init, tq=256, tk=1024):
    _, t, _ = q2.shape
    tk = min(tk, t)
    assert tk % tq == 0 and t % tk == 0
    const = lambda shape: pl.BlockSpec(shape, lambda h, i: (0, 0))
    return pl.pallas_call(
        functools.partial(_flash_kernel, tq=tq, tk=tk, lam_init=lam_init),
        out_shape=jax.ShapeDtypeStruct((t, ATTN_WIDTH), BF16),
        grid=(N_HEADS, t // tq),
        in_specs=[pl.BlockSpec((2, tq, LANES), lambda h, i: (0, i, h)),
                  pl.BlockSpec((t, LANES), lambda h, i: (0, h)),
                  pl.BlockSpec((t, LANES), lambda h, i: (0, h)),
                  const((1, HEAD_DIM_QK)), const((1, HEAD_DIM_QK)), const((1, HEAD_DIM_QK)),
                  const((1, HEAD_DIM_QK)), const((1, LANES))],
        out_specs=pl.BlockSpec((tq, LANES), lambda h, i: (i, h)),
        scratch_shapes=[pltpu.VMEM((t, 2 * HEAD_DIM_V), BF16), pltpu.VMEM((2 * tq, 1), F32),
                        pltpu.VMEM((2 * tq, 2 * HEAD_DIM_V), F32),
                        pltpu.VMEM((2 * tq, tk), F32), pltpu.VMEM((2 * tq, tk), F32)],
        compiler_params=_cparams(("arbitrary", "arbitrary")),
        name="flash_prompt",
    )(q2, kb, vb, lq1, lk1, lq2, lk2, g_sub)


def _paged_copies(pt_ref, ck_hbm, cv_hbm, kbuf, vbuf, sem, b, slot, n_pages):
    copies = []
    for p in range(n_pages):
        pg = pt_ref[b, p]
        copies.append(pltpu.make_async_copy(ck_hbm.at[pg], kbuf.at[slot, p], sem.at[0, slot]))
        copies.append(pltpu.make_async_copy(cv_hbm.at[pg], vbuf.at[slot, p], sem.at[1, slot]))
    return copies


def _sattn_kernel(pt_ref, qt_ref, kts_ref, vs_ref, lq1_ref, lk1_ref, lq2_ref, lk2_ref, gs_ref,
                  ck_hbm, cv_hbm, o_ref, kbuf, vbuf, qb_scr, s_scr, sem, *, n_pages, lam_init):
    b = pl.program_id(0)
    nb = pl.num_programs(0)
    slot = b % 2

    @pl.when(b == 0)
    def _():
        for c in _paged_copies(pt_ref, ck_hbm, cv_hbm, kbuf, vbuf, sem, 0, 0, n_pages):
            c.start()

    @pl.when(b + 1 < nb)
    def _():
        for c in _paged_copies(pt_ref, ck_hbm, cv_hbm, kbuf, vbuf, sem, b + 1, 1 - slot, n_pages):
            c.start()

    nr = 2 * N_HEADS
    dk = HEAD_DIM_QK
    qt = qt_ref[...]
    for r in range(nr):
        qb_scr[r] = jnp.broadcast_to(qt[:, r:r + 1], (dk, LANES))
    s_self_lanes = jnp.sum(qt * kts_ref[...], axis=0, keepdims=True)
    rr = lax.broadcasted_iota(I32, (nr, LANES), 0)
    ll = lax.broadcasted_iota(I32, (nr, LANES), 1)
    s_self = jnp.sum(jnp.where(ll == (rr % N_HEADS) * 2 + rr // N_HEADS, s_self_lanes, 0.0),
                     axis=-1, keepdims=True)

    for c in _paged_copies(pt_ref, ck_hbm, cv_hbm, kbuf, vbuf, sem, b, slot, n_pages):
        c.wait()

    for p in range(n_pages):
        for r in range(nr):
            h, c = divmod(r, 2)
            kt = kbuf[slot, p, pl.ds(r * dk, dk), :]
            row = c * N_HEADS + h
            s_scr[row:row + 1, p * PAGE_SIZE:(p + 1) * PAGE_SIZE] = jnp.sum(qb_scr[r] * kt, axis=0, keepdims=True)

    s = s_scr[...]
    m = jnp.maximum(jnp.max(s, axis=-1, keepdims=True), s_self)
    e = jnp.exp2(s - m)
    e_self = jnp.exp2(s_self - m)
    inv_l = 1.0 / (jnp.sum(e, axis=-1, keepdims=True) + e_self)
    lam = _diff_lambda(lq1_ref, lk1_ref, lq2_ref, lk2_ref, lam_init)
    w = inv_l[0:N_HEADS]
    w1 = lam * inv_l[N_HEADS:nr]
    pd = (e[0:N_HEADS] * w - e[N_HEADS:nr] * w1).astype(BF16)
    pd_self = e_self[0:N_HEADS] * w - e_self[N_HEADS:nr] * w1

    head_row = lax.broadcasted_iota(I32, (N_HEADS, HEAD_DIM_V), 0)
    o = pd_self * vs_ref[...]
    for h in range(N_HEADS):
        acc = jnp.zeros((N_HEADS, HEAD_DIM_V), F32)
        for p in range(n_pages):
            vh = vbuf[slot, p, pl.ds(h, PAGE_SIZE, stride=N_HEADS), :].astype(BF16)
            acc = acc + jnp.dot(pd[:, p * PAGE_SIZE:(p + 1) * PAGE_SIZE], vh, preferred_element_type=F32)
        o = o + jnp.where(head_row == h, acc, 0.0)
    o_ref[...] = _rms(o) * gs_ref[...] * (1.0 - lam_init)


def _attn_sample(page_table, qt, kts, vs, lq1, lk1, lq2, lk2, g_sub, ck_pages, cv_pages, lam_init):
    nb, n_pages = page_table.shape
    const2 = lambda shape: pl.BlockSpec(shape, lambda b, pt: (0, 0))
    per_seq = lambda shape: pl.BlockSpec((None,) + shape, lambda b, pt: (b, 0, 0))
    page_rows = ck_pages.shape[1]
    grid_spec = pltpu.PrefetchScalarGridSpec(
        num_scalar_prefetch=1,
        grid=(nb,),
        in_specs=[per_seq((HEAD_DIM_QK, LANES)), per_seq((HEAD_DIM_QK, LANES)), per_seq((N_HEADS, HEAD_DIM_V)),
                  const2((1, HEAD_DIM_QK)), const2((1, HEAD_DIM_QK)), const2((1, HEAD_DIM_QK)),
                  const2((1, HEAD_DIM_QK)), const2((1, HEAD_DIM_V)),
                  pl.BlockSpec(memory_space=pl.ANY), pl.BlockSpec(memory_space=pl.ANY)],
        out_specs=per_seq((N_HEADS, HEAD_DIM_V)),
        scratch_shapes=[pltpu.VMEM((2, n_pages, page_rows, LANES), F32),
                        pltpu.VMEM((2, n_pages, page_rows, LANES), F32),
                        pltpu.VMEM((2 * N_HEADS, HEAD_DIM_QK, LANES), F32),
                        pltpu.VMEM((2 * N_HEADS, n_pages * PAGE_SIZE), F32),
                        pltpu.SemaphoreType.DMA((2, 2))],
    )
    return pl.pallas_call(
        functools.partial(_sattn_kernel, n_pages=n_pages, lam_init=lam_init),
        out_shape=jax.ShapeDtypeStruct((nb, N_HEADS, HEAD_DIM_V), F32),
        grid_spec=grid_spec,
        compiler_params=_cparams(("arbitrary",)),
        name="attn_sample",
    )(page_table, qt, kts, vs, lq1, lk1, lq2, lk2, g_sub, ck_pages, cv_pages)


def _outproj_kernel(on_ref, cy_ref, x_ref, gt_ref, sh_ref, sc_ref, g2_ref, wa_ref, wc_ref, x1_ref, hn_ref, hnp_ref,
                    *, per_row):
    mix = (jnp.dot(on_ref[...].astype(BF16), wa_ref[...], preferred_element_type=F32)
           + jnp.dot(cy_ref[...], wc_ref[...], preferred_element_type=F32))
    x1 = x_ref[...] + _mod_rows(gt_ref, per_row) * mix
    x1_ref[...] = x1
    y = _rms(x1) * g2_ref[...]
    hn = y * (1.0 + _mod_rows(sc_ref, per_row)) + _mod_rows(sh_ref, per_row)
    hn_ref[...] = hn
    hnp_ref[...] = _pack_bf16_pairs(hn).reshape(hnp_ref.shape)


def _outproj(on, cy, x, mod, mod_block_rows, mod_row_block, per_row, g2, wo_a, wo_c, tm):
    t, d = x.shape
    row = lambda i: (i, 0)
    mod_spec = lambda chunk: pl.BlockSpec(
        (mod_block_rows, d), (lambda i: (i, chunk)) if per_row else (lambda i: (mod_row_block, chunk)))
    const = lambda shape: pl.BlockSpec(shape, lambda i: (0, 0))
    return pl.pallas_call(
        functools.partial(_outproj_kernel, per_row=per_row),
        out_shape=(jax.ShapeDtypeStruct((t, d), F32), jax.ShapeDtypeStruct((t, d), F32),
                   jax.ShapeDtypeStruct((t, 1, d // 2), I32)),
        grid=(t // tm,),
        in_specs=[pl.BlockSpec((tm, ATTN_WIDTH), row), pl.BlockSpec((tm, CONV_CH), row), pl.BlockSpec((tm, d), row),
                  mod_spec(2), mod_spec(3), mod_spec(4), const((1, d)),
                  const((ATTN_WIDTH, d)), const((CONV_CH, d))],
        out_specs=(pl.BlockSpec((tm, d), row), pl.BlockSpec((tm, d), row),
                   pl.BlockSpec((tm, 1, d // 2), lambda i: (i, 0, 0))),
        compiler_params=_cparams(("arbitrary",)),
        name="outproj",
    )(on, cy, x, mod, mod, mod, g2, wo_a, wo_c)


def _first_index_of_max(x, idx, axis):
    m = jnp.max(x, axis=axis, keepdims=True)
    return m, jnp.min(jnp.where(x == m, idx, float(N_EXPERTS)), axis=axis, keepdims=True)


def _route_kernel(hp_ref, hs_ref, wrh_ref, wrl_ref, bias_ref, tri_ref, idx_ref, wts_ref, pos_ref, cnt_ref, run_scr,
                  *, n_prompt_tiles):
    i = pl.program_id(0)

    @pl.when(i == 0)
    def _():
        run_scr[...] = jnp.zeros(run_scr.shape, F32)

    h = jnp.where(i < n_prompt_tiles, hp_ref[...], hs_ref[...])
    h_hi = h.astype(BF16)
    h_lo = (h - h_hi.astype(F32)).astype(BF16)
    nt = (((1,), (1,)), ((), ()))
    logits = (lax.dot_general(wrh_ref[...], h_hi, nt, preferred_element_type=F32)
              + lax.dot_general(wrh_ref[...], h_lo, nt, preferred_element_type=F32)
              + lax.dot_general(wrl_ref[...], h_hi, nt, preferred_element_type=F32))
    s = jax.nn.sigmoid(logits)
    sc = s + bias_ref[...]
    tm = s.shape[1]
    neg = -jnp.inf

    sc3 = sc.reshape(N_EXPERT_GROUPS, GROUP_SIZE, tm)
    in_grp = lax.broadcasted_iota(I32, sc3.shape, 1).astype(F32)
    m1, i1 = _first_index_of_max(sc3, in_grp, 1)
    m2 = jnp.max(jnp.where(in_grp == i1, neg, sc3), axis=1, keepdims=True)
    gs = (m1 + m2).reshape(N_EXPERT_GROUPS, tm)

    gidx = lax.broadcasted_iota(I32, gs.shape, 0).astype(F32)
    gsel = jnp.zeros(gs.shape, F32)
    for _ in range(TOPK_GROUPS):
        _, gi = _first_index_of_max(gs, gidx, 0)
        hit = gidx == gi
        gsel = jnp.where(hit, 1.0, gsel)
        gs = jnp.where(hit, neg, gs)
    emask = jnp.broadcast_to(gsel[:, None, :], sc3.shape).reshape(N_EXPERTS, tm)

    cand = jnp.where(emask > 0.0, sc, neg)
    eidx = lax.broadcasted_iota(I32, cand.shape, 0).astype(F32)
    onehot = jnp.zeros(cand.shape, F32)
    hits, idxs, ws = [], [], []
    for _ in range(TOP_K):
        _, ei = _first_index_of_max(cand, eidx, 0)
        hit = eidx == ei
        hits.append(hit)
        idxs.append(ei)
        ws.append(jnp.sum(jnp.where(hit, s, 0.0), axis=0, keepdims=True))
        cand = jnp.where(hit, neg, cand)
        onehot = jnp.where(hit, 1.0, onehot)
    w = jnp.concatenate(ws, axis=0)
    wts_ref[...] = w / jnp.sum(w, axis=0, keepdims=True) * ROUTED_SCALE
    idx_ref[...] = jnp.concatenate(idxs, axis=0).astype(I32)

    rank = run_scr[...] + jnp.dot(onehot.astype(BF16), tri_ref[...], preferred_element_type=F32)
    pos = [jnp.sum(jnp.where(hit, rank, 0.0), axis=0, keepdims=True) for hit in hits]
    pos_ref[...] = jnp.concatenate(pos, axis=0).astype(I32)
    run_scr[...] = run_scr[...] + jnp.sum(onehot, axis=1, keepdims=True)
    cnt_ref[...] = run_scr[...].astype(I32)


def _route(hn_p, hn_s, wr_hi, wr_lo, bias_col, tri, tm):
    (tp, d), ts = hn_p.shape, hn_s.shape[0]
    t = tp + ts
    npt, nst = tp // tm, ts // tm
    const = lambda shape: pl.BlockSpec(shape, lambda i: (0, 0))
    tok = pl.BlockSpec((TOP_K, tm), lambda i: (0, i))
    return pl.pallas_call(
        functools.partial(_route_kernel, n_prompt_tiles=npt),
        out_shape=(jax.ShapeDtypeStruct((TOP_K, t), I32), jax.ShapeDtypeStruct((TOP_K, t), F32),
                   jax.ShapeDtypeStruct((TOP_K, t), I32), jax.ShapeDtypeStruct((N_EXPERTS, 1), I32)),
        grid=(npt + nst,),
        in_specs=[pl.BlockSpec((tm, d), lambda i: (jnp.minimum(i, npt - 1), 0)),
                  pl.BlockSpec((tm, d), lambda i: (jnp.clip(i - npt, 0, nst - 1), 0)),
                  const((N_EXPERTS, d)), const((N_EXPERTS, d)), const((N_EXPERTS, 1)), const((tm, tm))],
        out_specs=(tok, tok, tok, const((N_EXPERTS, 1))),
        scratch_shapes=[pltpu.VMEM((N_EXPERTS, 1), F32)],
        compiler_params=_cparams(("arbitrary",)),
        name="route",
    )(hn_p, hn_s, wr_hi, wr_lo, bias_col, tri)


HI_HALF = -65536


def _pack_bf16_pairs(x):
    w = x.shape[1] // 2
    bits = lambda a: lax.bitcast_convert_type(a.astype(BF16).astype(F32), I32)
    return lax.shift_right_logical(bits(x[:, :w]), 16) | (bits(x[:, w:]) & HI_HALF)


def _unpack_bf16_pairs(u):
    lo = lax.bitcast_convert_type(lax.shift_left(u, 16), F32)
    hi = lax.bitcast_convert_type(u & HI_HALF, F32)
    return jnp.concatenate([lo, hi], axis=1)


def _wait_rows(src, dst, sem, n):
    pltpu.make_async_copy(src.at[pl.ds(0, n)], dst.at[pl.ds(0, n)], sem).wait()


def _dispatch_kernel(lo_ref, n_ref, nu_ref, dest_ref, hp_ref, hs_ref, xs_hbm, rowbuf, zbuf, sem, zsem, *, tm,
                     n_prompt_tiles, rows):
    i = pl.program_id(0)
    n = pl.num_programs(0)
    par = i % 2
    n_blocks = xs_hbm.shape[0] // rows
    base = pl.multiple_of(par * tm, tm)

    @pl.when(i < n_prompt_tiles)
    def _():
        rowbuf[pl.ds(base, tm)] = hp_ref[...]

    @pl.when(i >= n_prompt_tiles)
    def _():
        rowbuf[pl.ds(base, tm)] = hs_ref[...]

    def issue(t, carry):
        for k in range(TOP_K):
            slot = dest_ref[0, 0, k * tm + t]
            pltpu.make_async_copy(rowbuf.at[pl.ds(base + t, 1)], xs_hbm.at[pl.ds(slot, 1)],
                                  sem.at[par]).start(priority=k % 2)
        return carry

    lax.fori_loop(0, tm, issue, 0)

    def tail_copy(e):
        return pltpu.make_async_copy(zbuf.at[pl.ds(0, n_ref[e])], xs_hbm.at[pl.ds(lo_ref[e], n_ref[e])],
                                     zsem.at[0])

    def block_copy(b):
        return pltpu.make_async_copy(zbuf, xs_hbm.at[pl.ds(b * rows, rows)], zsem.at[0])

    def for_each_fill(act):
        def tail(e, carry):
            @pl.when(n_ref[e] > 0)
            def _():
                act(tail_copy(e))
            return carry

        def block(b, carry):
            act(block_copy(b))
            return carry

        lax.fori_loop(0, N_EXPERTS, tail, 0)
        lax.fori_loop(nu_ref[0], n_blocks, block, 0)

    @pl.when(i == 0)
    def _():
        zbuf[...] = jnp.zeros(zbuf.shape, zbuf.dtype)
        for_each_fill(lambda c: c.start())

    def wait_tile(p):
        for _ in range(TOP_K):
            _wait_rows(rowbuf, xs_hbm, sem.at[p], tm)

    @pl.when(i >= 1)
    def _():
        wait_tile(1 - par)

    @pl.when(i == n - 1)
    def _():
        wait_tile(par)
        for_each_fill(lambda c: c.wait())


def _dispatch(pad_lo, pad_n, n_used, dest_tiles, hnp_p, hnp_s, n_slots, rows, tm):
    n_tiles = dest_tiles.shape[0]
    w = hnp_p.shape[2]
    npt, nst = hnp_p.shape[0] // tm, hnp_s.shape[0] // tm
    grid_spec = pltpu.PrefetchScalarGridSpec(
        num_scalar_prefetch=3,
        grid=(n_tiles,),
        in_specs=[pl.BlockSpec((1, 1, TOP_K * tm), lambda i, lo, n, nu: (i, 0, 0), memory_space=pltpu.SMEM),
                  pl.BlockSpec((tm, 1, w), lambda i, lo, n, nu: (jnp.minimum(i, npt - 1), 0, 0)),
                  pl.BlockSpec((tm, 1, w), lambda i, lo, n, nu: (jnp.clip(i - npt, 0, nst - 1), 0, 0))],
        out_specs=pl.BlockSpec(memory_space=pl.ANY),
        scratch_shapes=[pltpu.VMEM((2 * tm, 1, w), I32), pltpu.VMEM((rows, 1, w), I32),
                        pltpu.SemaphoreType.DMA((2,)), pltpu.SemaphoreType.DMA((1,))],
    )
    return pl.pallas_call(
        functools.partial(_dispatch_kernel, tm=tm, n_prompt_tiles=hnp_p.shape[0] // tm, rows=rows),
        out_shape=jax.ShapeDtypeStruct((n_slots, 1, w), I32),
        grid_spec=grid_spec,
        compiler_params=_cparams(("arbitrary",)),
        name="dispatch",
    )(pad_lo, pad_n, n_used, dest_tiles, hnp_p, hnp_s)


def _dot_by_columns(x, w_ref, tn):
    cols = [jnp.dot(x, w_ref[:, c:c + tn].astype(BF16), preferred_element_type=F32)
            for c in range(0, w_ref.shape[1], tn)]
    return jnp.concatenate(cols, axis=1)


def _expert_kernel(be_ref, nu_ref, xs_ref, wg_ref, wu_ref, wd_ref, ys_ref, x2d):
    b = pl.program_id(0)

    @pl.when(b < nu_ref[0])
    def _():
        x2d[...] = xs_ref[...].reshape(x2d.shape)
        x = _unpack_bf16_pairs(x2d[...]).astype(BF16)
        g = _dot_by_columns(x, wg_ref, MXU_COLS)
        u = _dot_by_columns(x, wu_ref, MXU_COLS)
        hmid = (_silu(g) * u).astype(BF16)
        y = _dot_by_columns(hmid, wd_ref, MXU_COLS)
        ys_ref[...] = _pack_bf16_pairs(y).reshape(ys_ref.shape)


def _experts(block_e, n_used, xs, w_gate, w_up, w_down, layer, rows):
    n_slots, _, w = xs.shape
    nblk = n_slots // rows
    d = 2 * w
    used = lambda b, nu: jnp.minimum(b, nu[0] - 1)
    wspec = lambda shape: pl.BlockSpec((None, None) + shape, lambda b, be, nu: (layer, be[used(b, nu)], 0, 0))
    slot_spec = pl.BlockSpec((rows, 1, w), lambda b, be, nu: (used(b, nu), 0, 0))
    grid_spec = pltpu.PrefetchScalarGridSpec(
        num_scalar_prefetch=2,
        grid=(nblk,),
        in_specs=[slot_spec, wspec((d, EXPERT_FF)), wspec((d, EXPERT_FF)), wspec((EXPERT_FF, d))],
        out_specs=slot_spec,
        scratch_shapes=[pltpu.VMEM((rows, w), I32)],
    )
    return pl.pallas_call(
        _expert_kernel,
        out_shape=jax.ShapeDtypeStruct((n_slots, 1, w), I32),
        grid_spec=grid_spec,
        input_output_aliases={2: 0},
        compiler_params=_cparams(("arbitrary",)),
        name="experts",
    )(block_e, n_used, xs, w_gate, w_up, w_down)


def _slot_plan(idx_t, pos_t, counts, rows, n_blocks):
    nblk_e = (counts + rows - 1) // rows
    blk_end = jnp.cumsum(nblk_e)
    blk_start = blk_end - nblk_e
    is_e = idx_t[:, :, None] == jnp.arange(N_EXPERTS, dtype=I32)[None, None, :]
    dest = jnp.sum(jnp.where(is_e, (blk_start * rows)[None, None, :], 0), axis=-1) + pos_t
    blocks = jnp.arange(n_blocks, dtype=I32)
    block_e = jnp.minimum(jnp.searchsorted(blk_end, blocks, side="right"), N_EXPERTS - 1).astype(I32)
    pad_lo = (blk_start * rows + counts).astype(I32)
    pad_n = (nblk_e * rows - counts).astype(I32)
    return dest.astype(I32), block_e, blk_end[-1:].astype(I32), pad_lo, pad_n


def _dest_tiles(dest, tm):
    t = dest.shape[1]
    return dest.reshape(TOP_K, t // tm, tm).transpose(1, 0, 2).reshape(t // tm, 1, TOP_K * tm)


def _combine_kernel(dest_ref, ndest_ref, wt_ref, hn_ref, x1_ref, gt_ref, wg_ref, wu_ref, wd_ref, ys_hbm, o_ref,
                    ybuf, y2d, sem, *, per_row, tm):
    i = pl.program_id(0)
    n = pl.num_programs(0)
    par = i % 2
    per_tile = TOP_K * tm

    def gather(tbl_ref, tile_slot):
        def body(t, carry):
            for k in range(TOP_K):
                slot = tbl_ref[0, 0, k * tm + t]
                pltpu.make_async_copy(ys_hbm.at[pl.ds(slot, 1)],
                                      ybuf.at[pl.ds(tile_slot * per_tile + k * tm + t, 1)],
                                      sem.at[tile_slot]).start(priority=k % 2)
            return carry
        lax.fori_loop(0, tm, body, 0)

    @pl.when(i == 0)
    def _():
        gather(dest_ref, 0)

    @pl.when(i + 1 < n)
    def _():
        gather(ndest_ref, 1 - par)

    h = hn_ref[...].astype(BF16)
    g = jnp.dot(h, wg_ref[...], preferred_element_type=F32)
    u = jnp.dot(h, wu_ref[...], preferred_element_type=F32)
    moe = jnp.dot((_silu(g) * u).astype(BF16), wd_ref[...], preferred_element_type=F32)

    _wait_rows(ys_hbm, ybuf.at[pl.ds(par * per_tile, per_tile)], sem.at[par], per_tile)
    wt = wt_ref[...]
    for k in range(TOP_K):
        y2d[...] = ybuf[pl.ds(pl.multiple_of(par * per_tile + k * tm, tm), tm)].reshape(y2d.shape)
        moe = moe + wt[:, k:k + 1] * _unpack_bf16_pairs(y2d[...])
    o_ref[...] = x1_ref[...] + _mod_rows(gt_ref, per_row) * moe


def _combine(ys, dest_tiles, wts, hn, x1, mod, mod_block_rows, mod_row_block, per_row, ws_g, ws_u, ws_d, tm):
    t, d = hn.shape
    w = ys.shape[2]
    n_tiles = t // tm
    row = lambda i: (i, 0)
    const = lambda shape: pl.BlockSpec(shape, lambda i: (0, 0))
    gt_spec = pl.BlockSpec((mod_block_rows, d), (lambda i: (i, 5)) if per_row else (lambda i: (mod_row_block, 5)))
    tile = lambda f: pl.BlockSpec((1, 1, TOP_K * tm), lambda i: (f(i), 0, 0), memory_space=pltpu.SMEM)
    return pl.pallas_call(
        functools.partial(_combine_kernel, per_row=per_row, tm=tm),
        out_shape=jax.ShapeDtypeStruct((t, d), F32),
        grid=(n_tiles,),
        in_specs=[tile(lambda i: i), tile(lambda i: jnp.minimum(i + 1, n_tiles - 1)),
                  pl.BlockSpec((tm, TOP_K), row), pl.BlockSpec((tm, d), row), pl.BlockSpec((tm, d), row),
                  gt_spec, const((d, EXPERT_FF)), const((d, EXPERT_FF)), const((EXPERT_FF, d)),
                  pl.BlockSpec(memory_space=pl.ANY)],
        out_specs=pl.BlockSpec((tm, d), row),
        scratch_shapes=[pltpu.VMEM((2 * TOP_K * tm, 1, w), I32), pltpu.VMEM((tm, w), I32),
                        pltpu.SemaphoreType.DMA((2,))],
        compiler_params=_cparams(("arbitrary",)),
        name="combine",
    )(dest_tiles, dest_tiles, wts, hn, x1, mod, ws_g, ws_u, ws_d, ys)


def _rope_tables(pos):
    half = HEAD_DIM_QK // 2
    inv = ROPE_THETA ** (-jnp.arange(half, dtype=F32) * 2.0 / HEAD_DIM_QK)
    ang = pos.astype(F32)[:, None] * inv[None, :]
    cos, sin = jnp.cos(ang), jnp.sin(ang)
    return jnp.tile(jnp.concatenate([cos, cos], axis=1), (1, 2)), jnp.tile(jnp.concatenate([-sin, sin], axis=1), (1, 2))


def _n_expert_blocks(n_assign, rows):
    return -(-(n_assign + N_EXPERTS * (rows - 1)) // rows)


def kernel(x_prompt, x_sample, c_prompt, c_sample, cache_k, cache_v, state_conv, page_table, w_ada, b_ada, g_norm1, g_norm2, w_in, g_q, g_k, lambda_q1, lambda_k1, lambda_q2, lambda_k2, g_sub, w_dw, b_dw, ln_g, ln_b, w_out, w_router, router_bias, w_e_gate, w_e_up, w_e_down, w_s_gate, w_s_up, w_s_down):
    layer = 0
    lam_init = 0.8 - 0.6 * math.exp(-0.3 * layer)
    d = D_MODEL
    tp = x_prompt.shape[1]
    ns = x_sample.shape[0]
    t_all = tp + ns
    xp = x_prompt.reshape(tp, d)
    xs = x_sample.reshape(ns, d)

    row2 = lambda a: a[layer].reshape(1, -1)
    w_in_bf = w_in[layer].astype(BF16)
    wo_a = w_out[layer, :ATTN_WIDTH].astype(BF16)
    wo_c = w_out[layer, ATTN_WIDTH:].astype(BF16)
    ws_g, ws_u, ws_d = (w[layer].astype(BF16) for w in (w_s_gate, w_s_up, w_s_down))
    wr_t = w_router[layer].T
    wr_hi = wr_t.astype(BF16)
    wr_lo = (wr_t - wr_hi.astype(F32)).astype(BF16)
    gq = jnp.tile(row2(g_q), (1, 2))
    gk = jnp.tile(row2(g_k), (1, 2))
    lane = jnp.arange(LANES)
    bd = (lane[:, None] // HEAD_DIM_QK == lane[None, :] // HEAD_DIM_QK).astype(BF16)
    lams = [row2(a) for a in (lambda_q1, lambda_k1, lambda_q2, lambda_k2)]

    c_all = jnp.concatenate([c_sample, c_prompt, jnp.zeros((SUBLANES - 1, d), F32)], axis=0)
    mod = _ada(c_all, w_ada[layer], row2(b_ada))
    p_mod = dict(mod_block_rows=SUBLANES, mod_row_block=ns // SUBLANES, per_row=False)
    s_mod = dict(mod_block_rows=ns, mod_row_block=0, per_row=True)

    cos_p, sin_p = _rope_tables(jnp.arange(tp, dtype=I32))
    cos_s, sin_s = _rope_tables(jnp.full((ns,), PAST_LEN, I32))

    q_p, k_p, kb_p, v_p, vb_p, glu_p = _inproj(xp, mod, g1=row2(g_norm1), w_in_bf=w_in_bf, gq=gq, gk=gk,
                                               cos=cos_p, sin=sin_p, bd=bd, tm=512, **p_mod)
    cy_p = _conv_prompt(glu_p, w_dw[layer], row2(b_dw), row2(ln_g), row2(ln_b))
    on_p = _flash_prompt(q_p, kb_p, vb_p, *lams, row2(g_sub), lam_init)

    q_s, k_s, _, v_s, _, glu_s = _inproj(xs, mod, g1=row2(g_norm1), w_in_bf=w_in_bf, gq=gq, gk=gk,
                                         cos=cos_s, sin=sin_s, bd=bd, tm=ns, **s_mod)
    cy_s = _conv_sample(state_conv[layer], glu_s, w_dw[layer], row2(b_dw), row2(ln_g), row2(ln_b))
    n_phys = cache_k.shape[1]
    ck_pages = jnp.transpose(cache_k, (0, 1, 3, 4, 5, 2)).reshape(-1, ATTN_WIDTH, PAGE_SIZE)
    cv_pages = cache_v.reshape(-1, PAGE_SIZE * N_HEADS, HEAD_DIM_V)
    to_cols = lambda a: jnp.pad(jnp.swapaxes(a.reshape(ns, 2 * N_HEADS, HEAD_DIM_QK), 1, 2),
                                ((0, 0), (0, 0), (0, LANES - 2 * N_HEADS)))
    qt_s = to_cols(q_s[0].astype(F32) + q_s[1].astype(F32))
    on_s = _attn_sample(page_table + layer * n_phys, qt_s, to_cols(k_s), v_s.reshape(ns, N_HEADS, HEAD_DIM_V),
                        *lams, row2(g_sub), ck_pages, cv_pages, lam_init).reshape(ns, ATTN_WIDTH)

    x1_p, hn_p, hnp_p = _outproj(on_p, cy_p, xp, mod, g2=row2(g_norm2), wo_a=wo_a, wo_c=wo_c, tm=256, **p_mod)
    x1_s, hn_s, hnp_s = _outproj(on_s, cy_s, xs, mod, g2=row2(g_norm2), wo_a=wo_a, wo_c=wo_c, tm=ns, **s_mod)

    tm_r = ns
    tri = (jnp.arange(tm_r)[:, None] < jnp.arange(tm_r)[None, :]).astype(BF16)
    idx_t, wts_t, pos_t, counts = _route(hn_p, hn_s, wr_hi, wr_lo, router_bias[layer].reshape(-1, 1), tri, tm_r)
    n_blocks = _n_expert_blocks(TOP_K * t_all, EXPERT_ROWS)
    dest, block_e, n_used, pad_lo, pad_n = _slot_plan(idx_t, pos_t, counts.reshape(-1), EXPERT_ROWS, n_blocks)

    slots_x = _dispatch(pad_lo, pad_n, n_used, _dest_tiles(dest, tm_r), hnp_p, hnp_s, n_blocks * EXPERT_ROWS,
                        EXPERT_ROWS, tm_r)
    slots_y = _experts(block_e, n_used, slots_x, w_e_gate, w_e_up, w_e_down, layer, EXPERT_ROWS)
    tm_c = 256
    y_p = _combine(slots_y, _dest_tiles(dest[:, :tp], tm_c), wts_t[:, :tp].T, hn_p, x1_p, mod,
                   ws_g=ws_g, ws_u=ws_u, ws_d=ws_d, tm=tm_c, **p_mod)
    y_s = _combine(slots_y, _dest_tiles(dest[:, tp:], ns), wts_t[:, tp:].T, hn_s, x1_s, mod,
                   ws_g=ws_g, ws_u=ws_u, ws_d=ws_d, tm=ns, **s_mod)

    conv_p = glu_p[tp - CONV_STATE:]
    conv_s = jnp.concatenate([state_conv[layer][:, 1:], glu_s[:, None, :]], axis=1)
    return (y_p.reshape(1, tp, d), y_s.reshape(ns, 1, d),
            k_p.reshape(1, 1, tp, N_HEADS, 2, HEAD_DIM_QK), v_p.reshape(1, 1, tp, N_HEADS, HEAD_DIM_V),
            conv_p.reshape(1, 1, CONV_STATE, CONV_CH),
            k_s.reshape(1, ns, 1, N_HEADS, 2, HEAD_DIM_QK), v_s.reshape(1, ns, 1, N_HEADS, HEAD_DIM_V),
            conv_s.reshape(1, ns, CONV_STATE, CONV_CH))
```

```python
import functools
import math

import jax
import jax.numpy as jnp
from jax import lax
from jax.experimental import pallas as pl
from jax.experimental.pallas import tpu as pltpu

F32 = jnp.float32
BF16 = jnp.bfloat16
I32 = jnp.int32

D_MODEL = 2048
ATTN_WIDTH = 1024
CONV_CH = 1024
N_HEADS = 8
HEAD_DIM_V = 128
HEAD_DIM_QK = 64
CONV_LEN = 31
CONV_STATE = CONV_LEN - 1
N_EXPERTS = 256
TOP_K = 8
N_EXPERT_GROUPS = 8
GROUP_SIZE = N_EXPERTS // N_EXPERT_GROUPS
TOPK_GROUPS = 4
EXPERT_FF = 512
ROUTED_SCALE = 2.5
ROPE_THETA = 10000.0
NORM_EPS = 1e-6
PAST_LEN = 2048
PAGE_SIZE = 128
IN_COLS = 3 * ATTN_WIDTH + 2 * CONV_CH
LANES = 128
SUBLANES = 8
COL_TILE = 1024
MXU_COLS = 256
CONV_HALO = 32
EXPERT_ROWS = 320
VMEM_LIMIT = 56 * 1024 * 1024
Q_SCALE = HEAD_DIM_QK ** -0.5 * math.log2(math.e)


def _cparams(sem, vmem=VMEM_LIMIT):
    return pltpu.CompilerParams(dimension_semantics=sem, vmem_limit_bytes=vmem)


def _rms(x, eps=NORM_EPS):
    return x * lax.rsqrt(jnp.mean(x * x, axis=-1, keepdims=True) + eps)


def _silu(x):
    return x * jax.nn.sigmoid(x)


def _mod_rows(ref, per_row):
    return ref[...] if per_row else ref[0:1, :]


def _diff_lambda(lq1_ref, lk1_ref, lq2_ref, lk2_ref, lam_init):
    a = jnp.sum(lq1_ref[...] * lk1_ref[...], axis=-1, keepdims=True)
    b = jnp.sum(lq2_ref[...] * lk2_ref[...], axis=-1, keepdims=True)
    return jnp.exp(a) - jnp.exp(b) + lam_init


def _ada_kernel(c_ref, w_ref, b_ref, o_ref):
    s = _silu(c_ref[...]).astype(BF16)
    o_ref[...] = jnp.dot(s, w_ref[...].astype(BF16), preferred_element_type=F32) + b_ref[...]


def _ada(c_all, w_ada, b_ada):
    rows, d = c_all.shape
    n = w_ada.shape[1]
    tn = 1024
    return pl.pallas_call(
        _ada_kernel,
        out_shape=jax.ShapeDtypeStruct((rows, n), F32),
        grid=(n // tn,),
        in_specs=[pl.BlockSpec((rows, d), lambda j: (0, 0)),
                  pl.BlockSpec((d, tn), lambda j: (0, j)),
                  pl.BlockSpec((1, tn), lambda j: (0, j))],
        out_specs=pl.BlockSpec((rows, tn), lambda j: (0, j)),
        compiler_params=_cparams(("arbitrary",)),
        name="adaln",
    )(c_all, w_ada, b_ada)


def _qk_norm_rope(u, g, cos, sin, bd):
    lane = lax.broadcasted_iota(I32, (u.shape[0], LANES), 1)
    first_half = (lane & 32) == 0
    outs = []
    for c in range(u.shape[1] // LANES):
        x = u[:, c * LANES:(c + 1) * LANES]
        ss = jnp.dot((x * x).astype(BF16), bd, preferred_element_type=F32)
        xn = x * lax.rsqrt(ss * (1.0 / HEAD_DIM_QK) + NORM_EPS) * g
        rot = jnp.where(first_half, pltpu.roll(xn, 96, 1), pltpu.roll(xn, 32, 1))
        outs.append(xn * cos + rot * sin)
    return jnp.concatenate(outs, axis=1)


def _inproj_kernel(x_ref, sh_ref, sc_ref, g1_ref, w_ref, gq_ref, gk_ref, cos_ref, sin_ref, bd_ref,
                   q_ref, k_ref, kb_ref, v_ref, vb_ref, glu_ref, xn_scr, val_scr, *, per_row):
    j = pl.program_id(1)

    @pl.when(j == 0)
    def _():
        y = _rms(x_ref[...]) * g1_ref[...]
        xn_scr[...] = (y * (1.0 + _mod_rows(sc_ref, per_row)) + _mod_rows(sh_ref, per_row)).astype(BF16)

    u = jnp.dot(xn_scr[...], w_ref[...], preferred_element_type=F32)

    @pl.when(j == 0)
    def _():
        q = _qk_norm_rope(u, gq_ref[...], cos_ref[...], sin_ref[...], bd_ref[...]) * Q_SCALE
        lane = lax.broadcasted_iota(I32, q.shape, 1)
        map0 = (lane & HEAD_DIM_QK) == 0
        q_ref[0] = jnp.where(map0, q, 0.0).astype(BF16)
        q_ref[1] = jnp.where(map0, 0.0, q).astype(BF16)

    @pl.when(j == 1)
    def _():
        k = _qk_norm_rope(u, gk_ref[...], cos_ref[...], sin_ref[...], bd_ref[...])
        k_ref[...] = k
        kb_ref[...] = k.astype(BF16)

    @pl.when(j == 2)
    def _():
        v_ref[...] = u
        vb_ref[...] = u.astype(BF16)

    @pl.when(j == 3)
    def _():
        val_scr[...] = u

    @pl.when(j == 4)
    def _():
        glu_ref[...] = val_scr[...] * jax.nn.sigmoid(u)


def _inproj(x, mod, mod_block_rows, mod_row_block, per_row, g1, w_in_bf, gq, gk, cos, sin, bd, tm):
    t, d = x.shape
    nj = IN_COLS // COL_TILE
    row = lambda i, j: (i, 0)
    mod_spec = lambda chunk: pl.BlockSpec(
        (mod_block_rows, d), (lambda i, j: (i, chunk)) if per_row else (lambda i, j: (mod_row_block, chunk)))
    const = lambda shape: pl.BlockSpec(shape, lambda i, j: (0, 0))
    return pl.pallas_call(
        functools.partial(_inproj_kernel, per_row=per_row),
        out_shape=(jax.ShapeDtypeStruct((2, t, COL_TILE), BF16),
                   jax.ShapeDtypeStruct((t, COL_TILE), F32), jax.ShapeDtypeStruct((t, COL_TILE), BF16),
                   jax.ShapeDtypeStruct((t, COL_TILE), F32), jax.ShapeDtypeStruct((t, COL_TILE), BF16),
                   jax.ShapeDtypeStruct((t, COL_TILE), F32)),
        grid=(t // tm, nj),
        in_specs=[pl.BlockSpec((tm, d), row), mod_spec(0), mod_spec(1), const((1, d)),
                  pl.BlockSpec((d, COL_TILE), lambda i, j: (0, j)),
                  const((1, LANES)), const((1, LANES)),
                  pl.BlockSpec((tm, LANES), row), pl.BlockSpec((tm, LANES), row), const((LANES, LANES))],
        out_specs=(pl.BlockSpec((2, tm, COL_TILE), lambda i, j: (0, i, 0)),
                   pl.BlockSpec((tm, COL_TILE), row), pl.BlockSpec((tm, COL_TILE), row),
                   pl.BlockSpec((tm, COL_TILE), row), pl.BlockSpec((tm, COL_TILE), row),
                   pl.BlockSpec((tm, COL_TILE), row)),
        scratch_shapes=[pltpu.VMEM((tm, d), BF16), pltpu.VMEM((tm, COL_TILE), F32)],
        compiler_params=_cparams(("arbitrary", "arbitrary")),
        name="inproj",
    )(x, mod, mod, g1, w_in_bf, gq, gk, cos, sin, bd)


def _layernorm_swish(y, g, b):
    mu = jnp.mean(y, axis=-1, keepdims=True)
    yc = y - mu
    var = jnp.mean(yc * yc, axis=-1, keepdims=True)
    yn = yc * lax.rsqrt(var + NORM_EPS) * g + b
    return yn * jax.nn.sigmoid(yn)


def _conv_kernel(cur_ref, prev_ref, w_ref, b_ref, lg_ref, lb_ref, y_ref, xs_scr, sh_scr, *, tm, rc):
    i = pl.program_id(0)
    xs_scr[0:CONV_HALO, :] = jnp.where(i == 0, 0.0, prev_ref[...])
    xs_scr[CONV_HALO:, :] = cur_ref[...]
    n_sh = tm + CONV_HALO - SUBLANES
    for b in range(1, SUBLANES):
        sh_scr[b - 1] = xs_scr[b:b + n_sh, :]

    first = CONV_HALO - CONV_STATE

    def chunk(r, carry):
        r0 = pl.multiple_of(r * rc, rc)
        acc = jnp.zeros((rc, CONV_CH), F32)
        for j in range(CONV_LEN):
            o = first + j
            b, a = o % SUBLANES, (o // SUBLANES) * SUBLANES
            src = xs_scr if b == 0 else sh_scr.at[b - 1]
            acc = acc + w_ref[j:j + 1, :] * src[pl.ds(r0 + a, rc), :]
        y = _layernorm_swish(acc + b_ref[...], lg_ref[...], lb_ref[...])
        y_ref[pl.ds(r0, rc), :] = y.astype(y_ref.dtype)
        return carry

    lax.fori_loop(0, tm // rc, chunk, 0)


def _conv_prompt(glu, w_dw, b_dw, ln_g, ln_b, tm=256, rc=32):
    t, c = glu.shape
    per = tm // CONV_HALO
    const = lambda shape: pl.BlockSpec(shape, lambda i: (0, 0))
    return pl.pallas_call(
        functools.partial(_conv_kernel, tm=tm, rc=rc),
        out_shape=jax.ShapeDtypeStruct((t, c), BF16),
        grid=(t // tm,),
        in_specs=[pl.BlockSpec((tm, c), lambda i: (i, 0)),
                  pl.BlockSpec((CONV_HALO, c), lambda i: (jnp.maximum(i * per - 1, 0), 0)),
                  const((CONV_LEN, c)), const((1, c)), const((1, c)), const((1, c))],
        out_specs=pl.BlockSpec((tm, c), lambda i: (i, 0)),
        scratch_shapes=[pltpu.VMEM((tm + CONV_HALO, c), F32),
                        pltpu.VMEM((SUBLANES - 1, tm + CONV_HALO - SUBLANES, c), F32)],
        compiler_params=_cparams(("arbitrary",)),
        name="conv_prompt",
    )(glu, glu, w_dw, b_dw, ln_g, ln_b)


def _conv_sample_kernel(st_ref, glu_ref, w_ref, b_ref, lg_ref, lb_ref, y_ref):
    w = w_ref[...]
    acc = jnp.sum(st_ref[...] * w[None, 0:CONV_STATE, :], axis=1) + glu_ref[...] * w[CONV_STATE:CONV_LEN, :]
    y_ref[...] = _layernorm_swish(acc + b_ref[...], lg_ref[...], lb_ref[...]).astype(y_ref.dtype)


def _conv_sample(state, glu, w_dw, b_dw, ln_g, ln_b, sb=32):
    b, s, c = state.shape
    const = lambda shape: pl.BlockSpec(shape, lambda i: (0, 0))
    return pl.pallas_call(
        _conv_sample_kernel,
        out_shape=jax.ShapeDtypeStruct((b, c), BF16),
        grid=(b // sb,),
        in_specs=[pl.BlockSpec((sb, s, c), lambda i: (i, 0, 0)), pl.BlockSpec((sb, c), lambda i: (i, 0)),
                  const((CONV_LEN, c)), const((1, c)), const((1, c)), const((1, c))],
        out_specs=pl.BlockSpec((sb, c), lambda i: (i, 0)),
        compiler_params=_cparams(("arbitrary",)),
        name="conv_sample",
    )(state, glu, w_dw, b_dw, ln_g, ln_b)


def _flash_kernel(q_ref, k_ref, v_ref, lq1_ref, lk1_ref, lq2_ref, lk2_ref, gs_ref, o_ref,
                  vx_scr, m_scr, acc_scr, sa_scr, sb_scr, *, tq, tk, lam_init):
    i = pl.program_id(1)
    dv = HEAD_DIM_V

    @pl.when(i == 0)
    def _():
        vx_scr[:, 0:dv] = v_ref[...]
        vx_scr[:, dv:2 * dv] = jnp.ones((vx_scr.shape[0], dv), BF16)

    q = q_ref[...].reshape(2 * tq, LANES)
    m_scr[...] = jnp.full(m_scr.shape, -jnp.inf, F32)
    acc_scr[...] = jnp.zeros(acc_scr.shape, F32)

    def scores(j, s_ref):
        k = k_ref[pl.ds(pl.multiple_of(j * tk, tk), tk), :]
        s_ref[...] = lax.dot_general(q, k, (((1,), (1,)), ((), ())), preferred_element_type=F32)

    def consume(j, s_ref, masked):
        k0 = pl.multiple_of(j * tk, tk)
        s = s_ref[...]
        if masked:
            row = lax.broadcasted_iota(I32, s.shape, 0)
            col = lax.broadcasted_iota(I32, s.shape, 1)
            qpos = i * tq + jnp.where(row >= tq, row - tq, row)
            s = jnp.where(k0 + col <= qpos, s, -jnp.inf)
        m_prev = m_scr[...]
        m_new = jnp.maximum(m_prev, jnp.max(s, axis=-1, keepdims=True))
        alpha = jnp.exp2(m_prev - m_new)
        p = jnp.exp2(s - m_new).astype(BF16)
        acc_scr[...] = alpha * acc_scr[...] + jnp.dot(p, vx_scr[pl.ds(k0, tk), :], preferred_element_type=F32)
        m_scr[...] = m_new

    n_full = (i * tq) // tk
    scores(0, sa_scr)

    def pair(t, carry):
        scores(2 * t + 1, sb_scr)
        consume(2 * t, sa_scr, False)
        scores(2 * t + 2, sa_scr)
        consume(2 * t + 1, sb_scr, False)
        return carry

    lax.fori_loop(0, n_full // 2, pair, 0)

    @pl.when(n_full % 2 == 1)
    def _():
        scores(n_full, sb_scr)
        consume(n_full - 1, sa_scr, False)
        consume(n_full, sb_scr, True)

    @pl.when(n_full % 2 == 0)
    def _():
        consume(n_full, sa_scr, True)

    acc = acc_scr[...]
    o = acc[:, 0:dv] / acc[:, dv:2 * dv]
    lam = _diff_lambda(lq1_ref, lk1_ref, lq2_ref, lk2_ref, lam_init)
    od = o[0:tq] - lam * o[tq:2 * tq]
    o_ref[...] = (_rms(od) * gs_ref[...] * (1.0 - lam_init)).astype(o_ref.dtype)


def _flash_prompt(q2, kb, vb, lq1, lk1, lq2, lk2, g_sub, lam_init, tq=512, tk=512):
    _, t, _ = q2.shape
    tk = min(tk, t)
    assert tk % tq == 0 and t % tk == 0
    const = lambda shape: pl.BlockSpec(shape, lambda h, i: (0, 0))
    return pl.pallas_call(
        functools.partial(_flash_kernel, tq=tq, tk=tk, lam_init=lam_init),
        out_shape=jax.ShapeDtypeStruct((t, ATTN_WIDTH), BF16),
        grid=(N_HEADS, t // tq),
        in_specs=[pl.BlockSpec((2, tq, LANES), lambda h, i: (0, i, h)),
                  pl.BlockSpec((t, LANES), lambda h, i: (0, h)),
                  pl.BlockSpec((t, LANES), lambda h, i: (0, h)),
                  const((1, HEAD_DIM_QK)), const((1, HEAD_DIM_QK)), const((1, HEAD_DIM_QK)),
                  const((1, HEAD_DIM_QK)), const((1, LANES))],
        out_specs=pl.BlockSpec((tq, LANES), lambda h, i: (i, h)),
        scratch_shapes=[pltpu.VMEM((t, 2 * HEAD_DIM_V), BF16), pltpu.VMEM((2 * tq, 1), F32),
                        pltpu.VMEM((2 * tq, 2 * HEAD_DIM_V), F32),
                        pltpu.VMEM((2 * tq, tk), F32), pltpu.VMEM((2 * tq, tk), F32)],
        compiler_params=_cparams(("arbitrary", "arbitrary")),
        name="flash_prompt",
    )(q2, kb, vb, lq1, lk1, lq2, lk2, g_sub)


def _paged_copies(pt_ref, ck_hbm, cv_hbm, kbuf, vbuf, sem, b, slot, n_pages):
    copies = []
    for p in range(n_pages):
        pg = pt_ref[b, p]
        copies.append(pltpu.make_async_copy(ck_hbm.at[pg], kbuf.at[slot, p], sem.at[0, slot]))
        copies.append(pltpu.make_async_copy(cv_hbm.at[pg], vbuf.at[slot, p], sem.at[1, slot]))
    return copies


def _sattn_kernel(pt_ref, qt_ref, kts_ref, vs_ref, lq1_ref, lk1_ref, lq2_ref, lk2_ref, gs_ref,
                  ck_hbm, cv_hbm, o_ref, kbuf, vbuf, qb_scr, s_scr, sem, *, n_pages, lam_init):
    b = pl.program_id(0)
    nb = pl.num_programs(0)
    slot = b % 2

    @pl.when(b == 0)
    def _():
        for c in _paged_copies(pt_ref, ck_hbm, cv_hbm, kbuf, vbuf, sem, 0, 0, n_pages):
            c.start()

    @pl.when(b + 1 < nb)
    def _():
        for c in _paged_copies(pt_ref, ck_hbm, cv_hbm, kbuf, vbuf, sem, b + 1, 1 - slot, n_pages):
            c.start()

    nr = 2 * N_HEADS
    dk = HEAD_DIM_QK
    qt = qt_ref[...]
    for r in range(nr):
        qb_scr[r] = jnp.broadcast_to(qt[:, r:r + 1], (dk, LANES))
    s_self_lanes = jnp.sum(qt * kts_ref[...], axis=0, keepdims=True)
    rr = lax.broadcasted_iota(I32, (nr, LANES), 0)
    ll = lax.broadcasted_iota(I32, (nr, LANES), 1)
    s_self = jnp.sum(jnp.where(ll == (rr % N_HEADS) * 2 + rr // N_HEADS, s_self_lanes, 0.0),
                     axis=-1, keepdims=True)

    for c in _paged_copies(pt_ref, ck_hbm, cv_hbm, kbuf, vbuf, sem, b, slot, n_pages):
        c.wait()

    for p in range(n_pages):
        for r in range(nr):
            h, c = divmod(r, 2)
            kt = kbuf[slot, p, pl.ds(r * dk, dk), :]
            row = c * N_HEADS + h
            s_scr[row:row + 1, p * PAGE_SIZE:(p + 1) * PAGE_SIZE] = jnp.sum(qb_scr[r] * kt, axis=0, keepdims=True)

    s = s_scr[...]
    m = jnp.maximum(jnp.max(s, axis=-1, keepdims=True), s_self)
    e = jnp.exp2(s - m)
    e_self = jnp.exp2(s_self - m)
    inv_l = 1.0 / (jnp.sum(e, axis=-1, keepdims=True) + e_self)
    lam = _diff_lambda(lq1_ref, lk1_ref, lq2_ref, lk2_ref, lam_init)
    w = inv_l[0:N_HEADS]
    w1 = lam * inv_l[N_HEADS:nr]
    pd = (e[0:N_HEADS] * w - e[N_HEADS:nr] * w1).astype(BF16)
    pd_self = e_self[0:N_HEADS] * w - e_self[N_HEADS:nr] * w1

    head_row = lax.broadcasted_iota(I32, (N_HEADS, HEAD_DIM_V), 0)
    o = pd_self * vs_ref[...]
    for h in range(N_HEADS):
        acc = jnp.zeros((N_HEADS, HEAD_DIM_V), F32)
        for p in range(n_pages):
            vh = vbuf[slot, p, pl.ds(h, PAGE_SIZE, stride=N_HEADS), :].astype(BF16)
            acc = acc + jnp.dot(pd[:, p * PAGE_SIZE:(p + 1) * PAGE_SIZE], vh, preferred_element_type=F32)
        o = o + jnp.where(head_row == h, acc, 0.0)
    o_ref[...] = _rms(o) * gs_ref[...] * (1.0 - lam_init)


def _attn_sample(page_table, qt, kts, vs, lq1, lk1, lq2, lk2, g_sub, ck_pages, cv_pages, lam_init):
    nb, n_pages = page_table.shape
    const2 = lambda shape: pl.BlockSpec(shape, lambda b, pt: (0, 0))
    per_seq = lambda shape: pl.BlockSpec((None,) + shape, lambda b, pt: (b, 0, 0))
    page_rows = ck_pages.shape[1]
    grid_spec = pltpu.PrefetchScalarGridSpec(
        num_scalar_prefetch=1,
        grid=(nb,),
        in_specs=[per_seq((HEAD_DIM_QK, LANES)), per_seq((HEAD_DIM_QK, LANES)), per_seq((N_HEADS, HEAD_DIM_V)),
                  const2((1, HEAD_DIM_QK)), const2((1, HEAD_DIM_QK)), const2((1, HEAD_DIM_QK)),
                  const2((1, HEAD_DIM_QK)), const2((1, HEAD_DIM_V)),
                  pl.BlockSpec(memory_space=pl.ANY), pl.BlockSpec(memory_space=pl.ANY)],
        out_specs=per_seq((N_HEADS, HEAD_DIM_V)),
        scratch_shapes=[pltpu.VMEM((2, n_pages, page_rows, LANES), F32),
                        pltpu.VMEM((2, n_pages, page_rows, LANES), F32),
                        pltpu.VMEM((2 * N_HEADS, HEAD_DIM_QK, LANES), F32),
                        pltpu.VMEM((2 * N_HEADS, n_pages * PAGE_SIZE), F32),
                        pltpu.SemaphoreType.DMA((2, 2))],
    )
    return pl.pallas_call(
        functools.partial(_sattn_kernel, n_pages=n_pages, lam_init=lam_init),
        out_shape=jax.ShapeDtypeStruct((nb, N_HEADS, HEAD_DIM_V), F32),
        grid_spec=grid_spec,
        compiler_params=_cparams(("arbitrary",)),
        name="attn_sample",
    )(page_table, qt, kts, vs, lq1, lk1, lq2, lk2, g_sub, ck_pages, cv_pages)


def _outproj_kernel(on_ref, cy_ref, x_ref, gt_ref, sh_ref, sc_ref, g2_ref, wa_ref, wc_ref, x1_ref, hn_ref, hnp_ref,
                    *, per_row):
    mix = (jnp.dot(on_ref[...].astype(BF16), wa_ref[...], preferred_element_type=F32)
           + jnp.dot(cy_ref[...], wc_ref[...], preferred_element_type=F32))
    x1 = x_ref[...] + _mod_rows(gt_ref, per_row) * mix
    x1_ref[...] = x1
    y = _rms(x1) * g2_ref[...]
    hn = y * (1.0 + _mod_rows(sc_ref, per_row)) + _mod_rows(sh_ref, per_row)
    hn_ref[...] = hn
    hnp_ref[...] = _pack_bf16_pairs(hn).reshape(hnp_ref.shape)


def _outproj(on, cy, x, mod, mod_block_rows, mod_row_block, per_row, g2, wo_a, wo_c, tm):
    t, d = x.shape
    row = lambda i: (i, 0)
    mod_spec = lambda chunk: pl.BlockSpec(
        (mod_block_rows, d), (lambda i: (i, chunk)) if per_row else (lambda i: (mod_row_block, chunk)))
    const = lambda shape: pl.BlockSpec(shape, lambda i: (0, 0))
    return pl.pallas_call(
        functools.partial(_outproj_kernel, per_row=per_row),
        out_shape=(jax.ShapeDtypeStruct((t, d), F32), jax.ShapeDtypeStruct((t, d), F32),
                   jax.ShapeDtypeStruct((t, 1, d // 2), I32)),
        grid=(t // tm,),
        in_specs=[pl.BlockSpec((tm, ATTN_WIDTH), row), pl.BlockSpec((tm, CONV_CH), row), pl.BlockSpec((tm, d), row),
                  mod_spec(2), mod_spec(3), mod_spec(4), const((1, d)),
                  const((ATTN_WIDTH, d)), const((CONV_CH, d))],
        out_specs=(pl.BlockSpec((tm, d), row), pl.BlockSpec((tm, d), row),
                   pl.BlockSpec((tm, 1, d // 2), lambda i: (i, 0, 0))),
        compiler_params=_cparams(("arbitrary",)),
        name="outproj",
    )(on, cy, x, mod, mod, mod, g2, wo_a, wo_c)


def _first_index_of_max(x, idx, axis):
    m = jnp.max(x, axis=axis, keepdims=True)
    return m, jnp.min(jnp.where(x == m, idx, float(N_EXPERTS)), axis=axis, keepdims=True)


def _route_kernel(hp_ref, hs_ref, wrh_ref, wrl_ref, bias_ref, tri_ref, idx_ref, wts_ref, pos_ref, cnt_ref, run_scr,
                  *, n_prompt_tiles):
    i = pl.program_id(0)

    @pl.when(i == 0)
    def _():
        run_scr[...] = jnp.zeros(run_scr.shape, F32)

    h = jnp.where(i < n_prompt_tiles, hp_ref[...], hs_ref[...])
    h_hi = h.astype(BF16)
    h_lo = (h - h_hi.astype(F32)).astype(BF16)
    nt = (((1,), (1,)), ((), ()))
    logits = (lax.dot_general(wrh_ref[...], h_hi, nt, preferred_element_type=F32)
              + lax.dot_general(wrh_ref[...], h_lo, nt, preferred_element_type=F32)
              + lax.dot_general(wrl_ref[...], h_hi, nt, preferred_element_type=F32))
    s = jax.nn.sigmoid(logits)
    sc = s + bias_ref[...]
    tm = s.shape[1]
    neg = -jnp.inf

    sc3 = sc.reshape(N_EXPERT_GROUPS, GROUP_SIZE, tm)
    in_grp = lax.broadcasted_iota(I32, sc3.shape, 1).astype(F32)
    m1, i1 = _first_index_of_max(sc3, in_grp, 1)
    m2 = jnp.max(jnp.where(in_grp == i1, neg, sc3), axis=1, keepdims=True)
    gs = (m1 + m2).reshape(N_EXPERT_GROUPS, tm)

    gidx = lax.broadcasted_iota(I32, gs.shape, 0).astype(F32)
    gsel = jnp.zeros(gs.shape, F32)
    for _ in range(TOPK_GROUPS):
        _, gi = _first_index_of_max(gs, gidx, 0)
        hit = gidx == gi
        gsel = jnp.where(hit, 1.0, gsel)
        gs = jnp.where(hit, neg, gs)
    emask = jnp.broadcast_to(gsel[:, None, :], sc3.shape).reshape(N_EXPERTS, tm)

    cand = jnp.where(emask > 0.0, sc, neg)
    eidx = lax.broadcasted_iota(I32, cand.shape, 0).astype(F32)
    onehot = jnp.zeros(cand.shape, F32)
    hits, idxs, ws = [], [], []
    for _ in range(TOP_K):
        _, ei = _first_index_of_max(cand, eidx, 0)
        hit = eidx == ei
        hits.append(hit)
        idxs.append(ei)
        ws.append(jnp.sum(jnp.where(hit, s, 0.0), axis=0, keepdims=True))
        cand = jnp.where(hit, neg, cand)
        onehot = jnp.where(hit, 1.0, onehot)
    w = jnp.concatenate(ws, axis=0)
    wts_ref[...] = w / jnp.sum(w, axis=0, keepdims=True) * ROUTED_SCALE
    idx_ref[...] = jnp.concatenate(idxs, axis=0).astype(I32)

    rank = run_scr[...] + jnp.dot(onehot.astype(BF16), tri_ref[...], preferred_element_type=F32)
    pos = [jnp.sum(jnp.where(hit, rank, 0.0), axis=0, keepdims=True) for hit in hits]
    pos_ref[...] = jnp.concatenate(pos, axis=0).astype(I32)
    run_scr[...] = run_scr[...] + jnp.sum(onehot, axis=1, keepdims=True)
    cnt_ref[...] = run_scr[...].astype(I32)


def _route(hn_p, hn_s, wr_hi, wr_lo, bias_col, tri, tm):
    (tp, d), ts = hn_p.shape, hn_s.shape[0]
    t = tp + ts
    npt, nst = tp // tm, ts // tm
    const = lambda shape: pl.BlockSpec(shape, lambda i: (0, 0))
    tok = pl.BlockSpec((TOP_K, tm), lambda i: (0, i))
    return pl.pallas_call(
        functools.partial(_route_kernel, n_prompt_tiles=npt),
        out_shape=(jax.ShapeDtypeStruct((TOP_K, t), I32), jax.ShapeDtypeStruct((TOP_K, t), F32),
                   jax.ShapeDtypeStruct((TOP_K, t), I32), jax.ShapeDtypeStruct((N_EXPERTS, 1), I32)),
        grid=(npt + nst,),
        in_specs=[pl.BlockSpec((tm, d), lambda i: (jnp.minimum(i, npt - 1), 0)),
                  pl.BlockSpec((tm, d), lambda i: (jnp.clip(i - npt, 0, nst - 1), 0)),
                  const((N_EXPERTS, d)), const((N_EXPERTS, d)), const((N_EXPERTS, 1)), const((tm, tm))],
        out_specs=(tok, tok, tok, const((N_EXPERTS, 1))),
        scratch_shapes=[pltpu.VMEM((N_EXPERTS, 1), F32)],
        compiler_params=_cparams(("arbitrary",)),
        name="route",
    )(hn_p, hn_s, wr_hi, wr_lo, bias_col, tri)


HI_HALF = -65536


def _pack_bf16_pairs(x):
    w = x.shape[1] // 2
    bits = lambda a: lax.bitcast_convert_type(a.astype(BF16).astype(F32), I32)
    return lax.shift_right_logical(bits(x[:, :w]), 16) | (bits(x[:, w:]) & HI_HALF)


def _unpack_bf16_pairs(u):
    lo = lax.bitcast_convert_type(lax.shift_left(u, 16), F32)
    hi = lax.bitcast_convert_type(u & HI_HALF, F32)
    return jnp.concatenate([lo, hi], axis=1)


def _wait_rows(src, dst, sem, n):
    pltpu.make_async_copy(src.at[pl.ds(0, n)], dst.at[pl.ds(0, n)], sem).wait()


def _dispatch_kernel(lo_ref, n_ref, nu_ref, dest_ref, hp_ref, hs_ref, xs_hbm, rowbuf, zbuf, sem, zsem, *, tm,
                     n_prompt_tiles, rows):
    i = pl.program_id(0)
    n = pl.num_programs(0)
    par = i % 2
    n_blocks = xs_hbm.shape[0] // rows
    base = pl.multiple_of(par * tm, tm)

    @pl.when(i < n_prompt_tiles)
    def _():
        rowbuf[pl.ds(base, tm)] = hp_ref[...]

    @pl.when(i >= n_prompt_tiles)
    def _():
        rowbuf[pl.ds(base, tm)] = hs_ref[...]

    def issue(t, carry):
        for k in range(TOP_K):
            slot = dest_ref[0, 0, k * tm + t]
            pltpu.make_async_copy(rowbuf.at[pl.ds(base + t, 1)], xs_hbm.at[pl.ds(slot, 1)],
                                  sem.at[par]).start(priority=k % 2)
        return carry

    lax.fori_loop(0, tm, issue, 0)

    def tail_copy(e):
        return pltpu.make_async_copy(zbuf.at[pl.ds(0, n_ref[e])], xs_hbm.at[pl.ds(lo_ref[e], n_ref[e])],
                                     zsem.at[0])

    def block_copy(b):
        return pltpu.make_async_copy(zbuf, xs_hbm.at[pl.ds(b * rows, rows)], zsem.at[0])

    def for_each_fill(act):
        def tail(e, carry):
            @pl.when(n_ref[e] > 0)
            def _():
                act(tail_copy(e))
            return carry

        def block(b, carry):
            act(block_copy(b))
            return carry

        lax.fori_loop(0, N_EXPERTS, tail, 0)
        lax.fori_loop(nu_ref[0], n_blocks, block, 0)

    @pl.when(i == 0)
    def _():
        zbuf[...] = jnp.zeros(zbuf.shape, zbuf.dtype)
        for_each_fill(lambda c: c.start())

    def wait_tile(p):
        for _ in range(TOP_K):
            _wait_rows(rowbuf, xs_hbm, sem.at[p], tm)

    @pl.when(i >= 1)
    def _():
        wait_tile(1 - par)

    @pl.when(i == n - 1)
    def _():
        wait_tile(par)
        for_each_fill(lambda c: c.wait())


def _dispatch(pad_lo, pad_n, n_used, dest_tiles, hnp_p, hnp_s, n_slots, rows, tm):
    n_tiles = dest_tiles.shape[0]
    w = hnp_p.shape[2]
    npt, nst = hnp_p.shape[0] // tm, hnp_s.shape[0] // tm
    grid_spec = pltpu.PrefetchScalarGridSpec(
        num_scalar_prefetch=3,
        grid=(n_tiles,),
        in_specs=[pl.BlockSpec((1, 1, TOP_K * tm), lambda i, lo, n, nu: (i, 0, 0), memory_space=pltpu.SMEM),
                  pl.BlockSpec((tm, 1, w), lambda i, lo, n, nu: (jnp.minimum(i, npt - 1), 0, 0)),
                  pl.BlockSpec((tm, 1, w), lambda i, lo, n, nu: (jnp.clip(i - npt, 0, nst - 1), 0, 0))],
        out_specs=pl.BlockSpec(memory_space=pl.ANY),
        scratch_shapes=[pltpu.VMEM((2 * tm, 1, w), I32), pltpu.VMEM((rows, 1, w), I32),
                        pltpu.SemaphoreType.DMA((2,)), pltpu.SemaphoreType.DMA((1,))],
    )
    return pl.pallas_call(
        functools.partial(_dispatch_kernel, tm=tm, n_prompt_tiles=hnp_p.shape[0] // tm, rows=rows),
        out_shape=jax.ShapeDtypeStruct((n_slots, 1, w), I32),
        grid_spec=grid_spec,
        compiler_params=_cparams(("arbitrary",)),
        name="dispatch",
    )(pad_lo, pad_n, n_used, dest_tiles, hnp_p, hnp_s)


def _dot_by_columns(x, w_ref, tn):
    cols = [jnp.dot(x, w_ref[:, c:c + tn].astype(BF16), preferred_element_type=F32)
            for c in range(0, w_ref.shape[1], tn)]
    return jnp.concatenate(cols, axis=1)


def _expert_kernel(be_ref, nu_ref, xs_ref, wg_ref, wu_ref, wd_ref, ys_ref, x2d):
    b = pl.program_id(0)

    @pl.when(b < nu_ref[0])
    def _():
        x2d[...] = xs_ref[...].reshape(x2d.shape)
        x = _unpack_bf16_pairs(x2d[...]).astype(BF16)
        g = _dot_by_columns(x, wg_ref, MXU_COLS)
        u = _dot_by_columns(x, wu_ref, MXU_COLS)
        hmid = (_silu(g) * u).astype(BF16)
        y = _dot_by_columns(hmid, wd_ref, MXU_COLS)
        ys_ref[...] = _pack_bf16_pairs(y).reshape(ys_ref.shape)


def _experts(block_e, n_used, xs, w_gate, w_up, w_down, layer, rows):
    n_slots, _, w = xs.shape
    nblk = n_slots // rows
    d = 2 * w
    used = lambda b, nu: jnp.minimum(b, nu[0] - 1)
    wspec = lambda shape: pl.BlockSpec((None, None) + shape, lambda b, be, nu: (layer, be[used(b, nu)], 0, 0))
    slot_spec = pl.BlockSpec((rows, 1, w), lambda b, be, nu: (used(b, nu), 0, 0))
    grid_spec = pltpu.PrefetchScalarGridSpec(
        num_scalar_prefetch=2,
        grid=(nblk,),
        in_specs=[slot_spec, wspec((d, EXPERT_FF)), wspec((d, EXPERT_FF)), wspec((EXPERT_FF, d))],
        out_specs=slot_spec,
        scratch_shapes=[pltpu.VMEM((rows, w), I32)],
    )
    return pl.pallas_call(
        _expert_kernel,
        out_shape=jax.ShapeDtypeStruct((n_slots, 1, w), I32),
        grid_spec=grid_spec,
        input_output_aliases={2: 0},
        compiler_params=_cparams(("arbitrary",)),
        name="experts",
    )(block_e, n_used, xs, w_gate, w_up, w_down)


def _slot_plan(idx_t, pos_t, counts, rows, n_blocks):
    nblk_e = (counts + rows - 1) // rows
    blk_end = jnp.cumsum(nblk_e)
    blk_start = blk_end - nblk_e
    is_e = idx_t[:, :, None] == jnp.arange(N_EXPERTS, dtype=I32)[None, None, :]
    dest = jnp.sum(jnp.where(is_e, (blk_start * rows)[None, None, :], 0), axis=-1) + pos_t
    blocks = jnp.arange(n_blocks, dtype=I32)
    block_e = jnp.minimum(jnp.sum((blk_end[None, :] <= blocks[:, None]).astype(I32), axis=1), N_EXPERTS - 1)
    pad_lo = (blk_start * rows + counts).astype(I32)
    pad_n = (nblk_e * rows - counts).astype(I32)
    return dest.astype(I32), block_e, blk_end[-1:].astype(I32), pad_lo, pad_n


def _dest_tiles(dest, tm):
    t = dest.shape[1]
    return dest.reshape(TOP_K, t // tm, tm).transpose(1, 0, 2).reshape(t // tm, 1, TOP_K * tm)


def _combine_kernel(dest_ref, ndest_ref, wt_ref, hn_ref, x1_ref, gt_ref, wg_ref, wu_ref, wd_ref, ys_hbm, o_ref,
                    ybuf, y2d, sem, *, per_row, tm):
    i = pl.program_id(0)
    n = pl.num_programs(0)
    par = i % 2
    per_tile = TOP_K * tm

    def gather(tbl_ref, tile_slot):
        def body(t, carry):
            for k in range(TOP_K):
                slot = tbl_ref[0, 0, k * tm + t]
                pltpu.make_async_copy(ys_hbm.at[pl.ds(slot, 1)],
                                      ybuf.at[pl.ds(tile_slot * per_tile + k * tm + t, 1)],
                                      sem.at[tile_slot]).start(priority=k % 2)
            return carry
        lax.fori_loop(0, tm, body, 0)

    @pl.when(i == 0)
    def _():
        gather(dest_ref, 0)

    @pl.when(i + 1 < n)
    def _():
        gather(ndest_ref, 1 - par)

    h = hn_ref[...].astype(BF16)
    g = jnp.dot(h, wg_ref[...], preferred_element_type=F32)
    u = jnp.dot(h, wu_ref[...], preferred_element_type=F32)
    moe = jnp.dot((_silu(g) * u).astype(BF16), wd_ref[...], preferred_element_type=F32)

    _wait_rows(ys_hbm, ybuf.at[pl.ds(par * per_tile, per_tile)], sem.at[par], per_tile)
    wt = wt_ref[...]
    for k in range(TOP_K):
        y2d[...] = ybuf[pl.ds(pl.multiple_of(par * per_tile + k * tm, tm), tm)].reshape(y2d.shape)
        moe = moe + wt[:, k:k + 1] * _unpack_bf16_pairs(y2d[...])
    o_ref[...] = x1_ref[...] + _mod_rows(gt_ref, per_row) * moe


def _combine(ys, dest_tiles, wts, hn, x1, mod, mod_block_rows, mod_row_block, per_row, ws_g, ws_u, ws_d, tm):
    t, d = hn.shape
    w = ys.shape[2]
    n_tiles = t // tm
    row = lambda i: (i, 0)
    const = lambda shape: pl.BlockSpec(shape, lambda i: (0, 0))
    gt_spec = pl.BlockSpec((mod_block_rows, d), (lambda i: (i, 5)) if per_row else (lambda i: (mod_row_block, 5)))
    tile = lambda f: pl.BlockSpec((1, 1, TOP_K * tm), lambda i: (f(i), 0, 0), memory_space=pltpu.SMEM)
    return pl.pallas_call(
        functools.partial(_combine_kernel, per_row=per_row, tm=tm),
        out_shape=jax.ShapeDtypeStruct((t, d), F32),
        grid=(n_tiles,),
        in_specs=[tile(lambda i: i), tile(lambda i: jnp.minimum(i + 1, n_tiles - 1)),
                  pl.BlockSpec((tm, TOP_K), row), pl.BlockSpec((tm, d), row), pl.BlockSpec((tm, d), row),
                  gt_spec, const((d, EXPERT_FF)), const((d, EXPERT_FF)), const((EXPERT_FF, d)),
                  pl.BlockSpec(memory_space=pl.ANY)],
        out_specs=pl.BlockSpec((tm, d), row),
        scratch_shapes=[pltpu.VMEM((2 * TOP_K * tm, 1, w), I32), pltpu.VMEM((tm, w), I32),
                        pltpu.SemaphoreType.DMA((2,))],
        compiler_params=_cparams(("arbitrary",)),
        name="combine",
    )(dest_tiles, dest_tiles, wts, hn, x1, mod, ws_g, ws_u, ws_d, ys)


def _rope_tables(pos):
    half = HEAD_DIM_QK // 2
    inv = ROPE_THETA ** (-jnp.arange(half, dtype=F32) * 2.0 / HEAD_DIM_QK)
    ang = pos.astype(F32)[:, None] * inv[None, :]
    cos, sin = jnp.cos(ang), jnp.sin(ang)
    return jnp.tile(jnp.concatenate([cos, cos], axis=1), (1, 2)), jnp.tile(jnp.concatenate([-sin, sin], axis=1), (1, 2))


def _n_expert_blocks(n_assign, rows):
    return -(-(n_assign + N_EXPERTS * (rows - 1)) // rows)


def kernel(x_prompt, x_sample, c_prompt, c_sample, cache_k, cache_v, state_conv, page_table, w_ada, b_ada, g_norm1, g_norm2, w_in, g_q, g_k, lambda_q1, lambda_k1, lambda_q2, lambda_k2, g_sub, w_dw, b_dw, ln_g, ln_b, w_out, w_router, router_bias, w_e_gate, w_e_up, w_e_down, w_s_gate, w_s_up, w_s_down):
    layer = 0
    lam_init = 0.8 - 0.6 * math.exp(-0.3 * layer)
    d = D_MODEL
    tp = x_prompt.shape[1]
    ns = x_sample.shape[0]
    t_all = tp + ns
    xp = x_prompt.reshape(tp, d)
    xs = x_sample.reshape(ns, d)

    row2 = lambda a: a[layer].reshape(1, -1)
    w_in_bf = w_in[layer].astype(BF16)
    wo_a = w_out[layer, :ATTN_WIDTH].astype(BF16)
    wo_c = w_out[layer, ATTN_WIDTH:].astype(BF16)
    ws_g, ws_u, ws_d = (w[layer].astype(BF16) for w in (w_s_gate, w_s_up, w_s_down))
    wr_t = w_router[layer].T
    wr_hi = wr_t.astype(BF16)
    wr_lo = (wr_t - wr_hi.astype(F32)).astype(BF16)
    gq = jnp.tile(row2(g_q), (1, 2))
    gk = jnp.tile(row2(g_k), (1, 2))
    lane = jnp.arange(LANES)
    bd = (lane[:, None] // HEAD_DIM_QK == lane[None, :] // HEAD_DIM_QK).astype(BF16)
    lams = [row2(a) for a in (lambda_q1, lambda_k1, lambda_q2, lambda_k2)]

    c_all = jnp.concatenate([c_sample, c_prompt, jnp.zeros((SUBLANES - 1, d), F32)], axis=0)
    mod = _ada(c_all, w_ada[layer], row2(b_ada))
    p_mod = dict(mod_block_rows=SUBLANES, mod_row_block=ns // SUBLANES, per_row=False)
    s_mod = dict(mod_block_rows=ns, mod_row_block=0, per_row=True)

    cos_p, sin_p = _rope_tables(jnp.arange(tp, dtype=I32))
    cos_s, sin_s = _rope_tables(jnp.full((ns,), PAST_LEN, I32))

    q_p, k_p, kb_p, v_p, vb_p, glu_p = _inproj(xp, mod, g1=row2(g_norm1), w_in_bf=w_in_bf, gq=gq, gk=gk,
                                               cos=cos_p, sin=sin_p, bd=bd, tm=512, **p_mod)
    cy_p = _conv_prompt(glu_p, w_dw[layer], row2(b_dw), row2(ln_g), row2(ln_b))
    on_p = _flash_prompt(q_p, kb_p, vb_p, *lams, row2(g_sub), lam_init)

    q_s, k_s, _, v_s, _, glu_s = _inproj(xs, mod, g1=row2(g_norm1), w_in_bf=w_in_bf, gq=gq, gk=gk,
                                         cos=cos_s, sin=sin_s, bd=bd, tm=ns, **s_mod)
    cy_s = _conv_sample(state_conv[layer], glu_s, w_dw[layer], row2(b_dw), row2(ln_g), row2(ln_b))
    n_phys = cache_k.shape[1]
    ck_pages = jnp.transpose(cache_k, (0, 1, 3, 4, 5, 2)).reshape(-1, ATTN_WIDTH, PAGE_SIZE)
    cv_pages = cache_v.reshape(-1, PAGE_SIZE * N_HEADS, HEAD_DIM_V)
    to_cols = lambda a: jnp.pad(jnp.swapaxes(a.reshape(ns, 2 * N_HEADS, HEAD_DIM_QK), 1, 2),
                                ((0, 0), (0, 0), (0, LANES - 2 * N_HEADS)))
    qt_s = to_cols(q_s[0].astype(F32) + q_s[1].astype(F32))
    on_s = _attn_sample(page_table + layer * n_phys, qt_s, to_cols(k_s), v_s.reshape(ns, N_HEADS, HEAD_DIM_V),
                        *lams, row2(g_sub), ck_pages, cv_pages, lam_init).reshape(ns, ATTN_WIDTH)

    x1_p, hn_p, hnp_p = _outproj(on_p, cy_p, xp, mod, g2=row2(g_norm2), wo_a=wo_a, wo_c=wo_c, tm=256, **p_mod)
    x1_s, hn_s, hnp_s = _outproj(on_s, cy_s, xs, mod, g2=row2(g_norm2), wo_a=wo_a, wo_c=wo_c, tm=ns, **s_mod)

    tm_r = ns
    tri = (jnp.arange(tm_r)[:, None] < jnp.arange(tm_r)[None, :]).astype(BF16)
    idx_t, wts_t, pos_t, counts = _route(hn_p, hn_s, wr_hi, wr_lo, router_bias[layer].reshape(-1, 1), tri, tm_r)
    n_blocks = _n_expert_blocks(TOP_K * t_all, EXPERT_ROWS)
    dest, block_e, n_used, pad_lo, pad_n = _slot_plan(idx_t, pos_t, counts.reshape(-1), EXPERT_ROWS, n_blocks)

    slots_x = _dispatch(pad_lo, pad_n, n_used, _dest_tiles(dest, tm_r), hnp_p, hnp_s, n_blocks * EXPERT_ROWS,
                        EXPERT_ROWS, tm_r)
    slots_y = _experts(block_e, n_used, slots_x, w_e_gate, w_e_up, w_e_down, layer, EXPERT_ROWS)
    tm_c = 256
    y_p = _combine(slots_y, _dest_tiles(dest[:, :tp], tm_c), wts_t[:, :tp].T, hn_p, x1_p, mod,
                   ws_g=ws_g, ws_u=ws_u, ws_d=ws_d, tm=tm_c, **p_mod)
    y_s = _combine(slots_y, _dest_tiles(dest[:, tp:], ns), wts_t[:, tp:].T, hn_s, x1_s, mod,
                   ws_g=ws_g, ws_u=ws_u, ws_d=ws_d, tm=ns, **s_mod)

    conv_p = glu_p[tp - CONV_STATE:]
    conv_s = jnp.concatenate([state_conv[layer][:, 1:], glu_s[:, None, :]], axis=1)
    return (y_p.reshape(1, tp, d), y_s.reshape(ns, 1, d),
            k_p.reshape(1, 1, tp, N_HEADS, 2, HEAD_DIM_QK), v_p.reshape(1, 1, tp, N_HEADS, HEAD_DIM_V),
            conv_p.reshape(1, 1, CONV_STATE, CONV_CH),
            k_s.reshape(1, ns, 1, N_HEADS, 2, HEAD_DIM_QK), v_s.reshape(1, ns, 1, N_HEADS, HEAD_DIM_V),
            conv_s.reshape(1, ns, CONV_STATE, CONV_CH))
```

```python
import functools
import math

import jax
import jax.numpy as jnp
from jax import lax
from jax.experimental import pallas as pl
from jax.experimental.pallas import tpu as pltpu

F32 = jnp.float32
BF16 = jnp.bfloat16
I32 = jnp.int32

D_MODEL = 2048
ATTN_WIDTH = 1024
CONV_CH = 1024
N_HEADS = 8
HEAD_DIM_V = 128
HEAD_DIM_QK = 64
CONV_LEN = 31
CONV_STATE = CONV_LEN - 1
N_EXPERTS = 256
TOP_K = 8
N_EXPERT_GROUPS = 8
GROUP_SIZE = N_EXPERTS // N_EXPERT_GROUPS
TOPK_GROUPS = 4
EXPERT_FF = 512
ROUTED_SCALE = 2.5
ROPE_THETA = 10000.0
NORM_EPS = 1e-6
PAST_LEN = 2048
PAGE_SIZE = 128
IN_COLS = 3 * ATTN_WIDTH + 2 * CONV_CH
LANES = 128
SUBLANES = 8
COL_TILE = 1024
MXU_COLS = 256
CONV_HALO = 32
EXPERT_ROWS = 320
COMBINE_ROWS = 32
VMEM_LIMIT = 56 * 1024 * 1024
Q_SCALE = HEAD_DIM_QK ** -0.5 * math.log2(math.e)


def _cparams(sem, vmem=VMEM_LIMIT):
    return pltpu.CompilerParams(dimension_semantics=sem, vmem_limit_bytes=vmem)


def _rms(x, eps=NORM_EPS):
    return x * lax.rsqrt(jnp.mean(x * x, axis=-1, keepdims=True) + eps)


def _silu(x):
    return x * jax.nn.sigmoid(x)


def _mod_rows(ref, per_row):
    return ref[...] if per_row else ref[0:1, :]


def _diff_lambda(lq1_ref, lk1_ref, lq2_ref, lk2_ref, lam_init):
    a = jnp.sum(lq1_ref[...] * lk1_ref[...], axis=-1, keepdims=True)
    b = jnp.sum(lq2_ref[...] * lk2_ref[...], axis=-1, keepdims=True)
    return jnp.exp(a) - jnp.exp(b) + lam_init


def _ada_kernel(c_ref, w_ref, b_ref, o_ref):
    s = _silu(c_ref[...]).astype(BF16)
    o_ref[...] = jnp.dot(s, w_ref[...].astype(BF16), preferred_element_type=F32) + b_ref[...]


def _ada(c_all, w_ada, b_ada):
    rows, d = c_all.shape
    n = w_ada.shape[1]
    tn = 1024
    return pl.pallas_call(
        _ada_kernel,
        out_shape=jax.ShapeDtypeStruct((rows, n), F32),
        grid=(n // tn,),
        in_specs=[pl.BlockSpec((rows, d), lambda j: (0, 0)),
                  pl.BlockSpec((d, tn), lambda j: (0, j)),
                  pl.BlockSpec((1, tn), lambda j: (0, j))],
        out_specs=pl.BlockSpec((rows, tn), lambda j: (0, j)),
        compiler_params=_cparams(("arbitrary",)),
        name="adaln",
    )(c_all, w_ada, b_ada)


def _qk_norm_rope(u, g, cos, sin, bd):
    lane = lax.broadcasted_iota(I32, (u.shape[0], LANES), 1)
    first_half = (lane & 32) == 0
    outs = []
    for c in range(u.shape[1] // LANES):
        x = u[:, c * LANES:(c + 1) * LANES]
        ss = jnp.dot((x * x).astype(BF16), bd, preferred_element_type=F32)
        xn = x * lax.rsqrt(ss * (1.0 / HEAD_DIM_QK) + NORM_EPS) * g
        rot = jnp.where(first_half, pltpu.roll(xn, 96, 1), pltpu.roll(xn, 32, 1))
        outs.append(xn * cos + rot * sin)
    return jnp.concatenate(outs, axis=1)


def _inproj_kernel(x_ref, sh_ref, sc_ref, g1_ref, w_ref, gq_ref, gk_ref, cos_ref, sin_ref, bd_ref,
                   q_ref, k_ref, kb_ref, v_ref, vb_ref, glu_ref, xn_scr, val_scr, *, per_row):
    j = pl.program_id(1)

    @pl.when(j == 0)
    def _():
        y = _rms(x_ref[...]) * g1_ref[...]
        xn_scr[...] = (y * (1.0 + _mod_rows(sc_ref, per_row)) + _mod_rows(sh_ref, per_row)).astype(BF16)

    u = jnp.dot(xn_scr[...], w_ref[...], preferred_element_type=F32)

    @pl.when(j == 0)
    def _():
        q = _qk_norm_rope(u, gq_ref[...], cos_ref[...], sin_ref[...], bd_ref[...]) * Q_SCALE
        lane = lax.broadcasted_iota(I32, q.shape, 1)
        map0 = (lane & HEAD_DIM_QK) == 0
        q_ref[0] = jnp.where(map0, q, 0.0).astype(BF16)
        q_ref[1] = jnp.where(map0, 0.0, q).astype(BF16)

    @pl.when(j == 1)
    def _():
        k = _qk_norm_rope(u, gk_ref[...], cos_ref[...], sin_ref[...], bd_ref[...])
        k_ref[...] = k
        kb_ref[...] = k.astype(BF16)

    @pl.when(j == 2)
    def _():
        v_ref[...] = u
        vb_ref[...] = u.astype(BF16)

    @pl.when(j == 3)
    def _():
        val_scr[...] = u

    @pl.when(j == 4)
    def _():
        glu_ref[...] = val_scr[...] * jax.nn.sigmoid(u)


def _inproj(x, mod, mod_block_rows, mod_row_block, per_row, g1, w_in_bf, gq, gk, cos, sin, bd, tm):
    t, d = x.shape
    nj = IN_COLS // COL_TILE
    row = lambda i, j: (i, 0)
    mod_spec = lambda chunk: pl.BlockSpec(
        (mod_block_rows, d), (lambda i, j: (i, chunk)) if per_row else (lambda i, j: (mod_row_block, chunk)))
    const = lambda shape: pl.BlockSpec(shape, lambda i, j: (0, 0))
    return pl.pallas_call(
        functools.partial(_inproj_kernel, per_row=per_row),
        out_shape=(jax.ShapeDtypeStruct((2, t, COL_TILE), BF16),
                   jax.ShapeDtypeStruct((t, COL_TILE), F32), jax.ShapeDtypeStruct((t, COL_TILE), BF16),
                   jax.ShapeDtypeStruct((t, COL_TILE), F32), jax.ShapeDtypeStruct((t, COL_TILE), BF16),
                   jax.ShapeDtypeStruct((t, COL_TILE), F32)),
        grid=(t // tm, nj),
        in_specs=[pl.BlockSpec((tm, d), row), mod_spec(0), mod_spec(1), const((1, d)),
                  pl.BlockSpec((d, COL_TILE), lambda i, j: (0, j)),
                  const((1, LANES)), const((1, LANES)),
                  pl.BlockSpec((tm, LANES), row), pl.BlockSpec((tm, LANES), row), const((LANES, LANES))],
        out_specs=(pl.BlockSpec((2, tm, COL_TILE), lambda i, j: (0, i, 0)),
                   pl.BlockSpec((tm, COL_TILE), row), pl.BlockSpec((tm, COL_TILE), row),
                   pl.BlockSpec((tm, COL_TILE), row), pl.BlockSpec((tm, COL_TILE), row),
                   pl.BlockSpec((tm, COL_TILE), row)),
        scratch_shapes=[pltpu.VMEM((tm, d), BF16), pltpu.VMEM((tm, COL_TILE), F32)],
        compiler_params=_cparams(("arbitrary", "arbitrary")),
        name="inproj",
    )(x, mod, mod, g1, w_in_bf, gq, gk, cos, sin, bd)


def _layernorm_swish(y, g, b):
    mu = jnp.mean(y, axis=-1, keepdims=True)
    yc = y - mu
    var = jnp.mean(yc * yc, axis=-1, keepdims=True)
    yn = yc * lax.rsqrt(var + NORM_EPS) * g + b
    return yn * jax.nn.sigmoid(yn)


def _conv_kernel(cur_ref, prev_ref, w_ref, b_ref, lg_ref, lb_ref, y_ref, xs_scr, sh_scr, *, tm, rc):
    i = pl.program_id(0)
    xs_scr[0:CONV_HALO, :] = jnp.where(i == 0, 0.0, prev_ref[...])
    xs_scr[CONV_HALO:, :] = cur_ref[...]
    n_sh = tm + CONV_HALO - SUBLANES
    for b in range(1, SUBLANES):
        sh_scr[b - 1] = xs_scr[b:b + n_sh, :]

    first = CONV_HALO - CONV_STATE

    def chunk(r, carry):
        r0 = pl.multiple_of(r * rc, rc)
        acc = jnp.zeros((rc, CONV_CH), F32)
        for j in range(CONV_LEN):
            o = first + j
            b, a = o % SUBLANES, (o // SUBLANES) * SUBLANES
            src = xs_scr if b == 0 else sh_scr.at[b - 1]
            acc = acc + w_ref[j:j + 1, :] * src[pl.ds(r0 + a, rc), :]
        y = _layernorm_swish(acc + b_ref[...], lg_ref[...], lb_ref[...])
        y_ref[pl.ds(r0, rc), :] = y.astype(y_ref.dtype)
        return carry

    lax.fori_loop(0, tm // rc, chunk, 0)


def _conv_prompt(glu, w_dw, b_dw, ln_g, ln_b, tm=256, rc=32):
    t, c = glu.shape
    per = tm // CONV_HALO
    const = lambda shape: pl.BlockSpec(shape, lambda i: (0, 0))
    return pl.pallas_call(
        functools.partial(_conv_kernel, tm=tm, rc=rc),
        out_shape=jax.ShapeDtypeStruct((t, c), BF16),
        grid=(t // tm,),
        in_specs=[pl.BlockSpec((tm, c), lambda i: (i, 0)),
                  pl.BlockSpec((CONV_HALO, c), lambda i: (jnp.maximum(i * per - 1, 0), 0)),
                  const((CONV_LEN, c)), const((1, c)), const((1, c)), const((1, c))],
        out_specs=pl.BlockSpec((tm, c), lambda i: (i, 0)),
        scratch_shapes=[pltpu.VMEM((tm + CONV_HALO, c), F32),
                        pltpu.VMEM((SUBLANES - 1, tm + CONV_HALO - SUBLANES, c), F32)],
        compiler_params=_cparams(("arbitrary",)),
        name="conv_prompt",
    )(glu, glu, w_dw, b_dw, ln_g, ln_b)


def _conv_sample_kernel(st_ref, glu_ref, w_ref, b_ref, lg_ref, lb_ref, y_ref):
    w = w_ref[...]
    acc = jnp.sum(st_ref[...] * w[None, 0:CONV_STATE, :], axis=1) + glu_ref[...] * w[CONV_STATE:CONV_LEN, :]
    y_ref[...] = _layernorm_swish(acc + b_ref[...], lg_ref[...], lb_ref[...]).astype(y_ref.dtype)


def _conv_sample(state, glu, w_dw, b_dw, ln_g, ln_b, sb=32):
    b, s, c = state.shape
    const = lambda shape: pl.BlockSpec(shape, lambda i: (0, 0))
    return pl.pallas_call(
        _conv_sample_kernel,
        out_shape=jax.ShapeDtypeStruct((b, c), BF16),
        grid=(b // sb,),
        in_specs=[pl.BlockSpec((sb, s, c), lambda i: (i, 0, 0)), pl.BlockSpec((sb, c), lambda i: (i, 0)),
                  const((CONV_LEN, c)), const((1, c)), const((1, c)), const((1, c))],
        out_specs=pl.BlockSpec((sb, c), lambda i: (i, 0)),
        compiler_params=_cparams(("arbitrary",)),
        name="conv_sample",
    )(state, glu, w_dw, b_dw, ln_g, ln_b)


def _flash_kernel(q_ref, k_ref, v_ref, lq1_ref, lk1_ref, lq2_ref, lk2_ref, gs_ref, o_ref,
                  vx_scr, m_scr, acc_scr, sa_scr, sb_scr, *, tq, tk, lam_init):
    i = pl.program_id(1)
    dv = HEAD_DIM_V

    @pl.when(i == 0)
    def _():
        vx_scr[:, 0:dv] = v_ref[...]
        vx_scr[:, dv:2 * dv] = jnp.ones((vx_scr.shape[0], dv), BF16)

    q = q_ref[...].reshape(2 * tq, LANES)
    m_scr[...] = jnp.full(m_scr.shape, -jnp.inf, F32)
    acc_scr[...] = jnp.zeros(acc_scr.shape, F32)

    def scores(j, s_ref):
        k = k_ref[pl.ds(pl.multiple_of(j * tk, tk), tk), :]
        s_ref[...] = lax.dot_general(q, k, (((1,), (1,)), ((), ())), preferred_element_type=F32)

    def consume(j, s_ref, masked):
        k0 = pl.multiple_of(j * tk, tk)
        s = s_ref[...]
        if masked:
            row = lax.broadcasted_iota(I32, s.shape, 0)
            col = lax.broadcasted_iota(I32, s.shape, 1)
            qpos = i * tq + jnp.where(row >= tq, row - tq, row)
            s = jnp.where(k0 + col <= qpos, s, -jnp.inf)
        m_prev = m_scr[...]
        m_new = jnp.maximum(m_prev, jnp.max(s, axis=-1, keepdims=True))
        alpha = jnp.exp2(m_prev - m_new)
        p = jnp.exp2(s - m_new).astype(BF16)
        acc_scr[...] = alpha * acc_scr[...] + jnp.dot(p, vx_scr[pl.ds(k0, tk), :], preferred_element_type=F32)
        m_scr[...] = m_new

    n_full = (i * tq) // tk
    scores(0, sa_scr)

    def pair(t, carry):
        scores(2 * t + 1, sb_scr)
        consume(2 * t, sa_scr, False)
        scores(2 * t + 2, sa_scr)
        consume(2 * t + 1, sb_scr, False)
        return carry

    lax.fori_loop(0, n_full // 2, pair, 0)

    @pl.when(n_full % 2 == 1)
    def _():
        scores(n_full, sb_scr)
        consume(n_full - 1, sa_scr, False)
        consume(n_full, sb_scr, True)

    @pl.when(n_full % 2 == 0)
    def _():
        consume(n_full, sa_scr, True)

    acc = acc_scr[...]
    o = acc[:, 0:dv] / acc[:, dv:2 * dv]
    lam = _diff_lambda(lq1_ref, lk1_ref, lq2_ref, lk2_ref, lam_init)
    od = o[0:tq] - lam * o[tq:2 * tq]
    o_ref[...] = (_rms(od) * gs_ref[...] * (1.0 - lam_init)).astype(o_ref.dtype)


def _flash_prompt(q2, kb, vb, lq1, lk1, lq2, lk2, g_sub, lam_init, tq=512, tk=512):
    _, t, _ = q2.shape
    tk = min(tk, t)
    assert tk % tq == 0 and t % tk == 0
    const = lambda shape: pl.BlockSpec(shape, lambda h, i: (0, 0))
    return pl.pallas_call(
        functools.partial(_flash_kernel, tq=tq, tk=tk, lam_init=lam_init),
        out_shape=jax.ShapeDtypeStruct((t, ATTN_WIDTH), BF16),
        grid=(N_HEADS, t // tq),
        in_specs=[pl.BlockSpec((2, tq, LANES), lambda h, i: (0, i, h)),
                  pl.BlockSpec((t, LANES), lambda h, i: (0, h)),
                  pl.BlockSpec((t, LANES), lambda h, i: (0, h)),
                  const((1, HEAD_DIM_QK)), const((1, HEAD_DIM_QK)), const((1, HEAD_DIM_QK)),
                  const((1, HEAD_DIM_QK)), const((1, LANES))],
        out_specs=pl.BlockSpec((tq, LANES), lambda h, i: (i, h)),
        scratch_shapes=[pltpu.VMEM((t, 2 * HEAD_DIM_V), BF16), pltpu.VMEM((2 * tq, 1), F32),
                        pltpu.VMEM((2 * tq, 2 * HEAD_DIM_V), F32),
                        pltpu.VMEM((2 * tq, tk), F32), pltpu.VMEM((2 * tq, tk), F32)],
        compiler_params=_cparams(("arbitrary", "arbitrary")),
        name="flash_prompt",
    )(q2, kb, vb, lq1, lk1, lq2, lk2, g_sub)


def _paged_copies(pt_ref, ck_hbm, cv_hbm, kbuf, vbuf, sem, b, slot, n_pages):
    copies = []
    for p in range(n_pages):
        pg = pt_ref[b, p]
        copies.append(pltpu.make_async_copy(ck_hbm.at[pg], kbuf.at[slot, p], sem.at[0, slot]))
        copies.append(pltpu.make_async_copy(cv_hbm.at[pg], vbuf.at[slot, p], sem.at[1, slot]))
    return copies


def _sattn_kernel(pt_ref, qt_ref, kts_ref, vs_ref, lq1_ref, lk1_ref, lq2_ref, lk2_ref, gs_ref,
                  ck_hbm, cv_hbm, o_ref, kbuf, vbuf, qb_scr, s_scr, sem, *, n_pages, lam_init):
    b = pl.program_id(0)
    nb = pl.num_programs(0)
    slot = b % 2

    @pl.when(b == 0)
    def _():
        for c in _paged_copies(pt_ref, ck_hbm, cv_hbm, kbuf, vbuf, sem, 0, 0, n_pages):
            c.start()

    @pl.when(b + 1 < nb)
    def _():
        for c in _paged_copies(pt_ref, ck_hbm, cv_hbm, kbuf, vbuf, sem, b + 1, 1 - slot, n_pages):
            c.start()

    nr = 2 * N_HEADS
    dk = HEAD_DIM_QK
    qt = qt_ref[...]
    for r in range(nr):
        qb_scr[r] = jnp.broadcast_to(qt[:, r:r + 1], (dk, LANES))
    s_self_lanes = jnp.sum(qt * kts_ref[...], axis=0, keepdims=True)
    rr = lax.broadcasted_iota(I32, (nr, LANES), 0)
    ll = lax.broadcasted_iota(I32, (nr, LANES), 1)
    s_self = jnp.sum(jnp.where(ll == (rr % N_HEADS) * 2 + rr // N_HEADS, s_self_lanes, 0.0),
                     axis=-1, keepdims=True)

    for c in _paged_copies(pt_ref, ck_hbm, cv_hbm, kbuf, vbuf, sem, b, slot, n_pages):
        c.wait()

    for p in range(n_pages):
        for r in range(nr):
            h, c = divmod(r, 2)
            kt = kbuf[slot, p, pl.ds(r * dk, dk), :]
            row = c * N_HEADS + h
            s_scr[row:row + 1, p * PAGE_SIZE:(p + 1) * PAGE_SIZE] = jnp.sum(qb_scr[r] * kt, axis=0, keepdims=True)

    s = s_scr[...]
    m = jnp.maximum(jnp.max(s, axis=-1, keepdims=True), s_self)
    e = jnp.exp2(s - m)
    e_self = jnp.exp2(s_self - m)
    inv_l = 1.0 / (jnp.sum(e, axis=-1, keepdims=True) + e_self)
    lam = _diff_lambda(lq1_ref, lk1_ref, lq2_ref, lk2_ref, lam_init)
    w = inv_l[0:N_HEADS]
    w1 = lam * inv_l[N_HEADS:nr]
    pd = (e[0:N_HEADS] * w - e[N_HEADS:nr] * w1).astype(BF16)
    pd_self = e_self[0:N_HEADS] * w - e_self[N_HEADS:nr] * w1

    head_row = lax.broadcasted_iota(I32, (N_HEADS, HEAD_DIM_V), 0)
    o = pd_self * vs_ref[...]
    for h in range(N_HEADS):
        acc = jnp.zeros((N_HEADS, HEAD_DIM_V), F32)
        for p in range(n_pages):
            vh = vbuf[slot, p, pl.ds(h, PAGE_SIZE, stride=N_HEADS), :].astype(BF16)
            acc = acc + jnp.dot(pd[:, p * PAGE_SIZE:(p + 1) * PAGE_SIZE], vh, preferred_element_type=F32)
        o = o + jnp.where(head_row == h, acc, 0.0)
    o_ref[...] = _rms(o) * gs_ref[...] * (1.0 - lam_init)


def _attn_sample(page_table, qt, kts, vs, lq1, lk1, lq2, lk2, g_sub, ck_pages, cv_pages, lam_init):
    nb, n_pages = page_table.shape
    const2 = lambda shape: pl.BlockSpec(shape, lambda b, pt: (0, 0))
    per_seq = lambda shape: pl.BlockSpec((None,) + shape, lambda b, pt: (b, 0, 0))
    page_rows = ck_pages.shape[1]
    grid_spec = pltpu.PrefetchScalarGridSpec(
        num_scalar_prefetch=1,
        grid=(nb,),
        in_specs=[per_seq((HEAD_DIM_QK, LANES)), per_seq((HEAD_DIM_QK, LANES)), per_seq((N_HEADS, HEAD_DIM_V)),
                  const2((1, HEAD_DIM_QK)), const2((1, HEAD_DIM_QK)), const2((1, HEAD_DIM_QK)),
                  const2((1, HEAD_DIM_QK)), const2((1, HEAD_DIM_V)),
                  pl.BlockSpec(memory_space=pl.ANY), pl.BlockSpec(memory_space=pl.ANY)],
        out_specs=per_seq((N_HEADS, HEAD_DIM_V)),
        scratch_shapes=[pltpu.VMEM((2, n_pages, page_rows, LANES), F32),
                        pltpu.VMEM((2, n_pages, page_rows, LANES), F32),
                        pltpu.VMEM((2 * N_HEADS, HEAD_DIM_QK, LANES), F32),
                        pltpu.VMEM((2 * N_HEADS, n_pages * PAGE_SIZE), F32),
                        pltpu.SemaphoreType.DMA((2, 2))],
    )
    return pl.pallas_call(
        functools.partial(_sattn_kernel, n_pages=n_pages, lam_init=lam_init),
        out_shape=jax.ShapeDtypeStruct((nb, N_HEADS, HEAD_DIM_V), F32),
        grid_spec=grid_spec,
        compiler_params=_cparams(("arbitrary",)),
        name="attn_sample",
    )(page_table, qt, kts, vs, lq1, lk1, lq2, lk2, g_sub, ck_pages, cv_pages)


def _outproj_kernel(on_ref, cy_ref, x_ref, gt_ref, sh_ref, sc_ref, g2_ref, wa_ref, wc_ref, x1_ref, hn_ref, hnp_ref,
                    *, per_row):
    mix = (jnp.dot(on_ref[...].astype(BF16), wa_ref[...], preferred_element_type=F32)
           + jnp.dot(cy_ref[...], wc_ref[...], preferred_element_type=F32))
    x1 = x_ref[...] + _mod_rows(gt_ref, per_row) * mix
    x1_ref[...] = x1
    y = _rms(x1) * g2_ref[...]
    hn = y * (1.0 + _mod_rows(sc_ref, per_row)) + _mod_rows(sh_ref, per_row)
    hn_ref[...] = hn
    hnp_ref[...] = _pack_bf16_pairs(hn).reshape(hnp_ref.shape)


def _outproj(on, cy, x, mod, mod_block_rows, mod_row_block, per_row, g2, wo_a, wo_c, tm):
    t, d = x.shape
    row = lambda i: (i, 0)
    mod_spec = lambda chunk: pl.BlockSpec(
        (mod_block_rows, d), (lambda i: (i, chunk)) if per_row else (lambda i: (mod_row_block, chunk)))
    const = lambda shape: pl.BlockSpec(shape, lambda i: (0, 0))
    return pl.pallas_call(
        functools.partial(_outproj_kernel, per_row=per_row),
        out_shape=(jax.ShapeDtypeStruct((t, d), F32), jax.ShapeDtypeStruct((t, d), F32),
                   jax.ShapeDtypeStruct((t, 1, d // 2), I32)),
        grid=(t // tm,),
        in_specs=[pl.BlockSpec((tm, ATTN_WIDTH), row), pl.BlockSpec((tm, CONV_CH), row), pl.BlockSpec((tm, d), row),
                  mod_spec(2), mod_spec(3), mod_spec(4), const((1, d)),
                  const((ATTN_WIDTH, d)), const((CONV_CH, d))],
        out_specs=(pl.BlockSpec((tm, d), row), pl.BlockSpec((tm, d), row),
                   pl.BlockSpec((tm, 1, d // 2), lambda i: (i, 0, 0))),
        compiler_params=_cparams(("arbitrary",)),
        name="outproj",
    )(on, cy, x, mod, mod, mod, g2, wo_a, wo_c)


def _first_index_of_max(x, idx, axis):
    m = jnp.max(x, axis=axis, keepdims=True)
    return m, jnp.min(jnp.where(x == m, idx, float(N_EXPERTS)), axis=axis, keepdims=True)


def _route_kernel(hp_ref, hs_ref, wrh_ref, wrl_ref, bias_ref, tri_ref, idx_ref, wts_ref, pos_ref, cnt_ref, run_scr,
                  *, n_prompt_tiles):
    i = pl.program_id(0)

    @pl.when(i == 0)
    def _():
        run_scr[...] = jnp.zeros(run_scr.shape, F32)

    h = jnp.where(i < n_prompt_tiles, hp_ref[...], hs_ref[...])
    h_hi = h.astype(BF16)
    h_lo = (h - h_hi.astype(F32)).astype(BF16)
    nt = (((1,), (1,)), ((), ()))
    logits = (lax.dot_general(wrh_ref[...], h_hi, nt, preferred_element_type=F32)
              + lax.dot_general(wrh_ref[...], h_lo, nt, preferred_element_type=F32)
              + lax.dot_general(wrl_ref[...], h_hi, nt, preferred_element_type=F32))
    s = jax.nn.sigmoid(logits)
    sc = s + bias_ref[...]
    tm = s.shape[1]
    neg = -jnp.inf

    sc3 = sc.reshape(N_EXPERT_GROUPS, GROUP_SIZE, tm)
    in_grp = lax.broadcasted_iota(I32, sc3.shape, 1).astype(F32)
    m1, i1 = _first_index_of_max(sc3, in_grp, 1)
    m2 = jnp.max(jnp.where(in_grp == i1, neg, sc3), axis=1, keepdims=True)
    gs = (m1 + m2).reshape(N_EXPERT_GROUPS, tm)

    gidx = lax.broadcasted_iota(I32, gs.shape, 0).astype(F32)
    gsel = jnp.zeros(gs.shape, F32)
    for _ in range(TOPK_GROUPS):
        _, gi = _first_index_of_max(gs, gidx, 0)
        hit = gidx == gi
        gsel = jnp.where(hit, 1.0, gsel)
        gs = jnp.where(hit, neg, gs)
    emask = jnp.broadcast_to(gsel[:, None, :], sc3.shape).reshape(N_EXPERTS, tm)

    cand = jnp.where(emask > 0.0, sc, neg)
    eidx = lax.broadcasted_iota(I32, cand.shape, 0).astype(F32)
    onehot = jnp.zeros(cand.shape, F32)
    hits, idxs, ws = [], [], []
    for _ in range(TOP_K):
        _, ei = _first_index_of_max(cand, eidx, 0)
        hit = eidx == ei
        hits.append(hit)
        idxs.append(ei)
        ws.append(jnp.sum(jnp.where(hit, s, 0.0), axis=0, keepdims=True))
        cand = jnp.where(hit, neg, cand)
        onehot = jnp.where(hit, 1.0, onehot)
    w = jnp.concatenate(ws, axis=0)
    wts_ref[...] = w / jnp.sum(w, axis=0, keepdims=True) * ROUTED_SCALE
    idx_ref[...] = jnp.concatenate(idxs, axis=0).astype(I32)

    rank = run_scr[...] + jnp.dot(onehot.astype(BF16), tri_ref[...], preferred_element_type=F32)
    pos = [jnp.sum(jnp.where(hit, rank, 0.0), axis=0, keepdims=True) for hit in hits]
    pos_ref[...] = jnp.concatenate(pos, axis=0).astype(I32)
    run_scr[...] = run_scr[...] + jnp.sum(onehot, axis=1, keepdims=True)
    cnt_ref[...] = run_scr[...].astype(I32)


def _route(hn_p, hn_s, wr_hi, wr_lo, bias_col, tri, tm):
    (tp, d), ts = hn_p.shape, hn_s.shape[0]
    t = tp + ts
    npt, nst = tp // tm, ts // tm
    const = lambda shape: pl.BlockSpec(shape, lambda i: (0, 0))
    tok = pl.BlockSpec((TOP_K, tm), lambda i: (0, i))
    return pl.pallas_call(
        functools.partial(_route_kernel, n_prompt_tiles=npt),
        out_shape=(jax.ShapeDtypeStruct((TOP_K, t), I32), jax.ShapeDtypeStruct((TOP_K, t), F32),
                   jax.ShapeDtypeStruct((TOP_K, t), I32), jax.ShapeDtypeStruct((N_EXPERTS, 1), I32)),
        grid=(npt + nst,),
        in_specs=[pl.BlockSpec((tm, d), lambda i: (jnp.minimum(i, npt - 1), 0)),
                  pl.BlockSpec((tm, d), lambda i: (jnp.clip(i - npt, 0, nst - 1), 0)),
                  const((N_EXPERTS, d)), const((N_EXPERTS, d)), const((N_EXPERTS, 1)), const((tm, tm))],
        out_specs=(tok, tok, tok, const((N_EXPERTS, 1))),
        scratch_shapes=[pltpu.VMEM((N_EXPERTS, 1), F32)],
        compiler_params=_cparams(("arbitrary",)),
        name="route",
    )(hn_p, hn_s, wr_hi, wr_lo, bias_col, tri)


HI_HALF = -65536


def _pack_bf16_pairs(x):
    w = x.shape[1] // 2
    bits = lambda a: lax.bitcast_convert_type(a.astype(BF16).astype(F32), I32)
    return lax.shift_right_logical(bits(x[:, :w]), 16) | (bits(x[:, w:]) & HI_HALF)


def _unpack_bf16_pairs(u):
    lo = lax.bitcast_convert_type(lax.shift_left(u, 16), F32)
    hi = lax.bitcast_convert_type(u & HI_HALF, F32)
    return jnp.concatenate([lo, hi], axis=1)


def _wait_rows(src, dst, sem, n):
    pltpu.make_async_copy(src.at[pl.ds(0, n)], dst.at[pl.ds(0, n)], sem).wait()


def _dispatch_kernel(lo_ref, n_ref, nu_ref, dest_ref, hp_ref, hs_ref, xs_hbm, rowbuf, zbuf, sem, zsem, *, tm,
                     n_prompt_tiles, rows):
    i = pl.program_id(0)
    n = pl.num_programs(0)
    par = i % 2
    n_blocks = xs_hbm.shape[0] // rows
    base = pl.multiple_of(par * tm, tm)

    @pl.when(i < n_prompt_tiles)
    def _():
        rowbuf[pl.ds(base, tm)] = hp_ref[...]

    @pl.when(i >= n_prompt_tiles)
    def _():
        rowbuf[pl.ds(base, tm)] = hs_ref[...]

    def issue(t, carry):
        for k in range(TOP_K):
            slot = dest_ref[0, 0, k * tm + t]
            pltpu.make_async_copy(rowbuf.at[pl.ds(base + t, 1)], xs_hbm.at[pl.ds(slot, 1)],
                                  sem.at[par]).start(priority=k % 2)
        return carry

    lax.fori_loop(0, tm, issue, 0)

    def tail_copy(e):
        return pltpu.make_async_copy(zbuf.at[pl.ds(0, n_ref[e])], xs_hbm.at[pl.ds(lo_ref[e], n_ref[e])],
                                     zsem.at[0])

    def block_copy(b):
        return pltpu.make_async_copy(zbuf, xs_hbm.at[pl.ds(b * rows, rows)], zsem.at[0])

    def for_each_fill(act):
        def tail(e, carry):
            @pl.when(n_ref[e] > 0)
            def _():
                act(tail_copy(e))
            return carry

        def block(b, carry):
            act(block_copy(b))
            return carry

        lax.fori_loop(0, N_EXPERTS, tail, 0)
        lax.fori_loop(nu_ref[0], n_blocks, block, 0)

    @pl.when(i == 0)
    def _():
        zbuf[...] = jnp.zeros(zbuf.shape, zbuf.dtype)
        for_each_fill(lambda c: c.start())

    def wait_tile(p):
        for _ in range(TOP_K):
            _wait_rows(rowbuf, xs_hbm, sem.at[p], tm)

    @pl.when(i >= 1)
    def _():
        wait_tile(1 - par)

    @pl.when(i == n - 1)
    def _():
        wait_tile(par)
        for_each_fill(lambda c: c.wait())


def _dispatch(pad_lo, pad_n, n_used, dest_tiles, hnp_p, hnp_s, n_slots, rows, tm):
    n_tiles = dest_tiles.shape[0]
    w = hnp_p.shape[2]
    npt, nst = hnp_p.shape[0] // tm, hnp_s.shape[0] // tm
    grid_spec = pltpu.PrefetchScalarGridSpec(
        num_scalar_prefetch=3,
        grid=(n_tiles,),
        in_specs=[pl.BlockSpec((1, 1, TOP_K * tm), lambda i, lo, n, nu: (i, 0, 0), memory_space=pltpu.SMEM),
                  pl.BlockSpec((tm, 1, w), lambda i, lo, n, nu: (jnp.minimum(i, npt - 1), 0, 0)),
                  pl.BlockSpec((tm, 1, w), lambda i, lo, n, nu: (jnp.clip(i - npt, 0, nst - 1), 0, 0))],
        out_specs=pl.BlockSpec(memory_space=pl.ANY),
        scratch_shapes=[pltpu.VMEM((2 * tm, 1, w), I32), pltpu.VMEM((rows, 1, w), I32),
                        pltpu.SemaphoreType.DMA((2,)), pltpu.SemaphoreType.DMA((1,))],
    )
    return pl.pallas_call(
        functools.partial(_dispatch_kernel, tm=tm, n_prompt_tiles=hnp_p.shape[0] // tm, rows=rows),
        out_shape=jax.ShapeDtypeStruct((n_slots, 1, w), I32),
        grid_spec=grid_spec,
        compiler_params=_cparams(("arbitrary",)),
        name="dispatch",
    )(pad_lo, pad_n, n_used, dest_tiles, hnp_p, hnp_s)


def _dot_by_columns(x, w_ref, tn):
    cols = [jnp.dot(x, w_ref[:, c:c + tn].astype(BF16), preferred_element_type=F32)
            for c in range(0, w_ref.shape[1], tn)]
    return jnp.concatenate(cols, axis=1)


def _expert_kernel(be_ref, nu_ref, xs_ref, wg_ref, wu_ref, wd_ref, ys_ref, x2d):
    b = pl.program_id(0)

    @pl.when(b < nu_ref[0])
    def _():
        x2d[...] = xs_ref[...].reshape(x2d.shape)
        x = _unpack_bf16_pairs(x2d[...]).astype(BF16)
        g = _dot_by_columns(x, wg_ref, MXU_COLS)
        u = _dot_by_columns(x, wu_ref, MXU_COLS)
        hmid = (_silu(g) * u).astype(BF16)
        y = _dot_by_columns(hmid, wd_ref, MXU_COLS)
        ys_ref[...] = _pack_bf16_pairs(y).reshape(ys_ref.shape)


def _experts(block_e, n_used, xs, w_gate, w_up, w_down, layer, rows):
    n_slots, _, w = xs.shape
    nblk = n_slots // rows
    d = 2 * w
    used = lambda b, nu: jnp.minimum(b, nu[0] - 1)
    wspec = lambda shape: pl.BlockSpec((None, None) + shape, lambda b, be, nu: (layer, be[used(b, nu)], 0, 0))
    slot_spec = pl.BlockSpec((rows, 1, w), lambda b, be, nu: (used(b, nu), 0, 0))
    grid_spec = pltpu.PrefetchScalarGridSpec(
        num_scalar_prefetch=2,
        grid=(nblk,),
        in_specs=[slot_spec, wspec((d, EXPERT_FF)), wspec((d, EXPERT_FF)), wspec((EXPERT_FF, d))],
        out_specs=slot_spec,
        scratch_shapes=[pltpu.VMEM((rows, w), I32)],
    )
    return pl.pallas_call(
        _expert_kernel,
        out_shape=jax.ShapeDtypeStruct((n_slots, 1, w), I32),
        grid_spec=grid_spec,
        input_output_aliases={2: 0},
        compiler_params=_cparams(("arbitrary",)),
        name="experts",
    )(block_e, n_used, xs, w_gate, w_up, w_down)


def _slot_plan(idx_t, pos_t, counts, rows, n_blocks):
    nblk_e = (counts + rows - 1) // rows
    blk_end = jnp.cumsum(nblk_e)
    blk_start = blk_end - nblk_e
    is_e = idx_t[:, :, None] == jnp.arange(N_EXPERTS, dtype=I32)[None, None, :]
    dest = jnp.sum(jnp.where(is_e, (blk_start * rows)[None, None, :], 0), axis=-1) + pos_t
    blocks = jnp.arange(n_blocks, dtype=I32)
    block_e = jnp.minimum(jnp.sum((blk_end[None, :] <= blocks[:, None]).astype(I32), axis=1), N_EXPERTS - 1)
    pad_lo = (blk_start * rows + counts).astype(I32)
    pad_n = (nblk_e * rows - counts).astype(I32)
    return dest.astype(I32), block_e, blk_end[-1:].astype(I32), pad_lo, pad_n


def _dest_tiles(dest, tm):
    t = dest.shape[1]
    return dest.reshape(TOP_K, t // tm, tm).transpose(1, 0, 2).reshape(t // tm, 1, TOP_K * tm)


def _combine_kernel(dest_ref, ndest_ref, wt_ref, hn_ref, x1_ref, gt_ref, wg_ref, wu_ref, wd_ref, ys_hbm, o_ref,
                    ybuf, y2d, sem, *, per_row, tm):
    i = pl.program_id(0)
    n = pl.num_programs(0)
    par = i % 2
    per_tile = TOP_K * tm

    def gather(tbl_ref, tile_slot):
        def body(t, carry):
            for k in range(TOP_K):
                slot = tbl_ref[0, 0, k * tm + t]
                pltpu.make_async_copy(ys_hbm.at[pl.ds(slot, 1)],
                                      ybuf.at[pl.ds(tile_slot * per_tile + k * tm + t, 1)],
                                      sem.at[tile_slot]).start(priority=k % 2)
            return carry
        lax.fori_loop(0, tm, body, 0)

    @pl.when(i == 0)
    def _():
        gather(dest_ref, 0)

    h = hn_ref[...].astype(BF16)
    g = jnp.dot(h, wg_ref[...], preferred_element_type=F32)
    u = jnp.dot(h, wu_ref[...], preferred_element_type=F32)
    gt = _mod_rows(gt_ref, per_row)
    o_ref[...] = x1_ref[...] + gt * jnp.dot((_silu(g) * u).astype(BF16), wd_ref[...], preferred_element_type=F32)

    _wait_rows(ys_hbm, ybuf.at[pl.ds(par * per_tile, per_tile)], sem.at[par], per_tile)
    wt = wt_ref[...]
    rc = y2d.shape[1]
    for c in range(tm // rc):
        for t in range(c * rc, (c + 1) * rc):
            for k in range(TOP_K):
                slot = ndest_ref[0, 0, k * tm + t]
                pltpu.make_async_copy(ys_hbm.at[pl.ds(slot, 1)],
                                      ybuf.at[pl.ds((1 - par) * per_tile + k * tm + t, 1)],
                                      sem.at[1 - par]).start(priority=k % 2)
        acc = jnp.zeros((rc, o_ref.shape[1]), F32)
        for k in range(TOP_K):
            y2d[k] = ybuf[pl.ds(pl.multiple_of(par * per_tile + k * tm + c * rc, rc), rc)].reshape(y2d.shape[1:])
            acc = acc + wt[c * rc:(c + 1) * rc, k:k + 1] * _unpack_bf16_pairs(y2d[k])
        rows = slice(c * rc, (c + 1) * rc)
        o_ref[rows, :] = o_ref[rows, :] + (gt if gt.shape[0] == 1 else gt[rows]) * acc

    @pl.when(i == n - 1)
    def _():
        _wait_rows(ys_hbm, ybuf.at[pl.ds((1 - par) * per_tile, per_tile)], sem.at[1 - par], per_tile)


def _combine(ys, dest_tiles, wts, hn, x1, mod, mod_block_rows, mod_row_block, per_row, ws_g, ws_u, ws_d, tm):
    t, d = hn.shape
    w = ys.shape[2]
    n_tiles = t // tm
    row = lambda i: (i, 0)
    const = lambda shape: pl.BlockSpec(shape, lambda i: (0, 0))
    gt_spec = pl.BlockSpec((mod_block_rows, d), (lambda i: (i, 5)) if per_row else (lambda i: (mod_row_block, 5)))
    tile = lambda f: pl.BlockSpec((1, 1, TOP_K * tm), lambda i: (f(i), 0, 0), memory_space=pltpu.SMEM)
    return pl.pallas_call(
        functools.partial(_combine_kernel, per_row=per_row, tm=tm),
        out_shape=jax.ShapeDtypeStruct((t, d), F32),
        grid=(n_tiles,),
        in_specs=[tile(lambda i: i), tile(lambda i: jnp.minimum(i + 1, n_tiles - 1)),
                  pl.BlockSpec((tm, TOP_K), row), pl.BlockSpec((tm, d), row), pl.BlockSpec((tm, d), row),
                  gt_spec, const((d, EXPERT_FF)), const((d, EXPERT_FF)), const((EXPERT_FF, d)),
                  pl.BlockSpec(memory_space=pl.ANY)],
        out_specs=pl.BlockSpec((tm, d), row),
        scratch_shapes=[pltpu.VMEM((2 * TOP_K * tm, 1, w), I32), pltpu.VMEM((TOP_K, COMBINE_ROWS, w), I32),
                        pltpu.SemaphoreType.DMA((2,))],
        compiler_params=_cparams(("arbitrary",)),
        name="combine",
    )(dest_tiles, dest_tiles, wts, hn, x1, mod, ws_g, ws_u, ws_d, ys)


def _rope_tables(pos):
    half = HEAD_DIM_QK // 2
    inv = ROPE_THETA ** (-jnp.arange(half, dtype=F32) * 2.0 / HEAD_DIM_QK)
    ang = pos.astype(F32)[:, None] * inv[None, :]
    cos, sin = jnp.cos(ang), jnp.sin(ang)
    return jnp.tile(jnp.concatenate([cos, cos], axis=1), (1, 2)), jnp.tile(jnp.concatenate([-sin, sin], axis=1), (1, 2))


def _n_expert_blocks(n_assign, rows):
    return -(-(n_assign + N_EXPERTS * (rows - 1)) // rows)


def kernel(x_prompt, x_sample, c_prompt, c_sample, cache_k, cache_v, state_conv, page_table, w_ada, b_ada, g_norm1, g_norm2, w_in, g_q, g_k, lambda_q1, lambda_k1, lambda_q2, lambda_k2, g_sub, w_dw, b_dw, ln_g, ln_b, w_out, w_router, router_bias, w_e_gate, w_e_up, w_e_down, w_s_gate, w_s_up, w_s_down):
    layer = 0
    lam_init = 0.8 - 0.6 * math.exp(-0.3 * layer)
    d = D_MODEL
    tp = x_prompt.shape[1]
    ns = x_sample.shape[0]
    t_all = tp + ns
    xp = x_prompt.reshape(tp, d)
    xs = x_sample.reshape(ns, d)

    row2 = lambda a: a[layer].reshape(1, -1)
    w_in_bf = w_in[layer].astype(BF16)
    wo_a = w_out[layer, :ATTN_WIDTH].astype(BF16)
    wo_c = w_out[layer, ATTN_WIDTH:].astype(BF16)
    ws_g, ws_u, ws_d = (w[layer].astype(BF16) for w in (w_s_gate, w_s_up, w_s_down))
    wr_t = w_router[layer].T
    wr_hi = wr_t.astype(BF16)
    wr_lo = (wr_t - wr_hi.astype(F32)).astype(BF16)
    gq = jnp.tile(row2(g_q), (1, 2))
    gk = jnp.tile(row2(g_k), (1, 2))
    lane = jnp.arange(LANES)
    bd = (lane[:, None] // HEAD_DIM_QK == lane[None, :] // HEAD_DIM_QK).astype(BF16)
    lams = [row2(a) for a in (lambda_q1, lambda_k1, lambda_q2, lambda_k2)]

    c_all = jnp.concatenate([c_sample, c_prompt, jnp.zeros((SUBLANES - 1, d), F32)], axis=0)
    mod = _ada(c_all, w_ada[layer], row2(b_ada))
    p_mod = dict(mod_block_rows=SUBLANES, mod_row_block=ns // SUBLANES, per_row=False)
    s_mod = dict(mod_block_rows=ns, mod_row_block=0, per_row=True)

    cos_p, sin_p = _rope_tables(jnp.arange(tp, dtype=I32))
    cos_s, sin_s = _rope_tables(jnp.full((ns,), PAST_LEN, I32))

    q_p, k_p, kb_p, v_p, vb_p, glu_p = _inproj(xp, mod, g1=row2(g_norm1), w_in_bf=w_in_bf, gq=gq, gk=gk,
                                               cos=cos_p, sin=sin_p, bd=bd, tm=512, **p_mod)
    cy_p = _conv_prompt(glu_p, w_dw[layer], row2(b_dw), row2(ln_g), row2(ln_b))
    on_p = _flash_prompt(q_p, kb_p, vb_p, *lams, row2(g_sub), lam_init)

    q_s, k_s, _, v_s, _, glu_s = _inproj(xs, mod, g1=row2(g_norm1), w_in_bf=w_in_bf, gq=gq, gk=gk,
                                         cos=cos_s, sin=sin_s, bd=bd, tm=ns, **s_mod)
    cy_s = _conv_sample(state_conv[layer], glu_s, w_dw[layer], row2(b_dw), row2(ln_g), row2(ln_b))
    n_phys = cache_k.shape[1]
    ck_pages = jnp.transpose(cache_k, (0, 1, 3, 4, 5, 2)).reshape(-1, ATTN_WIDTH, PAGE_SIZE)
    cv_pages = cache_v.reshape(-1, PAGE_SIZE * N_HEADS, HEAD_DIM_V)
    to_cols = lambda a: jnp.pad(jnp.swapaxes(a.reshape(ns, 2 * N_HEADS, HEAD_DIM_QK), 1, 2),
                                ((0, 0), (0, 0), (0, LANES - 2 * N_HEADS)))
    qt_s = to_cols(q_s[0].astype(F32) + q_s[1].astype(F32))
    on_s = _attn_sample(page_table + layer * n_phys, qt_s, to_cols(k_s), v_s.reshape(ns, N_HEADS, HEAD_DIM_V),
                        *lams, row2(g_sub), ck_pages, cv_pages, lam_init).reshape(ns, ATTN_WIDTH)

    x1_p, hn_p, hnp_p = _outproj(on_p, cy_p, xp, mod, g2=row2(g_norm2), wo_a=wo_a, wo_c=wo_c, tm=256, **p_mod)
    x1_s, hn_s, hnp_s = _outproj(on_s, cy_s, xs, mod, g2=row2(g_norm2), wo_a=wo_a, wo_c=wo_c, tm=ns, **s_mod)

    tm_r = ns
    tri = (jnp.arange(tm_r)[:, None] < jnp.arange(tm_r)[None, :]).astype(BF16)
    idx_t, wts_t, pos_t, counts = _route(hn_p, hn_s, wr_hi, wr_lo, router_bias[layer].reshape(-1, 1), tri, tm_r)
    n_blocks = _n_expert_blocks(TOP_K * t_all, EXPERT_ROWS)
    dest, block_e, n_used, pad_lo, pad_n = _slot_plan(idx_t, pos_t, counts.reshape(-1), EXPERT_ROWS, n_blocks)

    slots_x = _dispatch(pad_lo, pad_n, n_used, _dest_tiles(dest, tm_r), hnp_p, hnp_s, n_blocks * EXPERT_ROWS,
                        EXPERT_ROWS, tm_r)
    slots_y = _experts(block_e, n_used, slots_x, w_e_gate, w_e_up, w_e_down, layer, EXPERT_ROWS)
    tm_c = 256
    y_p = _combine(slots_y, _dest_tiles(dest[:, :tp], tm_c), wts_t[:, :tp].T, hn_p, x1_p, mod,
                   ws_g=ws_g, ws_u=ws_u, ws_d=ws_d, tm=tm_c, **p_mod)
    y_s = _combine(slots_y, _dest_tiles(dest[:, tp:], ns), wts_t[:, tp:].T, hn_s, x1_s, mod,
                   ws_g=ws_g, ws_u=ws_u, ws_d=ws_d, tm=ns, **s_mod)

    conv_p = glu_p[tp - CONV_STATE:]
    conv_s = jnp.concatenate([state_conv[layer][:, 1:], glu_s[:, None, :]], axis=1)
    return (y_p.reshape(1, tp, d), y_s.reshape(ns, 1, d),
            k_p.reshape(1, 1, tp, N_HEADS, 2, HEAD_DIM_QK), v_p.reshape(1, 1, tp, N_HEADS, HEAD_DIM_V),
            conv_p.reshape(1, 1, CONV_STATE, CONV_CH),
            k_s.reshape(1, ns, 1, N_HEADS, 2, HEAD_DIM_QK), v_s.reshape(1, ns, 1, N_HEADS, HEAD_DIM_V),
            conv_s.reshape(1, ns, CONV_STATE, CONV_CH))
```

```python
import functools
import math

import jax
import jax.numpy as jnp
from jax import lax
from jax.experimental import pallas as pl
from jax.experimental.pallas import tpu as pltpu

F32 = jnp.float32
BF16 = jnp.bfloat16
I32 = jnp.int32

D_MODEL = 2048
ATTN_WIDTH = 1024
CONV_CH = 1024
N_HEADS = 8
HEAD_DIM_V = 128
HEAD_DIM_QK = 64
CONV_LEN = 31
CONV_STATE = CONV_LEN - 1
N_EXPERTS = 256
TOP_K = 8
N_EXPERT_GROUPS = 8
GROUP_SIZE = N_EXPERTS // N_EXPERT_GROUPS
TOPK_GROUPS = 4
EXPERT_FF = 512
ROUTED_SCALE = 2.5
ROPE_THETA = 10000.0
NORM_EPS = 1e-6
PAST_LEN = 2048
PAGE_SIZE = 128
IN_COLS = 3 * ATTN_WIDTH + 2 * CONV_CH
LANES = 128
SUBLANES = 8
COL_TILE = 1024
MXU_COLS = 256
CONV_HALO = 32
EXPERT_ROWS = 320
COMBINE_ROWS = 32
VMEM_LIMIT = 56 * 1024 * 1024
Q_SCALE = HEAD_DIM_QK ** -0.5 * math.log2(math.e)


def _cparams(sem, vmem=VMEM_LIMIT):
    return pltpu.CompilerParams(dimension_semantics=sem, vmem_limit_bytes=vmem)


def _rms(x, eps=NORM_EPS):
    return x * lax.rsqrt(jnp.mean(x * x, axis=-1, keepdims=True) + eps)


def _silu(x):
    return x * jax.nn.sigmoid(x)


def _mod_rows(ref, per_row):
    return ref[...] if per_row else ref[0:1, :]


def _diff_lambda(lq1_ref, lk1_ref, lq2_ref, lk2_ref, lam_init):
    a = jnp.sum(lq1_ref[...] * lk1_ref[...], axis=-1, keepdims=True)
    b = jnp.sum(lq2_ref[...] * lk2_ref[...], axis=-1, keepdims=True)
    return jnp.exp(a) - jnp.exp(b) + lam_init


def _ada_kernel(c_ref, w_ref, b_ref, o_ref):
    s = _silu(c_ref[...]).astype(BF16)
    o_ref[...] = jnp.dot(s, w_ref[...].astype(BF16), preferred_element_type=F32) + b_ref[...]


def _ada(c_all, w_ada, b_ada):
    rows, d = c_all.shape
    n = w_ada.shape[1]
    tn = 1024
    return pl.pallas_call(
        _ada_kernel,
        out_shape=jax.ShapeDtypeStruct((rows, n), F32),
        grid=(n // tn,),
        in_specs=[pl.BlockSpec((rows, d), lambda j: (0, 0)),
                  pl.BlockSpec((d, tn), lambda j: (0, j)),
                  pl.BlockSpec((1, tn), lambda j: (0, j))],
        out_specs=pl.BlockSpec((rows, tn), lambda j: (0, j)),
        compiler_params=_cparams(("arbitrary",)),
        name="adaln",
    )(c_all, w_ada, b_ada)


def _qk_norm_rope(u, g, cos, sin, bd):
    lane = lax.broadcasted_iota(I32, (u.shape[0], LANES), 1)
    first_half = (lane & 32) == 0
    outs = []
    for c in range(u.shape[1] // LANES):
        x = u[:, c * LANES:(c + 1) * LANES]
        ss = jnp.dot((x * x).astype(BF16), bd, preferred_element_type=F32)
        xn = x * lax.rsqrt(ss * (1.0 / HEAD_DIM_QK) + NORM_EPS) * g
        rot = jnp.where(first_half, pltpu.roll(xn, 96, 1), pltpu.roll(xn, 32, 1))
        outs.append(xn * cos + rot * sin)
    return jnp.concatenate(outs, axis=1)


def _inproj_kernel(x_ref, sh_ref, sc_ref, g1_ref, w_ref, gq_ref, gk_ref, cos_ref, sin_ref, bd_ref,
                   q_ref, k_ref, kb_ref, v_ref, vb_ref, glu_ref, xn_scr, val_scr, *, per_row):
    j = pl.program_id(1)

    @pl.when(j == 0)
    def _():
        y = _rms(x_ref[...]) * g1_ref[...]
        xn_scr[...] = (y * (1.0 + _mod_rows(sc_ref, per_row)) + _mod_rows(sh_ref, per_row)).astype(BF16)

    u = jnp.dot(xn_scr[...], w_ref[...], preferred_element_type=F32)

    @pl.when(j == 0)
    def _():
        q = _qk_norm_rope(u, gq_ref[...], cos_ref[...], sin_ref[...], bd_ref[...]) * Q_SCALE
        lane = lax.broadcasted_iota(I32, q.shape, 1)
        map0 = (lane & HEAD_DIM_QK) == 0
        q_ref[0] = jnp.where(map0, q, 0.0).astype(BF16)
        q_ref[1] = jnp.where(map0, 0.0, q).astype(BF16)

    @pl.when(j == 1)
    def _():
        k = _qk_norm_rope(u, gk_ref[...], cos_ref[...], sin_ref[...], bd_ref[...])
        k_ref[...] = k
        kb_ref[...] = k.astype(BF16)

    @pl.when(j == 2)
    def _():
        v_ref[...] = u
        vb_ref[...] = u.astype(BF16)

    @pl.when(j == 3)
    def _():
        val_scr[...] = u

    @pl.when(j == 4)
    def _():
        glu_ref[...] = val_scr[...] * jax.nn.sigmoid(u)


def _inproj(x, mod, mod_block_rows, mod_row_block, per_row, g1, w_in_bf, gq, gk, cos, sin, bd, tm):
    t, d = x.shape
    nj = IN_COLS // COL_TILE
    row = lambda i, j: (i, 0)
    mod_spec = lambda chunk: pl.BlockSpec(
        (mod_block_rows, d), (lambda i, j: (i, chunk)) if per_row else (lambda i, j: (mod_row_block, chunk)))
    const = lambda shape: pl.BlockSpec(shape, lambda i, j: (0, 0))
    return pl.pallas_call(
        functools.partial(_inproj_kernel, per_row=per_row),
        out_shape=(jax.ShapeDtypeStruct((2, t, COL_TILE), BF16),
                   jax.ShapeDtypeStruct((t, COL_TILE), F32), jax.ShapeDtypeStruct((t, COL_TILE), BF16),
                   jax.ShapeDtypeStruct((t, COL_TILE), F32), jax.ShapeDtypeStruct((t, COL_TILE), BF16),
                   jax.ShapeDtypeStruct((t, COL_TILE), F32)),
        grid=(t // tm, nj),
        in_specs=[pl.BlockSpec((tm, d), row), mod_spec(0), mod_spec(1), const((1, d)),
                  pl.BlockSpec((d, COL_TILE), lambda i, j: (0, j)),
                  const((1, LANES)), const((1, LANES)),
                  pl.BlockSpec((tm, LANES), row), pl.BlockSpec((tm, LANES), row), const((LANES, LANES))],
        out_specs=(pl.BlockSpec((2, tm, COL_TILE), lambda i, j: (0, i, 0)),
                   pl.BlockSpec((tm, COL_TILE), row), pl.BlockSpec((tm, COL_TILE), row),
                   pl.BlockSpec((tm, COL_TILE), row), pl.BlockSpec((tm, COL_TILE), row),
                   pl.BlockSpec((tm, COL_TILE), row)),
        scratch_shapes=[pltpu.VMEM((tm, d), BF16), pltpu.VMEM((tm, COL_TILE), F32)],
        compiler_params=_cparams(("arbitrary", "arbitrary")),
        name="inproj",
    )(x, mod, mod, g1, w_in_bf, gq, gk, cos, sin, bd)


def _layernorm_swish(y, g, b):
    mu = jnp.mean(y, axis=-1, keepdims=True)
    yc = y - mu
    var = jnp.mean(yc * yc, axis=-1, keepdims=True)
    yn = yc * lax.rsqrt(var + NORM_EPS) * g + b
    return yn * jax.nn.sigmoid(yn)


def _conv_kernel(cur_ref, prev_ref, w_ref, b_ref, lg_ref, lb_ref, y_ref, xs_scr, sh_scr, *, tm, rc):
    i = pl.program_id(0)
    xs_scr[0:CONV_HALO, :] = jnp.where(i == 0, 0.0, prev_ref[...])
    xs_scr[CONV_HALO:, :] = cur_ref[...]
    n_sh = tm + CONV_HALO - SUBLANES
    for b in range(1, SUBLANES):
        sh_scr[b - 1] = xs_scr[b:b + n_sh, :]

    first = CONV_HALO - CONV_STATE

    def chunk(r, carry):
        r0 = pl.multiple_of(r * rc, rc)
        acc = jnp.zeros((rc, CONV_CH), F32)
        for j in range(CONV_LEN):
            o = first + j
            b, a = o % SUBLANES, (o // SUBLANES) * SUBLANES
            src = xs_scr if b == 0 else sh_scr.at[b - 1]
            acc = acc + w_ref[j:j + 1, :] * src[pl.ds(r0 + a, rc), :]
        y = _layernorm_swish(acc + b_ref[...], lg_ref[...], lb_ref[...])
        y_ref[pl.ds(r0, rc), :] = y.astype(y_ref.dtype)
        return carry

    lax.fori_loop(0, tm // rc, chunk, 0)


def _conv_prompt(glu, w_dw, b_dw, ln_g, ln_b, tm=256, rc=32):
    t, c = glu.shape
    per = tm // CONV_HALO
    const = lambda shape: pl.BlockSpec(shape, lambda i: (0, 0))
    return pl.pallas_call(
        functools.partial(_conv_kernel, tm=tm, rc=rc),
        out_shape=jax.ShapeDtypeStruct((t, c), BF16),
        grid=(t // tm,),
        in_specs=[pl.BlockSpec((tm, c), lambda i: (i, 0)),
                  pl.BlockSpec((CONV_HALO, c), lambda i: (jnp.maximum(i * per - 1, 0), 0)),
                  const((CONV_LEN, c)), const((1, c)), const((1, c)), const((1, c))],
        out_specs=pl.BlockSpec((tm, c), lambda i: (i, 0)),
        scratch_shapes=[pltpu.VMEM((tm + CONV_HALO, c), F32),
                        pltpu.VMEM((SUBLANES - 1, tm + CONV_HALO - SUBLANES, c), F32)],
        compiler_params=_cparams(("arbitrary",)),
        name="conv_prompt",
    )(glu, glu, w_dw, b_dw, ln_g, ln_b)


def _conv_sample_kernel(st_ref, glu_ref, w_ref, b_ref, lg_ref, lb_ref, y_ref):
    w = w_ref[...]
    acc = jnp.sum(st_ref[...] * w[None, 0:CONV_STATE, :], axis=1) + glu_ref[...] * w[CONV_STATE:CONV_LEN, :]
    y_ref[...] = _layernorm_swish(acc + b_ref[...], lg_ref[...], lb_ref[...]).astype(y_ref.dtype)


def _conv_sample(state, glu, w_dw, b_dw, ln_g, ln_b, sb=32):
    b, s, c = state.shape
    const = lambda shape: pl.BlockSpec(shape, lambda i: (0, 0))
    return pl.pallas_call(
        _conv_sample_kernel,
        out_shape=jax.ShapeDtypeStruct((b, c), BF16),
        grid=(b // sb,),
        in_specs=[pl.BlockSpec((sb, s, c), lambda i: (i, 0, 0)), pl.BlockSpec((sb, c), lambda i: (i, 0)),
                  const((CONV_LEN, c)), const((1, c)), const((1, c)), const((1, c))],
        out_specs=pl.BlockSpec((sb, c), lambda i: (i, 0)),
        compiler_params=_cparams(("arbitrary",)),
        name="conv_sample",
    )(state, glu, w_dw, b_dw, ln_g, ln_b)


def _flash_kernel(q_ref, k_ref, v_ref, lq1_ref, lk1_ref, lq2_ref, lk2_ref, gs_ref, o_ref,
                  vx_scr, m_scr, acc_scr, sa_scr, sb_scr, *, tq, tk, lam_init):
    i = pl.program_id(1)
    dv = HEAD_DIM_V

    @pl.when(i == 0)
    def _():
        vx_scr[:, 0:dv] = v_ref[...]
        vx_scr[:, dv:2 * dv] = jnp.ones((vx_scr.shape[0], dv), BF16)

    q = q_ref[...].reshape(2 * tq, LANES)
    m_scr[...] = jnp.full(m_scr.shape, -jnp.inf, F32)
    acc_scr[...] = jnp.zeros(acc_scr.shape, F32)

    def scores(j, s_ref):
        k = k_ref[pl.ds(pl.multiple_of(j * tk, tk), tk), :]
        s_ref[...] = lax.dot_general(q, k, (((1,), (1,)), ((), ())), preferred_element_type=F32)

    def consume(j, s_ref, masked):
        k0 = pl.multiple_of(j * tk, tk)
        s = s_ref[...]
        if masked:
            row = lax.broadcasted_iota(I32, s.shape, 0)
            col = lax.broadcasted_iota(I32, s.shape, 1)
            qpos = i * tq + jnp.where(row >= tq, row - tq, row)
            s = jnp.where(k0 + col <= qpos, s, -jnp.inf)
        m_prev = m_scr[...]
        m_new = jnp.maximum(m_prev, jnp.max(s, axis=-1, keepdims=True))
        alpha = jnp.exp2(m_prev - m_new)
        p = jnp.exp2(s - m_new).astype(BF16)
        acc_scr[...] = alpha * acc_scr[...] + jnp.dot(p, vx_scr[pl.ds(k0, tk), :], preferred_element_type=F32)
        m_scr[...] = m_new

    n_full = (i * tq) // tk
    scores(0, sa_scr)

    def pair(t, carry):
        scores(2 * t + 1, sb_scr)
        consume(2 * t, sa_scr, False)
        scores(2 * t + 2, sa_scr)
        consume(2 * t + 1, sb_scr, False)
        return carry

    lax.fori_loop(0, n_full // 2, pair, 0)

    @pl.when(n_full % 2 == 1)
    def _():
        scores(n_full, sb_scr)
        consume(n_full - 1, sa_scr, False)
        consume(n_full, sb_scr, True)

    @pl.when(n_full % 2 == 0)
    def _():
        consume(n_full, sa_scr, True)

    acc = acc_scr[...]
    o = acc[:, 0:dv] / acc[:, dv:2 * dv]
    lam = _diff_lambda(lq1_ref, lk1_ref, lq2_ref, lk2_ref, lam_init)
    od = o[0:tq] - lam * o[tq:2 * tq]
    o_ref[...] = (_rms(od) * gs_ref[...] * (1.0 - lam_init)).astype(o_ref.dtype)


def _flash_prompt(q2, kb, vb, lq1, lk1, lq2, lk2, g_sub, lam_init, tq=512, tk=512):
    _, t, _ = q2.shape
    tk = min(tk, t)
    assert tk % tq == 0 and t % tk == 0
    const = lambda shape: pl.BlockSpec(shape, lambda h, i: (0, 0))
    return pl.pallas_call(
        functools.partial(_flash_kernel, tq=tq, tk=tk, lam_init=lam_init),
        out_shape=jax.ShapeDtypeStruct((t, ATTN_WIDTH), BF16),
        grid=(N_HEADS, t // tq),
        in_specs=[pl.BlockSpec((2, tq, LANES), lambda h, i: (0, i, h)),
                  pl.BlockSpec((t, LANES), lambda h, i: (0, h)),
                  pl.BlockSpec((t, LANES), lambda h, i: (0, h)),
                  const((1, HEAD_DIM_QK)), const((1, HEAD_DIM_QK)), const((1, HEAD_DIM_QK)),
                  const((1, HEAD_DIM_QK)), const((1, LANES))],
        out_specs=pl.BlockSpec((tq, LANES), lambda h, i: (i, h)),
        scratch_shapes=[pltpu.VMEM((t, 2 * HEAD_DIM_V), BF16), pltpu.VMEM((2 * tq, 1), F32),
                        pltpu.VMEM((2 * tq, 2 * HEAD_DIM_V), F32),
                        pltpu.VMEM((2 * tq, tk), F32), pltpu.VMEM((2 * tq, tk), F32)],
        compiler_params=_cparams(("arbitrary", "arbitrary")),
        name="flash_prompt",
    )(q2, kb, vb, lq1, lk1, lq2, lk2, g_sub)


def _paged_copies(pt_ref, ck_hbm, cv_hbm, kbuf, vbuf, sem, b, slot, n_pages):
    copies = []
    for p in range(n_pages):
        pg = pt_ref[b, p]
        copies.append(pltpu.make_async_copy(ck_hbm.at[pg], kbuf.at[slot, p], sem.at[0, slot]))
        copies.append(pltpu.make_async_copy(cv_hbm.at[pg], vbuf.at[slot, p], sem.at[1, slot]))
    return copies


def _sattn_kernel(pt_ref, qt_ref, kts_ref, vs_ref, lq1_ref, lk1_ref, lq2_ref, lk2_ref, gs_ref,
                  ck_hbm, cv_hbm, o_ref, kbuf, vbuf, qb_scr, s_scr, sem, *, n_pages, lam_init):
    b = pl.program_id(0)
    nb = pl.num_programs(0)
    slot = b % 2

    @pl.when(b == 0)
    def _():
        for c in _paged_copies(pt_ref, ck_hbm, cv_hbm, kbuf, vbuf, sem, 0, 0, n_pages):
            c.start()

    @pl.when(b + 1 < nb)
    def _():
        for c in _paged_copies(pt_ref, ck_hbm, cv_hbm, kbuf, vbuf, sem, b + 1, 1 - slot, n_pages):
            c.start()

    nr = 2 * N_HEADS
    dk = HEAD_DIM_QK
    qt = qt_ref[...]
    for r in range(nr):
        qb_scr[r] = jnp.broadcast_to(qt[:, r:r + 1], (dk, LANES))
    s_self_lanes = jnp.sum(qt * kts_ref[...], axis=0, keepdims=True)
    rr = lax.broadcasted_iota(I32, (nr, LANES), 0)
    ll = lax.broadcasted_iota(I32, (nr, LANES), 1)
    s_self = jnp.sum(jnp.where(ll == (rr % N_HEADS) * 2 + rr // N_HEADS, s_self_lanes, 0.0),
                     axis=-1, keepdims=True)

    for c in _paged_copies(pt_ref, ck_hbm, cv_hbm, kbuf, vbuf, sem, b, slot, n_pages):
        c.wait()

    for p in range(n_pages):
        for r in range(nr):
            h, c = divmod(r, 2)
            kt = kbuf[slot, p, pl.ds(r * dk, dk), :]
            row = c * N_HEADS + h
            s_scr[row:row + 1, p * PAGE_SIZE:(p + 1) * PAGE_SIZE] = jnp.sum(qb_scr[r] * kt, axis=0, keepdims=True)

    s = s_scr[...]
    m = jnp.maximum(jnp.max(s, axis=-1, keepdims=True), s_self)
    e = jnp.exp2(s - m)
    e_self = jnp.exp2(s_self - m)
    inv_l = 1.0 / (jnp.sum(e, axis=-1, keepdims=True) + e_self)
    lam = _diff_lambda(lq1_ref, lk1_ref, lq2_ref, lk2_ref, lam_init)
    w = inv_l[0:N_HEADS]
    w1 = lam * inv_l[N_HEADS:nr]
    pd = (e[0:N_HEADS] * w - e[N_HEADS:nr] * w1).astype(BF16)
    pd_self = e_self[0:N_HEADS] * w - e_self[N_HEADS:nr] * w1

    head_row = lax.broadcasted_iota(I32, (N_HEADS, HEAD_DIM_V), 0)
    o = pd_self * vs_ref[...]
    for h in range(N_HEADS):
        acc = jnp.zeros((N_HEADS, HEAD_DIM_V), F32)
        for p in range(n_pages):
            vh = vbuf[slot, p, pl.ds(h, PAGE_SIZE, stride=N_HEADS), :].astype(BF16)
            acc = acc + jnp.dot(pd[:, p * PAGE_SIZE:(p + 1) * PAGE_SIZE], vh, preferred_element_type=F32)
        o = o + jnp.where(head_row == h, acc, 0.0)
    o_ref[...] = _rms(o) * gs_ref[...] * (1.0 - lam_init)


def _attn_sample(page_table, qt, kts, vs, lq1, lk1, lq2, lk2, g_sub, ck_pages, cv_pages, lam_init):
    nb, n_pages = page_table.shape
    const2 = lambda shape: pl.BlockSpec(shape, lambda b, pt: (0, 0))
    per_seq = lambda shape: pl.BlockSpec((None,) + shape, lambda b, pt: (b, 0, 0))
    page_rows = ck_pages.shape[1]
    grid_spec = pltpu.PrefetchScalarGridSpec(
        num_scalar_prefetch=1,
        grid=(nb,),
        in_specs=[per_seq((HEAD_DIM_QK, LANES)), per_seq((HEAD_DIM_QK, LANES)), per_seq((N_HEADS, HEAD_DIM_V)),
                  const2((1, HEAD_DIM_QK)), const2((1, HEAD_DIM_QK)), const2((1, HEAD_DIM_QK)),
                  const2((1, HEAD_DIM_QK)), const2((1, HEAD_DIM_V)),
                  pl.BlockSpec(memory_space=pl.ANY), pl.BlockSpec(memory_space=pl.ANY)],
        out_specs=per_seq((N_HEADS, HEAD_DIM_V)),
        scratch_shapes=[pltpu.VMEM((2, n_pages, page_rows, LANES), F32),
                        pltpu.VMEM((2, n_pages, page_rows, LANES), F32),
                        pltpu.VMEM((2 * N_HEADS, HEAD_DIM_QK, LANES), F32),
                        pltpu.VMEM((2 * N_HEADS, n_pages * PAGE_SIZE), F32),
                        pltpu.SemaphoreType.DMA((2, 2))],
    )
    return pl.pallas_call(
        functools.partial(_sattn_kernel, n_pages=n_pages, lam_init=lam_init),
        out_shape=jax.ShapeDtypeStruct((nb, N_HEADS, HEAD_DIM_V), F32),
        grid_spec=grid_spec,
        compiler_params=_cparams(("arbitrary",)),
        name="attn_sample",
    )(page_table, qt, kts, vs, lq1, lk1, lq2, lk2, g_sub, ck_pages, cv_pages)


def _outproj_kernel(on_ref, cy_ref, x_ref, gt_ref, sh_ref, sc_ref, g2_ref, wa_ref, wc_ref, x1_ref, hn_ref, hnp_ref,
                    *, per_row):
    mix = (jnp.dot(on_ref[...].astype(BF16), wa_ref[...], preferred_element_type=F32)
           + jnp.dot(cy_ref[...], wc_ref[...], preferred_element_type=F32))
    x1 = x_ref[...] + _mod_rows(gt_ref, per_row) * mix
    x1_ref[...] = x1
    y = _rms(x1) * g2_ref[...]
    hn = y * (1.0 + _mod_rows(sc_ref, per_row)) + _mod_rows(sh_ref, per_row)
    hn_ref[...] = hn
    hnp_ref[...] = _pack_bf16_pairs(hn).reshape(hnp_ref.shape)


def _outproj(on, cy, x, mod, mod_block_rows, mod_row_block, per_row, g2, wo_a, wo_c, tm):
    t, d = x.shape
    row = lambda i: (i, 0)
    mod_spec = lambda chunk: pl.BlockSpec(
        (mod_block_rows, d), (lambda i: (i, chunk)) if per_row else (lambda i: (mod_row_block, chunk)))
    const = lambda shape: pl.BlockSpec(shape, lambda i: (0, 0))
    return pl.pallas_call(
        functools.partial(_outproj_kernel, per_row=per_row),
        out_shape=(jax.ShapeDtypeStruct((t, d), F32), jax.ShapeDtypeStruct((t, d), F32),
                   jax.ShapeDtypeStruct((t, 1, d // 2), I32)),
        grid=(t // tm,),
        in_specs=[pl.BlockSpec((tm, ATTN_WIDTH), row), pl.BlockSpec((tm, CONV_CH), row), pl.BlockSpec((tm, d), row),
                  mod_spec(2), mod_spec(3), mod_spec(4), const((1, d)),
                  const((ATTN_WIDTH, d)), const((CONV_CH, d))],
        out_specs=(pl.BlockSpec((tm, d), row), pl.BlockSpec((tm, d), row),
                   pl.BlockSpec((tm, 1, d // 2), lambda i: (i, 0, 0))),
        compiler_params=_cparams(("arbitrary",)),
        name="outproj",
    )(on, cy, x, mod, mod, mod, g2, wo_a, wo_c)


def _first_index_of_max(x, idx, axis):
    m = jnp.max(x, axis=axis, keepdims=True)
    return m, jnp.min(jnp.where(x == m, idx, float(N_EXPERTS)), axis=axis, keepdims=True)


def _route_kernel(hp_ref, hs_ref, wrh_ref, wrl_ref, bias_ref, tri_ref, idx_ref, wts_ref, pos_ref, cnt_ref, run_scr,
                  *, n_prompt_tiles):
    i = pl.program_id(0)

    @pl.when(i == 0)
    def _():
        run_scr[...] = jnp.zeros(run_scr.shape, F32)

    h = jnp.where(i < n_prompt_tiles, hp_ref[...], hs_ref[...])
    h_hi = h.astype(BF16)
    h_lo = (h - h_hi.astype(F32)).astype(BF16)
    nt = (((1,), (1,)), ((), ()))
    logits = (lax.dot_general(wrh_ref[...], h_hi, nt, preferred_element_type=F32)
              + lax.dot_general(wrh_ref[...], h_lo, nt, preferred_element_type=F32)
              + lax.dot_general(wrl_ref[...], h_hi, nt, preferred_element_type=F32))
    s = jax.nn.sigmoid(logits)
    sc = s + bias_ref[...]
    tm = s.shape[1]
    neg = -jnp.inf

    sc3 = sc.reshape(N_EXPERT_GROUPS, GROUP_SIZE, tm)
    in_grp = lax.broadcasted_iota(I32, sc3.shape, 1).astype(F32)
    m1, i1 = _first_index_of_max(sc3, in_grp, 1)
    m2 = jnp.max(jnp.where(in_grp == i1, neg, sc3), axis=1, keepdims=True)
    gs = (m1 + m2).reshape(N_EXPERT_GROUPS, tm)

    gidx = lax.broadcasted_iota(I32, gs.shape, 0).astype(F32)
    gsel = jnp.zeros(gs.shape, F32)
    for _ in range(TOPK_GROUPS):
        _, gi = _first_index_of_max(gs, gidx, 0)
        hit = gidx == gi
        gsel = jnp.where(hit, 1.0, gsel)
        gs = jnp.where(hit, neg, gs)
    emask = jnp.broadcast_to(gsel[:, None, :], sc3.shape).reshape(N_EXPERTS, tm)

    cand = jnp.where(emask > 0.0, sc, neg)
    eidx = lax.broadcasted_iota(I32, cand.shape, 0).astype(F32)
    onehot = jnp.zeros(cand.shape, F32)
    hits, idxs, ws = [], [], []
    for _ in range(TOP_K):
        _, ei = _first_index_of_max(cand, eidx, 0)
        hit = eidx == ei
        hits.append(hit)
        idxs.append(ei)
        ws.append(jnp.sum(jnp.where(hit, s, 0.0), axis=0, keepdims=True))
        cand = jnp.where(hit, neg, cand)
        onehot = jnp.where(hit, 1.0, onehot)
    w = jnp.concatenate(ws, axis=0)
    wts_ref[...] = w / jnp.sum(w, axis=0, keepdims=True) * ROUTED_SCALE
    idx_ref[...] = jnp.concatenate(idxs, axis=0).astype(I32)

    rank = run_scr[...] + jnp.dot(onehot.astype(BF16), tri_ref[...], preferred_element_type=F32)
    pos = [jnp.sum(jnp.where(hit, rank, 0.0), axis=0, keepdims=True) for hit in hits]
    pos_ref[...] = jnp.concatenate(pos, axis=0).astype(I32)
    run_scr[...] = run_scr[...] + jnp.sum(onehot, axis=1, keepdims=True)
    cnt_ref[...] = run_scr[...].astype(I32)


def _route(hn_p, hn_s, wr_hi, wr_lo, bias_col, tri, tm):
    (tp, d), ts = hn_p.shape, hn_s.shape[0]
    t = tp + ts
    npt, nst = tp // tm, ts // tm
    const = lambda shape: pl.BlockSpec(shape, lambda i: (0, 0))
    tok = pl.BlockSpec((TOP_K, tm), lambda i: (0, i))
    return pl.pallas_call(
        functools.partial(_route_kernel, n_prompt_tiles=npt),
        out_shape=(jax.ShapeDtypeStruct((TOP_K, t), I32), jax.ShapeDtypeStruct((TOP_K, t), F32),
                   jax.ShapeDtypeStruct((TOP_K, t), I32), jax.ShapeDtypeStruct((N_EXPERTS, 1), I32)),
        grid=(npt + nst,),
        in_specs=[pl.BlockSpec((tm, d), lambda i: (jnp.minimum(i, npt - 1), 0)),
                  pl.BlockSpec((tm, d), lambda i: (jnp.clip(i - npt, 0, nst - 1), 0)),
                  const((N_EXPERTS, d)), const((N_EXPERTS, d)), const((N_EXPERTS, 1)), const((tm, tm))],
        out_specs=(tok, tok, tok, const((N_EXPERTS, 1))),
        scratch_shapes=[pltpu.VMEM((N_EXPERTS, 1), F32)],
        compiler_params=_cparams(("arbitrary",)),
        name="route",
    )(hn_p, hn_s, wr_hi, wr_lo, bias_col, tri)


HI_HALF = -65536


def _pack_bf16_pairs(x):
    w = x.shape[1] // 2
    bits = lambda a: lax.bitcast_convert_type(a.astype(BF16).astype(F32), I32)
    return lax.shift_right_logical(bits(x[:, :w]), 16) | (bits(x[:, w:]) & HI_HALF)


def _unpack_bf16_pairs(u):
    lo = lax.bitcast_convert_type(lax.shift_left(u, 16), F32)
    hi = lax.bitcast_convert_type(u & HI_HALF, F32)
    return jnp.concatenate([lo, hi], axis=1)


def _wait_rows(src, dst, sem, n):
    pltpu.make_async_copy(src.at[pl.ds(0, n)], dst.at[pl.ds(0, n)], sem).wait()


def _dispatch_kernel(lo_ref, n_ref, nu_ref, dest_ref, hp_ref, hs_ref, xs_hbm, rowbuf, zbuf, sem, zsem, *, tm,
                     n_prompt_tiles, rows):
    i = pl.program_id(0)
    n = pl.num_programs(0)
    par = i % 2
    n_blocks = xs_hbm.shape[0] // rows
    base = pl.multiple_of(par * tm, tm)

    @pl.when(i < n_prompt_tiles)
    def _():
        rowbuf[pl.ds(base, tm)] = hp_ref[...]

    @pl.when(i >= n_prompt_tiles)
    def _():
        rowbuf[pl.ds(base, tm)] = hs_ref[...]

    def issue(t, carry):
        for k in range(TOP_K):
            slot = dest_ref[0, 0, k * tm + t]
            pltpu.make_async_copy(rowbuf.at[pl.ds(base + t, 1)], xs_hbm.at[pl.ds(slot, 1)],
                                  sem.at[par]).start(priority=k % 2)
        return carry

    lax.fori_loop(0, tm, issue, 0)

    def tail_copy(e):
        return pltpu.make_async_copy(zbuf.at[pl.ds(0, n_ref[e])], xs_hbm.at[pl.ds(lo_ref[e], n_ref[e])],
                                     zsem.at[0])

    def block_copy(b):
        return pltpu.make_async_copy(zbuf, xs_hbm.at[pl.ds(b * rows, rows)], zsem.at[0])

    def for_each_fill(act):
        def tail(e, carry):
            @pl.when(n_ref[e] > 0)
            def _():
                act(tail_copy(e))
            return carry

        def block(b, carry):
            act(block_copy(b))
            return carry

        lax.fori_loop(0, N_EXPERTS, tail, 0)
        lax.fori_loop(nu_ref[0], n_blocks, block, 0)

    @pl.when(i == 0)
    def _():
        zbuf[...] = jnp.zeros(zbuf.shape, zbuf.dtype)
        for_each_fill(lambda c: c.start())

    def wait_tile(p):
        for _ in range(TOP_K):
            _wait_rows(rowbuf, xs_hbm, sem.at[p], tm)

    @pl.when(i >= 1)
    def _():
        wait_tile(1 - par)

    @pl.when(i == n - 1)
    def _():
        wait_tile(par)
        for_each_fill(lambda c: c.wait())


def _dispatch(pad_lo, pad_n, n_used, dest_tiles, hnp_p, hnp_s, n_slots, rows, tm):
    n_tiles = dest_tiles.shape[0]
    w = hnp_p.shape[2]
    npt, nst = hnp_p.shape[0] // tm, hnp_s.shape[0] // tm
    grid_spec = pltpu.PrefetchScalarGridSpec(
        num_scalar_prefetch=3,
        grid=(n_tiles,),
        in_specs=[pl.BlockSpec((1, 1, TOP_K * tm), lambda i, lo, n, nu: (i, 0, 0), memory_space=pltpu.SMEM),
                  pl.BlockSpec((tm, 1, w), lambda i, lo, n, nu: (jnp.minimum(i, npt - 1), 0, 0)),
                  pl.BlockSpec((tm, 1, w), lambda i, lo, n, nu: (jnp.clip(i - npt, 0, nst - 1), 0, 0))],
        out_specs=pl.BlockSpec(memory_space=pl.ANY),
        scratch_shapes=[pltpu.VMEM((2 * tm, 1, w), I32), pltpu.VMEM((rows, 1, w), I32),
                        pltpu.SemaphoreType.DMA((2,)), pltpu.SemaphoreType.DMA((1,))],
    )
    return pl.pallas_call(
        functools.partial(_dispatch_kernel, tm=tm, n_prompt_tiles=hnp_p.shape[0] // tm, rows=rows),
        out_shape=jax.ShapeDtypeStruct((n_slots, 1, w), I32),
        grid_spec=grid_spec,
        compiler_params=_cparams(("arbitrary",)),
        name="dispatch",
    )(pad_lo, pad_n, n_used, dest_tiles, hnp_p, hnp_s)


def _dot_by_columns(x, w_ref, tn):
    cols = [jnp.dot(x, w_ref[:, c:c + tn].astype(BF16), preferred_element_type=F32)
            for c in range(0, w_ref.shape[1], tn)]
    return jnp.concatenate(cols, axis=1)


def _expert_kernel(be_ref, nu_ref, xs_ref, wg0_ref, wg1_ref, wu0_ref, wu1_ref, wd0_ref, wd1_ref, ys_ref, x2d):
    b = pl.program_id(0)

    @pl.when(b < nu_ref[0])
    def _():
        x2d[...] = xs_ref[...].reshape(x2d.shape)
        x = _unpack_bf16_pairs(x2d[...]).astype(BF16)
        halves = lambda a, r0, r1: jnp.concatenate(
            [_dot_by_columns(a, r0, MXU_COLS), _dot_by_columns(a, r1, MXU_COLS)], axis=1)
        g = halves(x, wg0_ref, wg1_ref)
        u = halves(x, wu0_ref, wu1_ref)
        hmid = (_silu(g) * u).astype(BF16)
        y = halves(hmid, wd0_ref, wd1_ref)
        ys_ref[...] = _pack_bf16_pairs(y).reshape(ys_ref.shape)


def _experts(block_e, n_used, xs, w_gate, w_up, w_down, layer, rows):
    n_slots, _, w = xs.shape
    nblk = n_slots // rows
    d = 2 * w
    used = lambda b, nu: jnp.minimum(b, nu[0] - 1)
    wspec = lambda k, n, half: pl.BlockSpec((None, None, k, n // 2),
                                            lambda b, be, nu: (layer, be[used(b, nu)], 0, half))
    both = lambda k, n: [wspec(k, n, 0), wspec(k, n, 1)]
    slot_spec = pl.BlockSpec((rows, 1, w), lambda b, be, nu: (used(b, nu), 0, 0))
    grid_spec = pltpu.PrefetchScalarGridSpec(
        num_scalar_prefetch=2,
        grid=(nblk,),
        in_specs=[slot_spec] + both(d, EXPERT_FF) + both(d, EXPERT_FF) + both(EXPERT_FF, d),
        out_specs=slot_spec,
        scratch_shapes=[pltpu.VMEM((rows, w), I32)],
    )
    return pl.pallas_call(
        _expert_kernel,
        out_shape=jax.ShapeDtypeStruct((n_slots, 1, w), I32),
        grid_spec=grid_spec,
        input_output_aliases={2: 0},
        compiler_params=_cparams(("arbitrary",)),
        name="experts",
    )(block_e, n_used, xs, w_gate, w_gate, w_up, w_up, w_down, w_down)


def _slot_plan(idx_t, pos_t, counts, rows, n_blocks):
    nblk_e = (counts + rows - 1) // rows
    blk_end = jnp.cumsum(nblk_e)
    blk_start = blk_end - nblk_e
    is_e = idx_t[:, :, None] == jnp.arange(N_EXPERTS, dtype=I32)[None, None, :]
    dest = jnp.sum(jnp.where(is_e, (blk_start * rows)[None, None, :], 0), axis=-1) + pos_t
    blocks = jnp.arange(n_blocks, dtype=I32)
    block_e = jnp.minimum(jnp.sum((blk_end[None, :] <= blocks[:, None]).astype(I32), axis=1), N_EXPERTS - 1)
    pad_lo = (blk_start * rows + counts).astype(I32)
    pad_n = (nblk_e * rows - counts).astype(I32)
    return dest.astype(I32), block_e, blk_end[-1:].astype(I32), pad_lo, pad_n


def _dest_tiles(dest, tm):
    t = dest.shape[1]
    return dest.reshape(TOP_K, t // tm, tm).transpose(1, 0, 2).reshape(t // tm, 1, TOP_K * tm)


def _combine_kernel(dest_ref, ndest_ref, wt_ref, hn_ref, x1_ref, gt_ref, wg_ref, wu_ref, wd_ref, ys_hbm, o_ref,
                    ybuf, y2d, sem, *, per_row, tm):
    i = pl.program_id(0)
    n = pl.num_programs(0)
    par = i % 2
    per_tile = TOP_K * tm

    def gather(tbl_ref, tile_slot):
        def body(t, carry):
            for k in range(TOP_K):
                slot = tbl_ref[0, 0, k * tm + t]
                pltpu.make_async_copy(ys_hbm.at[pl.ds(slot, 1)],
                                      ybuf.at[pl.ds(tile_slot * per_tile + k * tm + t, 1)],
                                      sem.at[tile_slot]).start(priority=k % 2)
            return carry
        lax.fori_loop(0, tm, body, 0)

    @pl.when(i == 0)
    def _():
        gather(dest_ref, 0)

    h = hn_ref[...].astype(BF16)
    g = jnp.dot(h, wg_ref[...], preferred_element_type=F32)
    u = jnp.dot(h, wu_ref[...], preferred_element_type=F32)
    gt = _mod_rows(gt_ref, per_row)
    o_ref[...] = x1_ref[...] + gt * jnp.dot((_silu(g) * u).astype(BF16), wd_ref[...], preferred_element_type=F32)

    _wait_rows(ys_hbm, ybuf.at[pl.ds(par * per_tile, per_tile)], sem.at[par], per_tile)
    wt = wt_ref[...]
    rc = y2d.shape[1]
    for c in range(tm // rc):
        for t in range(c * rc, (c + 1) * rc):
            for k in range(TOP_K):
                slot = ndest_ref[0, 0, k * tm + t]
                pltpu.make_async_copy(ys_hbm.at[pl.ds(slot, 1)],
                                      ybuf.at[pl.ds((1 - par) * per_tile + k * tm + t, 1)],
                                      sem.at[1 - par]).start(priority=k % 2)
        acc = jnp.zeros((rc, o_ref.shape[1]), F32)
        for k in range(TOP_K):
            y2d[k] = ybuf[pl.ds(pl.multiple_of(par * per_tile + k * tm + c * rc, rc), rc)].reshape(y2d.shape[1:])
            acc = acc + wt[c * rc:(c + 1) * rc, k:k + 1] * _unpack_bf16_pairs(y2d[k])
        rows = slice(c * rc, (c + 1) * rc)
        o_ref[rows, :] = o_ref[rows, :] + (gt if gt.shape[0] == 1 else gt[rows]) * acc

    @pl.when(i == n - 1)
    def _():
        _wait_rows(ys_hbm, ybuf.at[pl.ds((1 - par) * per_tile, per_tile)], sem.at[1 - par], per_tile)


def _combine(ys, dest_tiles, wts, hn, x1, mod, mod_block_rows, mod_row_block, per_row, ws_g, ws_u, ws_d, tm):
    t, d = hn.shape
    w = ys.shape[2]
    n_tiles = t // tm
    row = lambda i: (i, 0)
    const = lambda shape: pl.BlockSpec(shape, lambda i: (0, 0))
    gt_spec = pl.BlockSpec((mod_block_rows, d), (lambda i: (i, 5)) if per_row else (lambda i: (mod_row_block, 5)))
    tile = lambda f: pl.BlockSpec((1, 1, TOP_K * tm), lambda i: (f(i), 0, 0), memory_space=pltpu.SMEM)
    return pl.pallas_call(
        functools.partial(_combine_kernel, per_row=per_row, tm=tm),
        out_shape=jax.ShapeDtypeStruct((t, d), F32),
        grid=(n_tiles,),
        in_specs=[tile(lambda i: i), tile(lambda i: jnp.minimum(i + 1, n_tiles - 1)),
                  pl.BlockSpec((tm, TOP_K), row), pl.BlockSpec((tm, d), row), pl.BlockSpec((tm, d), row),
                  gt_spec, const((d, EXPERT_FF)), const((d, EXPERT_FF)), const((EXPERT_FF, d)),
                  pl.BlockSpec(memory_space=pl.ANY)],
        out_specs=pl.BlockSpec((tm, d), row),
        scratch_shapes=[pltpu.VMEM((2 * TOP_K * tm, 1, w), I32), pltpu.VMEM((TOP_K, COMBINE_ROWS, w), I32),
                        pltpu.SemaphoreType.DMA((2,))],
        compiler_params=_cparams(("arbitrary",)),
        name="combine",
    )(dest_tiles, dest_tiles, wts, hn, x1, mod, ws_g, ws_u, ws_d, ys)


def _rope_tables(pos):
    half = HEAD_DIM_QK // 2
    inv = ROPE_THETA ** (-jnp.arange(half, dtype=F32) * 2.0 / HEAD_DIM_QK)
    ang = pos.astype(F32)[:, None] * inv[None, :]
    cos, sin = jnp.cos(ang), jnp.sin(ang)
    return jnp.tile(jnp.concatenate([cos, cos], axis=1), (1, 2)), jnp.tile(jnp.concatenate([-sin, sin], axis=1), (1, 2))


def _n_expert_blocks(n_assign, rows):
    return -(-(n_assign + N_EXPERTS * (rows - 1)) // rows)


def kernel(x_prompt, x_sample, c_prompt, c_sample, cache_k, cache_v, state_conv, page_table, w_ada, b_ada, g_norm1, g_norm2, w_in, g_q, g_k, lambda_q1, lambda_k1, lambda_q2, lambda_k2, g_sub, w_dw, b_dw, ln_g, ln_b, w_out, w_router, router_bias, w_e_gate, w_e_up, w_e_down, w_s_gate, w_s_up, w_s_down):
    layer = 0
    lam_init = 0.8 - 0.6 * math.exp(-0.3 * layer)
    d = D_MODEL
    tp = x_prompt.shape[1]
    ns = x_sample.shape[0]
    t_all = tp + ns
    xp = x_prompt.reshape(tp, d)
    xs = x_sample.reshape(ns, d)

    row2 = lambda a: a[layer].reshape(1, -1)
    w_in_bf = w_in[layer].astype(BF16)
    wo_a = w_out[layer, :ATTN_WIDTH].astype(BF16)
    wo_c = w_out[layer, ATTN_WIDTH:].astype(BF16)
    ws_g, ws_u, ws_d = (w[layer].astype(BF16) for w in (w_s_gate, w_s_up, w_s_down))
    wr_t = w_router[layer].T
    wr_hi = wr_t.astype(BF16)
    wr_lo = (wr_t - wr_hi.astype(F32)).astype(BF16)
    gq = jnp.tile(row2(g_q), (1, 2))
    gk = jnp.tile(row2(g_k), (1, 2))
    lane = jnp.arange(LANES)
    bd = (lane[:, None] // HEAD_DIM_QK == lane[None, :] // HEAD_DIM_QK).astype(BF16)
    lams = [row2(a) for a in (lambda_q1, lambda_k1, lambda_q2, lambda_k2)]

    c_all = jnp.concatenate([c_sample, c_prompt, jnp.zeros((SUBLANES - 1, d), F32)], axis=0)
    mod = _ada(c_all, w_ada[layer], row2(b_ada))
    p_mod = dict(mod_block_rows=SUBLANES, mod_row_block=ns // SUBLANES, per_row=False)
    s_mod = dict(mod_block_rows=ns, mod_row_block=0, per_row=True)

    cos_p, sin_p = _rope_tables(jnp.arange(tp, dtype=I32))
    cos_s, sin_s = _rope_tables(jnp.full((ns,), PAST_LEN, I32))

    q_p, k_p, kb_p, v_p, vb_p, glu_p = _inproj(xp, mod, g1=row2(g_norm1), w_in_bf=w_in_bf, gq=gq, gk=gk,
                                               cos=cos_p, sin=sin_p, bd=bd, tm=512, **p_mod)
    cy_p = _conv_prompt(glu_p, w_dw[layer], row2(b_dw), row2(ln_g), row2(ln_b))
    on_p = _flash_prompt(q_p, kb_p, vb_p, *lams, row2(g_sub), lam_init)

    q_s, k_s, _, v_s, _, glu_s = _inproj(xs, mod, g1=row2(g_norm1), w_in_bf=w_in_bf, gq=gq, gk=gk,
                                         cos=cos_s, sin=sin_s, bd=bd, tm=ns, **s_mod)
    cy_s = _conv_sample(state_conv[layer], glu_s, w_dw[layer], row2(b_dw), row2(ln_g), row2(ln_b))
    n_phys = cache_k.shape[1]
    ck_pages = jnp.transpose(cache_k, (0, 1, 3, 4, 5, 2)).reshape(-1, ATTN_WIDTH, PAGE_SIZE)
    cv_pages = cache_v.reshape(-1, PAGE_SIZE * N_HEADS, HEAD_DIM_V)
    to_cols = lambda a: jnp.pad(jnp.swapaxes(a.reshape(ns, 2 * N_HEADS, HEAD_DIM_QK), 1, 2),
                                ((0, 0), (0, 0), (0, LANES - 2 * N_HEADS)))
    qt_s = to_cols(q_s[0].astype(F32) + q_s[1].astype(F32))
    on_s = _attn_sample(page_table + layer * n_phys, qt_s, to_cols(k_s), v_s.reshape(ns, N_HEADS, HEAD_DIM_V),
                        *lams, row2(g_sub), ck_pages, cv_pages, lam_init).reshape(ns, ATTN_WIDTH)

    x1_p, hn_p, hnp_p = _outproj(on_p, cy_p, xp, mod, g2=row2(g_norm2), wo_a=wo_a, wo_c=wo_c, tm=256, **p_mod)
    x1_s, hn_s, hnp_s = _outproj(on_s, cy_s, xs, mod, g2=row2(g_norm2), wo_a=wo_a, wo_c=wo_c, tm=ns, **s_mod)

    tm_r = ns
    tri = (jnp.arange(tm_r)[:, None] < jnp.arange(tm_r)[None, :]).astype(BF16)
    idx_t, wts_t, pos_t, counts = _route(hn_p, hn_s, wr_hi, wr_lo, router_bias[layer].reshape(-1, 1), tri, tm_r)
    n_blocks = _n_expert_blocks(TOP_K * t_all, EXPERT_ROWS)
    dest, block_e, n_used, pad_lo, pad_n = _slot_plan(idx_t, pos_t, counts.reshape(-1), EXPERT_ROWS, n_blocks)

    slots_x = _dispatch(pad_lo, pad_n, n_used, _dest_tiles(dest, tm_r), hnp_p, hnp_s, n_blocks * EXPERT_ROWS,
                        EXPERT_ROWS, tm_r)
    slots_y = _experts(block_e, n_used, slots_x, w_e_gate, w_e_up, w_e_down, layer, EXPERT_ROWS)
    tm_c = 256
    y_p = _combine(slots_y, _dest_tiles(dest[:, :tp], tm_c), wts_t[:, :tp].T, hn_p, x1_p, mod,
                   ws_g=ws_g, ws_u=ws_u, ws_d=ws_d, tm=tm_c, **p_mod)
    y_s = _combine(slots_y, _dest_tiles(dest[:, tp:], ns), wts_t[:, tp:].T, hn_s, x1_s, mod,
                   ws_g=ws_g, ws_u=ws_u, ws_d=ws_d, tm=ns, **s_mod)

    conv_p = glu_p[tp - CONV_STATE:]
    conv_s = jnp.concatenate([state_conv[layer][:, 1:], glu_s[:, None, :]], axis=1)
    return (y_p.reshape(1, tp, d), y_s.reshape(ns, 1, d),
            k_p.reshape(1, 1, tp, N_HEADS, 2, HEAD_DIM_QK), v_p.reshape(1, 1, tp, N_HEADS, HEAD_DIM_V),
            conv_p.reshape(1, 1, CONV_STATE, CONV_CH),
            k_s.reshape(1, ns, 1, N_HEADS, 2, HEAD_DIM_QK), v_s.reshape(1, ns, 1, N_HEADS, HEAD_DIM_V),
            conv_s.reshape(1, ns, CONV_STATE, CONV_CH))
```
